```python
import math
import jax, jax.numpy as jnp
from jax import lax
import numpy as np

D_MODEL = 1024
BATCH = 8
SEQ = 4096
DEPTH = 4

N_MIXERS = 2
N_ATTN_LAYERS = (DEPTH + 1) // 2
N_SSM_LAYERS = DEPTH // 2

HEAD_DIM = 64
N_HEADS = D_MODEL // (2 * HEAD_DIM)
Q_BLOCK = 128

REL_BUCKETS = 32
REL_MAX_DIST = 128

GROUP_CH = 16
GROUPS = D_MODEL // GROUP_CH
SSM_STATE = 64
SSM_CHUNK = 128

D_FF = 2816
FFN_RESIDUAL = 0.5

PLE_DIM = 256

N_NORMS = 8
RMS_EPS = 1e-6
NEG_INF = -1e30

kernel_name = "hybrid_diffattn_s5_macaron_trunk"


def rmsnorm(x, g):
    xf = x.astype(jnp.float32)
    y = xf * lax.rsqrt(jnp.mean(xf * xf, axis=-1, keepdims=True) + RMS_EPS)
    return (y * g.astype(jnp.float32)).astype(x.dtype)


def swiglu(h, w_in, w_out):
    gu = h @ w_in
    return (jax.nn.silu(gu[..., :D_FF]) * gu[..., D_FF:]) @ w_out


def t5_bucket(n):
    n = jnp.maximum(n, 0)
    max_exact = REL_BUCKETS // 2
    nf = jnp.maximum(n, 1).astype(jnp.float32)
    large = max_exact + (jnp.log(nf / max_exact) / math.log(REL_MAX_DIST / max_exact)
                         * (REL_BUCKETS - max_exact)).astype(jnp.int32)
    large = jnp.minimum(large, REL_BUCKETS - 1)
    return jnp.where(n < max_exact, n, large)


def diff_attention(h, w_qkv, w_o, lam_vecs, subln_g, rel_bias, lambda_init):
    Bsz, S, _ = h.shape
    n_blk = S // Q_BLOCK
    qkv = h @ w_qkv
    q, k, v = jnp.split(qkv, 3, axis=-1)
    q = q.reshape(Bsz, S, N_HEADS, 2, HEAD_DIM).transpose(0, 2, 3, 1, 4) * (HEAD_DIM ** -0.5)
    k = k.reshape(Bsz, S, N_HEADS, 2, HEAD_DIM).transpose(0, 2, 3, 1, 4)
    v = v.reshape(Bsz, S, N_HEADS, 2 * HEAD_DIM).transpose(0, 2, 1, 3)

    lv = lam_vecs.astype(jnp.float32)
    lam = jnp.exp(jnp.sum(lv[0] * lv[1])) - jnp.exp(jnp.sum(lv[2] * lv[3])) + lambda_init

    q_blocks = jnp.moveaxis(q.reshape(Bsz, N_HEADS, 2, n_blk, Q_BLOCK, HEAD_DIM), 3, 0)
    k_pos = jnp.arange(S, dtype=jnp.int32)

    def block(args):
        q_blk, blk = args
        q_pos = blk * Q_BLOCK + jnp.arange(Q_BLOCK, dtype=jnp.int32)
        dist = q_pos[:, None] - k_pos[None, :]
        bias = rel_bias.astype(jnp.float32)[t5_bucket(dist)]
        bias = bias.reshape(Q_BLOCK, S, N_HEADS, 2).transpose(2, 3, 0, 1)
        s = jnp.einsum('bhmqd,bhmkd->bhmqk', q_blk, k).astype(jnp.float32) + bias
        s = jnp.where(dist >= 0, s, NEG_INF)
        prob = jax.nn.softmax(s, axis=-1)
        attn = prob[:, :, 0] - lam * prob[:, :, 1]
        return jnp.einsum('bhqk,bhkv->bhqv', attn.astype(v.dtype), v)

    o = lax.map(block, (q_blocks, jnp.arange(n_blk, dtype=jnp.int32)))
    o = jnp.moveaxis(o, 0, 2).reshape(Bsz, N_HEADS, S, 2 * HEAD_DIM)
    o = rmsnorm(o, subln_g) * (1.0 - lambda_init)
    o = o.transpose(0, 2, 1, 3).reshape(Bsz, S, D_MODEL)
    return o @ w_o


def s5_glu(h, lam_re, lam_im, log_dt, b_re, b_im, c_re, c_im, d_skip, w_glu, b_glu):
    Bsz, S, _ = h.shape
    n_chunks = S // SSM_CHUNK
    lam = lax.complex(lam_re.astype(jnp.float32), lam_im.astype(jnp.float32))
    dt = jnp.exp(log_dt.astype(jnp.float32))[:, None]
    lam_dt = lam * dt
    a_bar = jnp.exp(lam_dt)
    b = lax.complex(b_re.astype(jnp.float32), b_im.astype(jnp.float32))
    b_bar = ((a_bar - 1.0) / lam)[:, :, None] * b
    c = lax.complex(c_re.astype(jnp.float32), c_im.astype(jnp.float32))
    d_g = d_skip.astype(jnp.float32).reshape(GROUPS, GROUP_CH)
    steps = jnp.arange(1, SSM_CHUNK + 1, dtype=jnp.float32)
    a_pow = jnp.exp(steps[:, None, None] * lam_dt[None])

    u = h.astype(jnp.float32).reshape(Bsz, n_chunks, SSM_CHUNK, GROUPS, GROUP_CH)
    u = jnp.moveaxis(u, 1, 0)

    def combine(e1, e2):
        a1, x1 = e1
        a2, x2 = e2
        return a1 * a2, a2 * x1 + x2

    def chunk_step(state, u_c):
        bu = jnp.einsum('gph,bcgh->bcgp', b_bar, u_c.astype(jnp.complex64))
        a = jnp.broadcast_to(a_bar, bu.shape)
        _, xs = lax.associative_scan(combine, (a, bu), axis=1)
        xs = xs + a_pow[None] * state[:, None]
        y = jnp.einsum('ghp,bcgp->bcgh', c, xs).real + d_g * u_c
        return xs[:, -1], y

    state0 = jnp.zeros((Bsz, GROUPS, SSM_STATE), jnp.complex64)
    _, ys = lax.scan(chunk_step, state0, u)
    y = jnp.moveaxis(ys, 0, 1).reshape(Bsz, S, D_MODEL).astype(h.dtype)
    z = jax.nn.gelu(y) @ w_glu + b_glu
    return z[..., :D_MODEL] * jax.nn.sigmoid(z[..., D_MODEL:])


def setup_inputs(seed: int = 0) -> dict:
    key = jax.random.key(seed)
    ks = jax.random.split(key, 24)
    f32 = jnp.float32
    nrm = lambda k, shape, scale: jax.random.normal(k, shape, f32) * scale

    x = nrm(ks[0], (BATCH, SEQ, D_MODEL), 1.0)
    p = nrm(ks[1], (DEPTH, BATCH, SEQ, PLE_DIM), 1.0)
    norm_g = 1.0 + nrm(ks[2], (DEPTH, N_NORMS, D_MODEL), 0.05)
    ffn_w_in = nrm(ks[3], (DEPTH, 2, D_MODEL, 2 * D_FF), D_MODEL ** -0.5)
    ffn_w_out = nrm(ks[4], (DEPTH, 2, D_FF, D_MODEL), D_FF ** -0.5)

    attn_w_qkv = nrm(ks[5], (N_ATTN_LAYERS, D_MODEL, 3 * D_MODEL), D_MODEL ** -0.5)
    attn_w_o = nrm(ks[6], (N_ATTN_LAYERS, D_MODEL, D_MODEL), D_MODEL ** -0.5)
    attn_lam = nrm(ks[7], (N_ATTN_LAYERS, 4, HEAD_DIM), 0.1)
    attn_subln_g = 1.0 + nrm(ks[8], (N_ATTN_LAYERS, 2 * HEAD_DIM), 0.05)
    rel_bias = nrm(ks[9], (REL_BUCKETS, 2 * N_HEADS), 0.5)

    n_idx = jnp.arange(SSM_STATE, dtype=f32)
    ssm_lam_re = jnp.full((N_SSM_LAYERS, GROUPS, SSM_STATE), -0.5, f32) + nrm(ks[10], (N_SSM_LAYERS, GROUPS, SSM_STATE), 1e-3)
    ssm_lam_im = jnp.broadcast_to(math.pi * n_idx, (N_SSM_LAYERS, GROUPS, SSM_STATE)) + nrm(ks[11], (N_SSM_LAYERS, GROUPS, SSM_STATE), 1e-3)
    ssm_log_dt = jax.random.uniform(ks[12], (N_SSM_LAYERS, GROUPS), f32, math.log(1e-3), math.log(1e-1))
    b_scale = (2.0 * GROUP_CH) ** -0.5
    c_scale = (2.0 * SSM_STATE) ** -0.5
    ssm_b_re = nrm(ks[13], (N_SSM_LAYERS, GROUPS, SSM_STATE, GROUP_CH), b_scale)
    ssm_b_im = nrm(ks[14], (N_SSM_LAYERS, GROUPS, SSM_STATE, GROUP_CH), b_scale)
    ssm_c_re = nrm(ks[15], (N_SSM_LAYERS, GROUPS, GROUP_CH, SSM_STATE), c_scale)
    ssm_c_im = nrm(ks[16], (N_SSM_LAYERS, GROUPS, GROUP_CH, SSM_STATE), c_scale)
    ssm_d = nrm(ks[17], (N_SSM_LAYERS, D_MODEL), 1.0)
    ssm_w_glu = nrm(ks[18], (N_SSM_LAYERS, D_MODEL, 2 * D_MODEL), D_MODEL ** -0.5)
    ssm_b_glu = nrm(ks[19], (N_SSM_LAYERS, 2 * D_MODEL), 0.01)

    ple_w_proj = nrm(ks[20], (DEPTH, PLE_DIM, D_MODEL), PLE_DIM ** -0.5)
    ple_w_gate = nrm(ks[21], (DEPTH, D_MODEL, D_MODEL), D_MODEL ** -0.5)

    return {"x": x, "p": p, "norm_g": norm_g, "ffn_w_in": ffn_w_in, "ffn_w_out": ffn_w_out,
            "attn_w_qkv": attn_w_qkv, "attn_w_o": attn_w_o, "attn_lam": attn_lam,
            "attn_subln_g": attn_subln_g, "rel_bias": rel_bias,
            "ssm_lam_re": ssm_lam_re, "ssm_lam_im": ssm_lam_im, "ssm_log_dt": ssm_log_dt,
            "ssm_b_re": ssm_b_re, "ssm_b_im": ssm_b_im, "ssm_c_re": ssm_c_re, "ssm_c_im": ssm_c_im,
            "ssm_d": ssm_d, "ssm_w_glu": ssm_w_glu, "ssm_b_glu": ssm_b_glu,
            "ple_w_proj": ple_w_proj, "ple_w_gate": ple_w_gate}


def reference(x, p, norm_g, ffn_w_in, ffn_w_out, attn_w_qkv, attn_w_o, attn_lam, attn_subln_g,
              rel_bias, ssm_lam_re, ssm_lam_im, ssm_log_dt, ssm_b_re, ssm_b_im, ssm_c_re, ssm_c_im,
              ssm_d, ssm_w_glu, ssm_b_glu, ple_w_proj, ple_w_gate):
    for i in range(DEPTH):
        g = norm_g[i]
        x = x + FFN_RESIDUAL * rmsnorm(swiglu(rmsnorm(x, g[0]), ffn_w_in[i, 0], ffn_w_out[i, 0]), g[1])
        h = rmsnorm(x, g[2])
        j = i // N_MIXERS
        if i % N_MIXERS == 0:
            lambda_init = 0.8 - 0.6 * math.exp(-0.3 * i)
            m = diff_attention(h, attn_w_qkv[j], attn_w_o[j], attn_lam[j], attn_subln_g[j],
                               rel_bias, lambda_init)
        else:
            m = s5_glu(h, ssm_lam_re[j], ssm_lam_im[j], ssm_log_dt[j], ssm_b_re[j], ssm_b_im[j],
                       ssm_c_re[j], ssm_c_im[j], ssm_d[j], ssm_w_glu[j], ssm_b_glu[j])
        x = x + rmsnorm(m, g[3])
        x = x + FFN_RESIDUAL * rmsnorm(swiglu(rmsnorm(x, g[4]), ffn_w_in[i, 1], ffn_w_out[i, 1]), g[5])
        gate = jax.nn.sigmoid(rmsnorm(x, g[6]) @ ple_w_gate[i])
        x = x + rmsnorm(gate * (p[i] @ ple_w_proj[i]), g[7])
    return x
```

```python
import functools
import math

import numpy as np
import jax
import jax.numpy as jnp
from jax import lax
from jax.experimental import pallas as pl
from jax.experimental.pallas import tpu as pltpu

D_MODEL = 1024
BATCH = 8
SEQ = 4096
DEPTH = 4
N_MIXERS = 2
HEAD_DIM = 64
N_HEADS = D_MODEL // (2 * HEAD_DIM)
REL_BUCKETS = 32
REL_MAX_DIST = 128
GROUP_CH = 16
GROUPS = D_MODEL // GROUP_CH
SSM_STATE = 64
D_FF = 2816
FFN_RESIDUAL = 0.5
PLE_DIM = 256
RMS_EPS = 1e-6
NEG_INF = -1e30

TOKENS = BATCH * SEQ
F32 = jnp.float32
BF16 = jnp.bfloat16

V7X_VMEM_BYTES = 64 * 1024 * 1024
VMEM_LIMIT = V7X_VMEM_BYTES - 8 * 1024 * 1024

TM = 512
FF_CHUNK = D_FF // 2
TQ = 256
TK = 256
SSM_L = 16
SSM_NC = SEQ // SSM_L
SSM_W = SSM_L * GROUP_CH


def _t5_thresholds():
    n = np.arange(0, 4 * REL_MAX_DIST)
    max_exact = REL_BUCKETS // 2
    nf = np.maximum(n, 1).astype(np.float64)
    large = max_exact + (np.log(nf / max_exact) / math.log(REL_MAX_DIST / max_exact)
                         * (REL_BUCKETS - max_exact)).astype(np.int32)
    bucket = np.where(n < max_exact, n, np.minimum(large, REL_BUCKETS - 1))
    return [int(np.argmax(bucket >= j)) for j in range(1, REL_BUCKETS)]


T5_THRESHOLDS = _t5_thresholds()
assert T5_THRESHOLDS[-1] <= TK, "far key blocks must sit entirely in the last bucket"


def _const_spec(shape):
    nd = len(shape)
    return pl.BlockSpec(shape, lambda *_: (0,) * nd, pipeline_mode=pl.Buffered(1))


def _params(n_axes=1):
    return pltpu.CompilerParams(dimension_semantics=("arbitrary",) * n_axes,
                                vmem_limit_bytes=VMEM_LIMIT)


def _rms(x, g):
    return x * lax.rsqrt(jnp.mean(x * x, axis=-1, keepdims=True) + RMS_EPS) * g


def _sigmoid(x):
    return 1.0 / (1.0 + jnp.exp(-x))


def _dot(a, b):
    return jnp.dot(a, b, preferred_element_type=F32)


def _ffn_kernel(x_ref, g_ref, wg_ref, wu_ref, wo_ref, o_ref):
    x = x_ref[...]
    h = _rms(x, g_ref[0:1, :]).astype(BF16)
    y = jnp.zeros((TM, D_MODEL), F32)
    for c in range(D_FF // FF_CHUNK):
        sl = slice(c * FF_CHUNK, (c + 1) * FF_CHUNK)
        a = _dot(h, wg_ref[:, sl])
        u = _dot(h, wu_ref[:, sl])
        act = (a * _sigmoid(a) * u).astype(BF16)
        y = y + _dot(act, wo_ref[sl, :])
    o_ref[...] = x + FFN_RESIDUAL * _rms(y, g_ref[1:2, :])


def _ffn(x, g2, w_gate, w_up, w_out):
    return pl.pallas_call(
        _ffn_kernel,
        grid=(TOKENS // TM,),
        in_specs=[pl.BlockSpec((TM, D_MODEL), lambda t: (t, 0)),
                  _const_spec((2, D_MODEL)),
                  _const_spec((D_MODEL, D_FF)),
                  _const_spec((D_MODEL, D_FF)),
                  _const_spec((D_FF, D_MODEL))],
        out_specs=pl.BlockSpec((TM, D_MODEL), lambda t: (t, 0)),
        out_shape=jax.ShapeDtypeStruct((TOKENS, D_MODEL), F32),
        compiler_params=_params(),
        name="ffn",
    )(x, g2, w_gate, w_up, w_out)


def _qkv_kernel(x_ref, g_ref, w_ref, q_ref, k_ref, v_ref):
    h = _rms(x_ref[...], g_ref[...]).astype(BF16)
    qkv = _dot(h, w_ref[...])
    q_ref[...] = (qkv[:, :D_MODEL] * (HEAD_DIM ** -0.5)).astype(BF16)
    k_ref[...] = qkv[:, D_MODEL:2 * D_MODEL].astype(BF16)
    v_ref[...] = qkv[:, 2 * D_MODEL:].astype(BF16)


def _qkv(x, g, w_qkv):
    tile = pl.BlockSpec((TM, D_MODEL), lambda t: (t, 0))
    out = jax.ShapeDtypeStruct((TOKENS, D_MODEL), BF16)
    return pl.pallas_call(
        _qkv_kernel,
        grid=(TOKENS // TM,),
        in_specs=[tile, _const_spec((1, D_MODEL)), _const_spec((D_MODEL, 3 * D_MODEL))],
        out_specs=(tile, tile, tile),
        out_shape=(out, out, out),
        compiler_params=_params(),
        name="qkv",
    )(x, g, w_qkv)


def _bias_kernel(rel_ref, o_ref):
    c = pl.program_id(0)
    r = pl.program_id(1)
    qi = lax.broadcasted_iota(jnp.int32, (TQ, TK), 0)
    ki = lax.broadcasted_iota(jnp.int32, (TQ, TK), 1)
    d = qi - ki + r * TK
    val = jnp.full((TQ, TK), rel_ref[0, c], F32)
    for j, thr in enumerate(T5_THRESHOLDS, start=1):
        val = jnp.where(d >= thr, rel_ref[j, c], val)
    val = val - rel_ref[REL_BUCKETS - 1, c]
    o_ref[...] = jnp.where(d >= 0, val, NEG_INF)


def _bias_tiles(rel_bias):
    tiles = pl.pallas_call(
        _bias_kernel,
        grid=(2 * N_HEADS, 3),
        in_specs=[pl.BlockSpec(memory_space=pltpu.SMEM)],
        out_specs=pl.BlockSpec((None, None, TQ, TK), lambda c, r: (c, r, 0, 0)),
        out_shape=jax.ShapeDtypeStruct((2 * N_HEADS, 3, TQ, TK), F32),
        compiler_params=_params(2),
        name="t5_bias",
    )(rel_bias)
    return tiles.reshape(N_HEADS, 2, 3, TQ, TK)


def _attn_kernel(q_ref, k_ref, v_ref, bias_ref, lam_ref, sg_ref, o_ref, m_ref, l_ref, acc_ref,
                 *, lambda_init):
    i = pl.program_id(2)
    q = q_ref[...]
    lane = lax.broadcasted_iota(jnp.int32, q.shape, 1)
    zero = jnp.zeros_like(q)
    q_maps = (jnp.where(lane < HEAD_DIM, q, zero), jnp.where(lane >= HEAD_DIM, q, zero))

    m_ref[...] = jnp.full(m_ref.shape, NEG_INF, F32)
    l_ref[...] = jnp.zeros(l_ref.shape, F32)
    acc_ref[...] = jnp.zeros(acc_ref.shape, F32)

    def body(j, carry):
        off = pl.multiple_of(j * TK, TK)
        kb = k_ref[pl.ds(off, TK), :]
        vb = v_ref[pl.ds(off, TK), :]
        r = jnp.minimum(i - j, 2)
        for mi in range(2):
            s = lax.dot_general(q_maps[mi], kb, (((1,), (1,)), ((), ())),
                                preferred_element_type=F32)
            s = s + bias_ref[mi, r]
            m_prev = m_ref[mi]
            m_new = jnp.maximum(m_prev, jnp.max(s, axis=-1, keepdims=True))
            alpha = jnp.exp(m_prev - m_new)
            p = jnp.exp(s - m_new)
            l_ref[mi] = alpha * l_ref[mi] + jnp.sum(p, axis=-1, keepdims=True)
            acc_ref[mi] = alpha * acc_ref[mi] + _dot(p.astype(BF16), vb)
            m_ref[mi] = m_new
        return carry

    lax.fori_loop(0, i + 1, body, 0)

    lv = lam_ref[...]
    lam = (jnp.exp(jnp.sum(lv[0:1] * lv[1:2], keepdims=True))
           - jnp.exp(jnp.sum(lv[2:3] * lv[3:4], keepdims=True)) + lambda_init)
    o = acc_ref[0] / l_ref[0] - lam * (acc_ref[1] / l_ref[1])
    o = _rms(o, sg_ref[...]) * (1.0 - lambda_init)
    o_ref[...] = o.astype(BF16)


def _attention(q, k, v, bias, lam_vecs, subln_g, lambda_init):
    nq = SEQ // TQ
    kv_spec = pl.BlockSpec((SEQ, 2 * HEAD_DIM), lambda b, h, i: (b, h))
    return pl.pallas_call(
        functools.partial(_attn_kernel, lambda_init=lambda_init),
        grid=(BATCH, N_HEADS, nq),
        in_specs=[pl.BlockSpec((TQ, 2 * HEAD_DIM), lambda b, h, i: (b * nq + i, h)),
                  kv_spec, kv_spec,
                  pl.BlockSpec((None, 2, 3, TQ, TK), lambda b, h, i: (h, 0, 0, 0, 0)),
                  pl.BlockSpec((4, HEAD_DIM), lambda b, h, i: (0, 0)),
                  pl.BlockSpec((1, 2 * HEAD_DIM), lambda b, h, i: (0, 0))],
        out_specs=pl.BlockSpec((TQ, 2 * HEAD_DIM), lambda b, h, i: (b * nq + i, h)),
        out_shape=jax.ShapeDtypeStruct((TOKENS, D_MODEL), BF16),
        scratch_shapes=[pltpu.VMEM((2, TQ, 1), F32), pltpu.VMEM((2, TQ, 1), F32),
                        pltpu.VMEM((2, TQ, 2 * HEAD_DIM), F32)],
        compiler_params=_params(3),
        name="diff_attn",
    )(q, k, v, bias, lam_vecs, subln_g)


def _proj_kernel(a_ref, x_ref, w_ref, g_ref, o_ref):
    o_ref[...] = x_ref[...] + _rms(_dot(a_ref[...], w_ref[...]), g_ref[...])


def _out_proj(a, x, w, g):
    tile = pl.BlockSpec((TM, D_MODEL), lambda t: (t, 0))
    return pl.pallas_call(
        _proj_kernel,
        grid=(TOKENS // TM,),
        in_specs=[tile, tile, _const_spec((D_MODEL, D_MODEL)), _const_spec((1, D_MODEL))],
        out_specs=tile,
        out_shape=jax.ShapeDtypeStruct((TOKENS, D_MODEL), F32),
        compiler_params=_params(),
        name="attn_out",
    )(a, x, w, g)


def _norm_kernel(x_ref, g_ref, o_ref):
    o_ref[...] = _rms(x_ref[...], g_ref[...]).astype(BF16)


def _norm(x, g):
    tile = pl.BlockSpec((TM, D_MODEL), lambda t: (t, 0))
    return pl.pallas_call(
        _norm_kernel,
        grid=(TOKENS // TM,),
        in_specs=[tile, _const_spec((1, D_MODEL))],
        out_specs=tile,
        out_shape=jax.ShapeDtypeStruct((TOKENS, D_MODEL), BF16),
        compiler_params=_params(),
        name="ssm_norm",
    )(x, g)


def _ssm_operators(lam_re, lam_im, log_dt, b_re, b_im, c_re, c_im, d_skip):
    hp = lax.Precision.HIGHEST
    dt = jnp.exp(log_dt)[:, None]
    zr, zi = lam_re * dt, lam_im * dt
    ks = jnp.arange(SSM_L + 1, dtype=F32)[:, None, None]
    mag = jnp.exp(ks * zr)
    pr, pi = mag * jnp.cos(ks * zi), mag * jnp.sin(ks * zi)
    nr = jnp.expm1(zr) * jnp.cos(zi) - 2.0 * jnp.sin(0.5 * zi) ** 2
    ni = pi[1]
    den = lam_re * lam_re + lam_im * lam_im
    fr, fi = (nr * lam_re + ni * lam_im) / den, (ni * lam_re - nr * lam_im) / den
    bb_re = fr[..., None] * b_re - fi[..., None] * b_im
    bb_im = fr[..., None] * b_im + fi[..., None] * b_re
    ca_re = c_re[None] * pr[:, :, None, :] - c_im[None] * pi[:, :, None, :]
    ca_im = c_re[None] * pi[:, :, None, :] + c_im[None] * pr[:, :, None, :]

    kern = (jnp.einsum('tgop,gph->tgoh', ca_re[:SSM_L], bb_re, precision=hp)
            - jnp.einsum('tgop,gph->tgoh', ca_im[:SSM_L], bb_im, precision=hp))
    eye = jnp.eye(GROUP_CH, dtype=F32)
    kern = kern.at[0].add(d_skip.reshape(GROUPS, GROUP_CH)[:, :, None] * eye[None])
    kern = jnp.concatenate([kern, jnp.zeros_like(kern[:1])], axis=0)
    s_in = np.arange(SSM_L)[:, None]
    s_out = np.arange(SSM_L)[None, :]
    lag = np.where(s_out >= s_in, s_out - s_in, SSM_L)
    toep = kern[lag]
    toep = toep.transpose(2, 0, 4, 1, 3).reshape(GROUPS, SSM_W, SSM_W)

    pw_re, pw_im = pr[SSM_L - 1::-1][:SSM_L], pi[SSM_L - 1::-1][:SSM_L]
    bp_re = pw_re[..., None] * bb_re[None] - pw_im[..., None] * bb_im[None]
    bp_im = pw_re[..., None] * bb_im[None] + pw_im[..., None] * bb_re[None]
    lay = lambda t: t.transpose(1, 0, 3, 2).reshape(GROUPS, SSM_W, SSM_STATE)
    b_pow = jnp.zeros((GROUPS, SSM_W, 256), F32)
    b_pow = b_pow.at[:, :, 0:SSM_STATE].set(lay(bp_re)).at[:, :, 128:128 + SSM_STATE].set(lay(bp_im))

    lay_c = lambda t: t.transpose(1, 3, 0, 2).reshape(GROUPS, SSM_STATE, SSM_W)
    c_pow = jnp.zeros((GROUPS, 256, SSM_W), F32)
    c_pow = (c_pow.at[:, 0:SSM_STATE].set(lay_c(ca_re[1:]))
             .at[:, 128:128 + SSM_STATE].set(lay_c(-ca_im[1:])))

    a_chunk = jnp.zeros((GROUPS, 2, 128), F32)
    a_chunk = a_chunk.at[:, 0, :SSM_STATE].set(pr[SSM_L]).at[:, 1, :SSM_STATE].set(pi[SSM_L])
    return toep.astype(BF16), b_pow.astype(BF16), c_pow.astype(BF16), a_chunk


def _ssm_kernel(u_ref, toep_ref, bpow_ref, cpow_ref, a_ref, y_ref, s_ref, x_ref):
    u = u_ref[...]
    s_ref[...] = _dot(u, bpow_ref[...])
    ar = jnp.broadcast_to(a_ref[0:1, :], (BATCH, 128))
    ai = jnp.broadcast_to(a_ref[1:2, :], (BATCH, 128))

    def step(c, carry):
        xr, xi = carry
        r0 = pl.multiple_of(c * BATCH, BATCH)
        x_ref[pl.ds(r0, BATCH), 0:128] = xr
        x_ref[pl.ds(r0, BATCH), 128:256] = xi
        sr = s_ref[pl.ds(r0, BATCH), 0:128]
        si = s_ref[pl.ds(r0, BATCH), 128:256]
        return ar * xr - ai * xi + sr, ar * xi + ai * xr + si

    zeros = jnp.zeros((BATCH, 128), F32)
    lax.fori_loop(0, SSM_NC, step, (zeros, zeros), unroll=8)
    y_ref[...] = _dot(u, toep_ref[...]) + _dot(x_ref[...].astype(BF16), cpow_ref[...])


def _ssm_core(u_g, toep, b_pow, c_pow, a_chunk):
    rows = SSM_NC * BATCH
    per_group = lambda *tail: pl.BlockSpec((None,) + tail, lambda g: (g,) + (0,) * len(tail))
    return pl.pallas_call(
        _ssm_kernel,
        grid=(GROUPS,),
        in_specs=[per_group(rows, SSM_W), per_group(SSM_W, SSM_W), per_group(SSM_W, 256),
                  per_group(256, SSM_W), per_group(2, 128)],
        out_specs=per_group(rows, SSM_W),
        out_shape=jax.ShapeDtypeStruct((GROUPS, rows, SSM_W), F32),
        scratch_shapes=[pltpu.VMEM((rows, 256), F32), pltpu.VMEM((rows, 256), F32)],
        compiler_params=_params(),
        name="s5_chunks",
    )(u_g, toep, b_pow, c_pow, a_chunk)


def _gelu_tanh(x):
    return 0.5 * x * (1.0 + jnp.tanh(math.sqrt(2.0 / math.pi) * (x + 0.044715 * (x * x * x))))


def _glu_kernel(y_ref, x_ref, w_ref, b_ref, g_ref, o_ref):
    z = _dot(_gelu_tanh(y_ref[...]).astype(BF16), w_ref[...]) + b_ref[...]
    m = z[:, :D_MODEL] * _sigmoid(z[:, D_MODEL:])
    o_ref[...] = x_ref[...] + _rms(m, g_ref[...])


def _glu(y, x, w, b, g):
    tile = pl.BlockSpec((TM, D_MODEL), lambda t: (t, 0))
    return pl.pallas_call(
        _glu_kernel,
        grid=(TOKENS // TM,),
        in_specs=[tile, tile, _const_spec((D_MODEL, 2 * D_MODEL)), _const_spec((1, 2 * D_MODEL)),
                  _const_spec((1, D_MODEL))],
        out_specs=tile,
        out_shape=jax.ShapeDtypeStruct((TOKENS, D_MODEL), F32),
        compiler_params=_params(),
        name="s5_glu",
    )(y, x, w, b, g)


def _ple_kernel(x_ref, p_ref, g_ref, wg_ref, wp_ref, o_ref):
    x = x_ref[...]
    gate = _sigmoid(_dot(_rms(x, g_ref[0:1, :]).astype(BF16), wg_ref[...]))
    emb = _dot(p_ref[...].astype(BF16), wp_ref[...])
    o_ref[...] = x + _rms(gate * emb, g_ref[1:2, :])


def _ple(x, p, layer, g2, w_gate, w_proj):
    tile = pl.BlockSpec((TM, D_MODEL), lambda t: (t, 0))
    return pl.pallas_call(
        _ple_kernel,
        grid=(TOKENS // TM,),
        in_specs=[tile, pl.BlockSpec((None, TM, PLE_DIM), lambda t: (layer, t, 0)),
                  _const_spec((2, D_MODEL)), _const_spec((D_MODEL, D_MODEL)),
                  _const_spec((PLE_DIM, D_MODEL))],
        out_specs=tile,
        out_shape=jax.ShapeDtypeStruct((TOKENS, D_MODEL), F32),
        compiler_params=_params(),
        name="ple",
    )(x, p, g2, w_gate, w_proj)


def kernel(x, p, norm_g, ffn_w_in, ffn_w_out, attn_w_qkv, attn_w_o, attn_lam, attn_subln_g,
           rel_bias, ssm_lam_re, ssm_lam_im, ssm_log_dt, ssm_b_re, ssm_b_im, ssm_c_re, ssm_c_im,
           ssm_d, ssm_w_glu, ssm_b_glu, ple_w_proj, ple_w_gate):
    x = x.reshape(TOKENS, D_MODEL)
    p = p.reshape(DEPTH, TOKENS, PLE_DIM)
    bias = _bias_tiles(rel_bias)
    for i in range(DEPTH):
        g = norm_g[i]
        j = i // N_MIXERS

        w_in = ffn_w_in[i, 0].astype(BF16)
        x = _ffn(x, g[0:2], w_in[:, :D_FF], w_in[:, D_FF:], ffn_w_out[i, 0].astype(BF16))

        if i % N_MIXERS == 0:
            lambda_init = 0.8 - 0.6 * math.exp(-0.3 * i)
            q, k, v = _qkv(x, g[2:3], attn_w_qkv[j].astype(BF16))
            o = _attention(q, k, v, bias, attn_lam[j], attn_subln_g[j].reshape(1, -1), lambda_init)
            x = _out_proj(o, x, attn_w_o[j].astype(BF16), g[3:4])
        else:
            ops = _ssm_operators(ssm_lam_re[j], ssm_lam_im[j], ssm_log_dt[j], ssm_b_re[j],
                                 ssm_b_im[j], ssm_c_re[j], ssm_c_im[j], ssm_d[j])
            u = _norm(x, g[2:3])
            u_g = (u.reshape(BATCH, SSM_NC, SSM_L, GROUPS, GROUP_CH).transpose(3, 1, 0, 2, 4)
                   .reshape(GROUPS, SSM_NC * BATCH, SSM_W))
            y_g = _ssm_core(u_g, *ops)
            y = (y_g.reshape(GROUPS, SSM_NC, BATCH, SSM_L, GROUP_CH).transpose(2, 1, 3, 0, 4)
                 .reshape(TOKENS, D_MODEL))
            x = _glu(y, x, ssm_w_glu[j].astype(BF16), ssm_b_glu[j].reshape(1, -1), g[3:4])

        w_in = ffn_w_in[i, 1].astype(BF16)
        x = _ffn(x, g[4:6], w_in[:, :D_FF], w_in[:, D_FF:], ffn_w_out[i, 1].astype(BF16))
        x = _ple(x, p, i, g[6:8], ple_w_gate[i].astype(BF16), ple_w_proj[i].astype(BF16))
    return x.reshape(BATCH, SEQ, D_MODEL)
```

```python
import functools
import math

import numpy as np
import jax
import jax.numpy as jnp
from jax import lax
from jax.experimental import pallas as pl
from jax.experimental.pallas import tpu as pltpu

D_MODEL = 1024
BATCH = 8
SEQ = 4096
DEPTH = 4
N_MIXERS = 2
HEAD_DIM = 64
N_HEADS = D_MODEL // (2 * HEAD_DIM)
REL_BUCKETS = 32
REL_MAX_DIST = 128
GROUP_CH = 16
GROUPS = D_MODEL // GROUP_CH
SSM_STATE = 64
D_FF = 2816
FFN_RESIDUAL = 0.5
PLE_DIM = 256
RMS_EPS = 1e-6
NEG_INF = -1e30

TOKENS = BATCH * SEQ
F32 = jnp.float32
BF16 = jnp.bfloat16
LANES = 128

V7X_VMEM_BYTES = 64 * 1024 * 1024
VMEM_LIMIT = V7X_VMEM_BYTES - 8 * 1024 * 1024

TM = 512
TILES_PER_SEQ = SEQ // TM
FF_CHUNK = D_FF // 2
TQ = 512
TK = 512
BIAS_TILES = 2
LOG2E = math.log2(math.e)

SSM_L = 16
SSM_NC = SEQ // SSM_L
OCTETS = D_MODEL // LANES
OCT_GROUPS = LANES // GROUP_CH
OCT_W = SSM_L * LANES
OCT_STATE = OCT_GROUPS * SSM_STATE
SSM_ROWS = BATCH * SSM_NC
SSM_RT = 512
CH_PER_TILE = TM // SSM_L
RE_COLS = OCT_STATE // LANES
STATE_COLS = 2 * RE_COLS


def _t5_thresholds():
    n = np.arange(0, 4 * REL_MAX_DIST)
    max_exact = REL_BUCKETS // 2
    nf = np.maximum(n, 1).astype(np.float64)
    large = max_exact + (np.log(nf / max_exact) / math.log(REL_MAX_DIST / max_exact)
                         * (REL_BUCKETS - max_exact)).astype(np.int32)
    bucket = np.where(n < max_exact, n, np.minimum(large, REL_BUCKETS - 1))
    return [int(np.argmax(bucket >= j)) for j in range(1, REL_BUCKETS)]


T5_THRESHOLDS = _t5_thresholds()
assert TQ == TK and T5_THRESHOLDS[-1] <= TK, "key blocks before i-1 sit in the last bucket"

_NT = (((1,), (1,)), ((), ()))


def _const_spec(shape):
    nd = len(shape)
    return pl.BlockSpec(shape, lambda *_: (0,) * nd, pipeline_mode=pl.Buffered(1))


def _params(n_axes=1):
    return pltpu.CompilerParams(dimension_semantics=("arbitrary",) * n_axes,
                                vmem_limit_bytes=VMEM_LIMIT)


def _rms(x, g):
    return x * lax.rsqrt(jnp.mean(x * x, axis=-1, keepdims=True) + RMS_EPS) * g


def _sigmoid(x):
    return 1.0 / (1.0 + jnp.exp(-x))


def _dot(a, b):
    return jnp.dot(a, b, preferred_element_type=F32)


_TOKEN_TILE = pl.BlockSpec((TM, D_MODEL), lambda t: (t, 0))


def _ffn_kernel(x_ref, g_ref, wi_ref, wo_ref, o_ref):
    x = x_ref[...]
    h = _rms(x, g_ref[0:1, :]).astype(BF16)
    y = jnp.zeros((TM, D_MODEL), F32)
    for c in range(D_FF // FF_CHUNK):
        lo, hi = c * FF_CHUNK, (c + 1) * FF_CHUNK
        a = _dot(h, wi_ref[:, lo:hi])
        u = _dot(h, wi_ref[:, D_FF + lo:D_FF + hi])
        act = (a * _sigmoid(a) * u).astype(BF16)
        y = y + _dot(act, wo_ref[lo:hi, :])
    o_ref[...] = x + FFN_RESIDUAL * _rms(y, g_ref[1:2, :])


def _ffn(x, g2, w_in, w_out):
    return pl.pallas_call(
        _ffn_kernel,
        grid=(TOKENS // TM,),
        in_specs=[_TOKEN_TILE, _const_spec((2, D_MODEL)), _const_spec((D_MODEL, 2 * D_FF)),
                  _const_spec((D_FF, D_MODEL))],
        out_specs=_TOKEN_TILE,
        out_shape=jax.ShapeDtypeStruct((TOKENS, D_MODEL), F32),
        compiler_params=_params(),
        name="ffn",
    )(x, g2, w_in, w_out)


def _qkv_kernel(x_ref, g_ref, wqt_ref, wk_ref, wvt_ref, qt_ref, k_ref, vt_ref):
    h = _rms(x_ref[...], g_ref[...]).astype(BF16)
    qt = lax.dot_general(wqt_ref[...], h, _NT, preferred_element_type=F32)
    qt_ref[...] = (qt * (HEAD_DIM ** -0.5 * LOG2E)).astype(BF16)
    k_ref[...] = _dot(h, wk_ref[...]).astype(BF16)
    vt_ref[...] = lax.dot_general(wvt_ref[...], h, _NT, preferred_element_type=F32).astype(BF16)


def _qkv(x, g, wq_t, wk, wv_t):
    feat_tile = pl.BlockSpec((D_MODEL, TM), lambda t: (0, t))
    feat = jax.ShapeDtypeStruct((D_MODEL, TOKENS), BF16)
    w_spec = _const_spec((D_MODEL, D_MODEL))
    return pl.pallas_call(
        _qkv_kernel,
        grid=(TOKENS // TM,),
        in_specs=[_TOKEN_TILE, _const_spec((1, D_MODEL)), w_spec, w_spec, w_spec],
        out_specs=(feat_tile, _TOKEN_TILE, feat_tile),
        out_shape=(feat, jax.ShapeDtypeStruct((TOKENS, D_MODEL), BF16), feat),
        compiler_params=_params(),
        name="qkv",
    )(x, g, wq_t, wk, wv_t)


def _bias_kernel(rel_ref, o_ref):
    c = pl.program_id(0)
    r = pl.program_id(1)
    ki = lax.broadcasted_iota(jnp.int32, (TK, TQ), 0)
    qi = lax.broadcasted_iota(jnp.int32, (TK, TQ), 1)
    d = qi - ki + r * TK
    val = jnp.full((TK, TQ), rel_ref[0, c], F32)
    for j, thr in enumerate(T5_THRESHOLDS, start=1):
        val = jnp.where(d >= thr, rel_ref[j, c], val)
    val = (val - rel_ref[REL_BUCKETS - 1, c]) * LOG2E
    o_ref[...] = jnp.where(d >= 0, val, NEG_INF)


def _bias_tiles(rel_bias):
    tiles = pl.pallas_call(
        _bias_kernel,
        grid=(2 * N_HEADS, BIAS_TILES),
        in_specs=[pl.BlockSpec(memory_space=pltpu.SMEM)],
        out_specs=pl.BlockSpec((None, None, TK, TQ), lambda c, r: (c, r, 0, 0)),
        out_shape=jax.ShapeDtypeStruct((2 * N_HEADS, BIAS_TILES, TK, TQ), F32),
        compiler_params=_params(2),
        name="t5_bias",
    )(rel_bias)
    return tiles.reshape(N_HEADS, 2, BIAS_TILES, TK, TQ)


def _attn_kernel(qt_ref, k_ref, vt_ref, bias_ref, lam_ref, sg_ref, o_ref, m_ref, l_ref, acc_ref,
                 *, lambda_init):
    i = pl.program_id(2)
    qt = qt_ref[...]
    row = lax.broadcasted_iota(jnp.int32, qt.shape, 0)
    zero = jnp.zeros_like(qt)
    q_maps = (jnp.where(row < HEAD_DIM, qt, zero), jnp.where(row >= HEAD_DIM, qt, zero))

    m_ref[...] = jnp.full(m_ref.shape, NEG_INF, F32)
    l_ref[...] = jnp.zeros(l_ref.shape, F32)
    acc_ref[...] = jnp.zeros(acc_ref.shape, F32)

    def block(j, bias_tile):
        off = pl.multiple_of(j * TK, TK)
        kb = k_ref[pl.ds(off, TK), :]
        vb = vt_ref[:, pl.ds(off, TK)]
        for mi in range(2):
            s = _dot(kb, q_maps[mi])
            if bias_tile is not None:
                s = s + bias_ref[mi, bias_tile]
            m_prev = m_ref[mi]
            m_new = jnp.maximum(m_prev, jnp.max(s, axis=0, keepdims=True))
            alpha = jnp.exp2(m_prev - m_new)
            p = jnp.exp2(s - m_new)
            l_ref[mi] = alpha * l_ref[mi] + jnp.sum(p, axis=0, keepdims=True)
            acc_ref[mi] = alpha * acc_ref[mi] + _dot(vb, p.astype(BF16))
            m_ref[mi] = m_new

    def far_block(j, carry):
        block(j, None)
        return carry

    lax.fori_loop(0, i - 1, far_block, 0)

    @pl.when(i > 0)
    def _():
        block(i - 1, 1)

    block(i, 0)

    lv = lam_ref[...]
    lam = (jnp.exp(jnp.sum(lv[0:1] * lv[1:2], keepdims=True))
           - jnp.exp(jnp.sum(lv[2:3] * lv[3:4], keepdims=True)) + lambda_init)
    ot = acc_ref[0] * (1.0 / l_ref[0]) - lam * (acc_ref[1] * (1.0 / l_ref[1]))
    o = _rms(ot.T, sg_ref[...]) * (1.0 - lambda_init)
    o_ref[...] = o.astype(BF16)


def _attention(qt, k, vt, bias, lam_vecs, subln_g, lambda_init):
    nq = SEQ // TQ
    return pl.pallas_call(
        functools.partial(_attn_kernel, lambda_init=lambda_init),
        grid=(BATCH, N_HEADS, nq),
        in_specs=[pl.BlockSpec((2 * HEAD_DIM, TQ), lambda b, h, i: (h, b * nq + i)),
                  pl.BlockSpec((SEQ, 2 * HEAD_DIM), lambda b, h, i: (b, h)),
                  pl.BlockSpec((2 * HEAD_DIM, SEQ), lambda b, h, i: (h, b)),
                  pl.BlockSpec((None, 2, BIAS_TILES, TK, TQ), lambda b, h, i: (h, 0, 0, 0, 0)),
                  pl.BlockSpec((4, HEAD_DIM), lambda b, h, i: (0, 0)),
                  pl.BlockSpec((1, 2 * HEAD_DIM), lambda b, h, i: (0, 0))],
        out_specs=pl.BlockSpec((TQ, 2 * HEAD_DIM), lambda b, h, i: (b * nq + i, h)),
        out_shape=jax.ShapeDtypeStruct((TOKENS, D_MODEL), BF16),
        scratch_shapes=[pltpu.VMEM((2, 1, TQ), F32), pltpu.VMEM((2, 1, TQ), F32),
                        pltpu.VMEM((2, 2 * HEAD_DIM, TQ), F32)],
        compiler_params=_params(3),
        name="diff_attn",
    )(qt, k, vt, bias, lam_vecs, subln_g)


def _proj_kernel(a_ref, x_ref, w_ref, g_ref, o_ref):
    o_ref[...] = x_ref[...] + _rms(_dot(a_ref[...], w_ref[...]), g_ref[...])


def _out_proj(a, x, w, g):
    return pl.pallas_call(
        _proj_kernel,
        grid=(TOKENS // TM,),
        in_specs=[_TOKEN_TILE, _TOKEN_TILE, _const_spec((D_MODEL, D_MODEL)),
                  _const_spec((1, D_MODEL))],
        out_specs=_TOKEN_TILE,
        out_shape=jax.ShapeDtypeStruct((TOKENS, D_MODEL), F32),
        compiler_params=_params(),
        name="attn_out",
    )(a, x, w, g)


_SUPER_TILE = pl.BlockSpec((OCTETS, CH_PER_TILE, OCT_W),
                           lambda b, c: (0, b * TILES_PER_SEQ + c, 0))
_SEQ_TOKEN_TILE = pl.BlockSpec((TM, D_MODEL), lambda b, c: (b * TILES_PER_SEQ + c, 0))


def _ssm_norm_kernel(x_ref, g_ref, o_ref, h_scr):
    h = _rms(x_ref[...], g_ref[...])
    for j in range(OCTETS):
        h_scr[j] = h[:, j * LANES:(j + 1) * LANES]
    for j in range(OCTETS):
        for s in range(SSM_L):
            rows = h_scr[j, pl.ds(s, CH_PER_TILE, stride=SSM_L), :]
            o_ref[j, :, s * LANES:(s + 1) * LANES] = rows.astype(BF16)


def _ssm_norm(x, g):
    return pl.pallas_call(
        _ssm_norm_kernel,
        grid=(BATCH, TILES_PER_SEQ),
        in_specs=[_SEQ_TOKEN_TILE, pl.BlockSpec((1, D_MODEL), lambda b, c: (0, 0))],
        out_specs=_SUPER_TILE,
        out_shape=jax.ShapeDtypeStruct((OCTETS, SSM_ROWS, OCT_W), BF16),
        scratch_shapes=[pltpu.VMEM((OCTETS, TM, LANES), F32)],
        compiler_params=_params(2),
        name="ssm_norm",
    )(x, g)


def _ssm_operators(lam_re, lam_im, log_dt, b_re, b_im, c_re, c_im, d_skip):
    hp = lax.Precision.HIGHEST
    lam_re, lam_im, log_dt, b_re, b_im, c_re, c_im, d_skip = lax.optimization_barrier(
        (lam_re, lam_im, log_dt, b_re, b_im, c_re, c_im, d_skip))
    dt = jnp.exp(log_dt)[:, None]
    zr, zi = lam_re * dt, lam_im * dt
    ks = jnp.arange(SSM_L + 1, dtype=F32)[:, None, None]
    mag = jnp.exp(ks * zr)
    pr, pi = mag * jnp.cos(ks * zi), mag * jnp.sin(ks * zi)
    nr = jnp.expm1(zr) * jnp.cos(zi) - 2.0 * jnp.sin(0.5 * zi) ** 2
    ni = pi[1]
    den = lam_re * lam_re + lam_im * lam_im
    fr, fi = (nr * lam_re + ni * lam_im) / den, (ni * lam_re - nr * lam_im) / den
    bb_re = fr[..., None] * b_re - fi[..., None] * b_im
    bb_im = fr[..., None] * b_im + fi[..., None] * b_re
    ca_re = c_re[None] * pr[:, :, None, :] - c_im[None] * pi[:, :, None, :]
    ca_im = c_re[None] * pi[:, :, None, :] + c_im[None] * pr[:, :, None, :]

    kern = (jnp.einsum('tgop,gph->tgoh', ca_re[:SSM_L], bb_re, precision=hp)
            - jnp.einsum('tgop,gph->tgoh', ca_im[:SSM_L], bb_im, precision=hp))
    eye_h = jnp.eye(GROUP_CH, dtype=F32)
    kern = kern.at[0].add(d_skip.reshape(GROUPS, GROUP_CH)[:, :, None] * eye_h[None])
    toep = jnp.stack([jnp.pad(kern[:SSM_L - s], ((s, 0), (0, 0), (0, 0), (0, 0)))
                      for s in range(SSM_L)])

    same = jnp.eye(OCT_GROUPS, dtype=F32)[None, None, :, None, None, :, None]
    oct_split = lambda t, ax: t.reshape(t.shape[:ax] + (OCTETS, OCT_GROUPS) + t.shape[ax + 1:])
    block_diag = lambda t: lax.optimization_barrier(t)[:, :, :, :, :, None, :] * same

    toep = oct_split(toep, 2)
    toep = block_diag(toep.transpose(2, 0, 3, 5, 1, 4))
    toep = toep.reshape(OCTETS, OCT_W, OCT_W)

    kr = jnp.arange(SSM_L - 1, -1, -1, dtype=F32)[:, None, None]
    mag_r = jnp.exp(kr * zr)
    pw_re, pw_im = mag_r * jnp.cos(kr * zi), mag_r * jnp.sin(kr * zi)
    bp = jnp.stack([pw_re[..., None] * bb_re[None] - pw_im[..., None] * bb_im[None],
                    pw_re[..., None] * bb_im[None] + pw_im[..., None] * bb_re[None]])
    bp = oct_split(bp, 2).transpose(2, 1, 3, 5, 0, 4)
    b_pow = block_diag(bp).reshape(OCTETS, OCT_W, 2 * OCT_STATE)

    cp = oct_split(jnp.stack([ca_re[1:], -ca_im[1:]]), 2)
    cp = cp.transpose(2, 0, 3, 5, 1, 4)
    c_pow = block_diag(cp).reshape(OCTETS, 2 * OCT_STATE, OCT_W)

    a_chunk = jnp.stack([pr[SSM_L], pi[SSM_L]])
    a_chunk = oct_split(a_chunk, 1).transpose(1, 0, 2, 3).reshape(OCTETS, 2, OCT_STATE)
    return toep.astype(BF16), b_pow.astype(BF16), c_pow.astype(BF16), a_chunk


def _s5_state_kernel(u_ref, bpow_ref, a_ref, xs_ref, s_scr, x_scr):
    for r in range(0, SSM_ROWS, SSM_RT):
        s_loc = _dot(u_ref[r:r + SSM_RT, :], bpow_ref[...])
        for k in range(STATE_COLS):
            s_scr[k, r:r + SSM_RT, :] = s_loc[:, k * LANES:(k + 1) * LANES]
    a = a_ref[...]
    col = lambda r, k: jnp.broadcast_to(a[r:r + 1, k * LANES:(k + 1) * LANES], (BATCH, LANES))
    ar = [col(0, k) for k in range(RE_COLS)]
    ai = [col(1, k) for k in range(RE_COLS)]

    def step(c, carry):
        rows = pl.ds(c, BATCH, stride=SSM_NC)
        nxt_re, nxt_im = [], []
        for k in range(RE_COLS):
            xr, xi = carry[k], carry[RE_COLS + k]
            x_scr[k, rows, :] = xr
            x_scr[RE_COLS + k, rows, :] = xi
            nxt_re.append(ar[k] * xr - ai[k] * xi + s_scr[k, rows, :])
            nxt_im.append(ar[k] * xi + ai[k] * xr + s_scr[RE_COLS + k, rows, :])
        return tuple(nxt_re + nxt_im)

    zeros = jnp.zeros((BATCH, LANES), F32)
    lax.fori_loop(0, SSM_NC, step, (zeros,) * STATE_COLS, unroll=4)
    for k in range(STATE_COLS):
        xs_ref[:, k * LANES:(k + 1) * LANES] = x_scr[k].astype(BF16)


def _s5_states(u_oct, b_pow, a_chunk):
    per_oct = lambda *tail, **kw: pl.BlockSpec((None,) + tail, lambda j: (j,) + (0,) * len(tail),
                                               **kw)
    return pl.pallas_call(
        _s5_state_kernel,
        grid=(OCTETS,),
        in_specs=[per_oct(SSM_ROWS, OCT_W),
                  per_oct(OCT_W, 2 * OCT_STATE, pipeline_mode=pl.Buffered(1)),
                  per_oct(2, OCT_STATE)],
        out_specs=per_oct(SSM_ROWS, 2 * OCT_STATE),
        out_shape=jax.ShapeDtypeStruct((OCTETS, SSM_ROWS, 2 * OCT_STATE), BF16),
        scratch_shapes=[pltpu.VMEM((STATE_COLS, SSM_ROWS, LANES), F32),
                        pltpu.VMEM((STATE_COLS, SSM_ROWS, LANES), F32)],
        compiler_params=_params(),
        name="s5_states",
    )(u_oct, b_pow, a_chunk)


def _s5_out_kernel(u_ref, xs_ref, toep_ref, cpow_ref, y_ref):
    y = _dot(u_ref[...], toep_ref[...]) + _dot(xs_ref[...], cpow_ref[...])
    y_ref[...] = y.astype(BF16)


def _s5_output(u_oct, xs, toep, c_pow):
    rows = lambda w: pl.BlockSpec((None, SSM_RT, w), lambda j, r: (j, r, 0))
    per_oct = lambda h, w: pl.BlockSpec((None, h, w), lambda j, r: (j, 0, 0),
                                        pipeline_mode=pl.Buffered(1))
    return pl.pallas_call(
        _s5_out_kernel,
        grid=(OCTETS, SSM_ROWS // SSM_RT),
        in_specs=[rows(OCT_W), rows(2 * OCT_STATE), per_oct(OCT_W, OCT_W),
                  per_oct(2 * OCT_STATE, OCT_W)],
        out_specs=rows(OCT_W),
        out_shape=jax.ShapeDtypeStruct((OCTETS, SSM_ROWS, OCT_W), BF16),
        compiler_params=_params(2),
        name="s5_output",
    )(u_oct, xs, toep, c_pow)


def _gelu_tanh(x):
    return 0.5 * x * (1.0 + jnp.tanh(math.sqrt(2.0 / math.pi) * (x + 0.044715 * (x * x * x))))


def _glu_kernel(y_ref, x_ref, w_ref, b_ref, g_ref, o_ref, y_scr):
    for j in range(OCTETS):
        for s in range(SSM_L):
            y_scr[j, pl.ds(s, CH_PER_TILE, stride=SSM_L), :] = (
                y_ref[j, :, s * LANES:(s + 1) * LANES].astype(F32))
    y = jnp.concatenate([y_scr[j] for j in range(OCTETS)], axis=1)
    z = _dot(_gelu_tanh(y).astype(BF16), w_ref[...]) + b_ref[...]
    m = z[:, :D_MODEL] * _sigmoid(z[:, D_MODEL:])
    o_ref[...] = x_ref[...] + _rms(m, g_ref[...])


def _glu(y_oct, x, w, b, g):
    const = lambda shape: pl.BlockSpec(shape, lambda b, c: (0, 0), pipeline_mode=pl.Buffered(1))
    return pl.pallas_call(
        _glu_kernel,
        grid=(BATCH, TILES_PER_SEQ),
        in_specs=[_SUPER_TILE, _SEQ_TOKEN_TILE, const((D_MODEL, 2 * D_MODEL)),
                  const((1, 2 * D_MODEL)), const((1, D_MODEL))],
        out_specs=_SEQ_TOKEN_TILE,
        out_shape=jax.ShapeDtypeStruct((TOKENS, D_MODEL), F32),
        scratch_shapes=[pltpu.VMEM((OCTETS, TM, LANES), F32)],
        compiler_params=_params(2),
        name="s5_glu",
    )(y_oct, x, w, b, g)


def _ple_kernel(x_ref, p_ref, g_ref, wg_ref, wp_ref, o_ref):
    x = x_ref[...]
    gate = _sigmoid(_dot(_rms(x, g_ref[0:1, :]).astype(BF16), wg_ref[...]))
    emb = _dot(p_ref[...].astype(BF16), wp_ref[...])
    o_ref[...] = x + _rms(gate * emb, g_ref[1:2, :])


def _ple(x, p, layer, g2, w_gate, w_proj):
    return pl.pallas_call(
        _ple_kernel,
        grid=(TOKENS // TM,),
        in_specs=[_TOKEN_TILE, pl.BlockSpec((None, TM, PLE_DIM), lambda t: (layer, t, 0)),
                  _const_spec((2, D_MODEL)), _const_spec((D_MODEL, D_MODEL)),
                  _const_spec((PLE_DIM, D_MODEL))],
        out_specs=_TOKEN_TILE,
        out_shape=jax.ShapeDtypeStruct((TOKENS, D_MODEL), F32),
        compiler_params=_params(),
        name="ple",
    )(x, p, g2, w_gate, w_proj)


def kernel(x, p, norm_g, ffn_w_in, ffn_w_out, attn_w_qkv, attn_w_o, attn_lam, attn_subln_g,
           rel_bias, ssm_lam_re, ssm_lam_im, ssm_log_dt, ssm_b_re, ssm_b_im, ssm_c_re, ssm_c_im,
           ssm_d, ssm_w_glu, ssm_b_glu, ple_w_proj, ple_w_gate):
    x = x.reshape(TOKENS, D_MODEL)
    p = p.reshape(DEPTH, TOKENS, PLE_DIM)
    bias = _bias_tiles(rel_bias)
    for i in range(DEPTH):
        g = norm_g[i]
        j = i // N_MIXERS

        x = _ffn(x, g[0:2], ffn_w_in[i, 0].astype(BF16), ffn_w_out[i, 0].astype(BF16))

        if i % N_MIXERS == 0:
            lambda_init = 0.8 - 0.6 * math.exp(-0.3 * i)
            w = attn_w_qkv[j].astype(BF16)
            qt, k, vt = _qkv(x, g[2:3], w[:, :D_MODEL].T, w[:, D_MODEL:2 * D_MODEL],
                             w[:, 2 * D_MODEL:].T)
            o = _attention(qt, k, vt, bias, attn_lam[j], attn_subln_g[j].reshape(1, -1),
                           lambda_init)
            x = _out_proj(o, x, attn_w_o[j].astype(BF16), g[3:4])
        else:
            toep, b_pow, c_pow, a_chunk = _ssm_operators(
                ssm_lam_re[j], ssm_lam_im[j], ssm_log_dt[j], ssm_b_re[j], ssm_b_im[j],
                ssm_c_re[j], ssm_c_im[j], ssm_d[j])
            u_oct = _ssm_norm(x, g[2:3])
            xs = _s5_states(u_oct, b_pow, a_chunk)
            y_oct = _s5_output(u_oct, xs, toep, c_pow)
            x = _glu(y_oct, x, ssm_w_glu[j].astype(BF16), ssm_b_glu[j].reshape(1, -1), g[3:4])

        x = _ffn(x, g[4:6], ffn_w_in[i, 1].astype(BF16), ffn_w_out[i, 1].astype(BF16))
        x = _ple(x, p, i, g[6:8], ple_w_gate[i].astype(BF16), ple_w_proj[i].astype(BF16))
    return x.reshape(BATCH, SEQ, D_MODEL)
```

```python
import functools
import math

import numpy as np
import jax
import jax.numpy as jnp
from jax import lax
from jax.experimental import pallas as pl
from jax.experimental.pallas import tpu as pltpu

D_MODEL = 1024
BATCH = 8
SEQ = 4096
DEPTH = 4
N_MIXERS = 2
HEAD_DIM = 64
N_HEADS = D_MODEL // (2 * HEAD_DIM)
REL_BUCKETS = 32
REL_MAX_DIST = 128
GROUP_CH = 16
GROUPS = D_MODEL // GROUP_CH
SSM_STATE = 64
D_FF = 2816
FFN_RESIDUAL = 0.5
PLE_DIM = 256
RMS_EPS = 1e-6
NEG_INF = -1e30

TOKENS = BATCH * SEQ
F32 = jnp.float32
BF16 = jnp.bfloat16
LANES = 128

V7X_VMEM_BYTES = 64 * 1024 * 1024
VMEM_LIMIT = V7X_VMEM_BYTES - 8 * 1024 * 1024

TM = 512
TILES_PER_SEQ = SEQ // TM
FF_CHUNK = D_FF // 2
TQ = 512
TK = 512
BIAS_TILES = 2
LOG2E = math.log2(math.e)

SSM_L = 16
SSM_NC = SEQ // SSM_L
OCTETS = D_MODEL // LANES
OCT_GROUPS = LANES // GROUP_CH
OCT_W = SSM_L * LANES
OCT_STATE = OCT_GROUPS * SSM_STATE
SSM_ROWS = BATCH * SSM_NC
SSM_RT = 512
CH_PER_TILE = TM // SSM_L
RE_COLS = OCT_STATE // LANES
STATE_COLS = 2 * RE_COLS


def _t5_thresholds():
    n = np.arange(0, 4 * REL_MAX_DIST)
    max_exact = REL_BUCKETS // 2
    nf = np.maximum(n, 1).astype(np.float64)
    large = max_exact + (np.log(nf / max_exact) / math.log(REL_MAX_DIST / max_exact)
                         * (REL_BUCKETS - max_exact)).astype(np.int32)
    bucket = np.where(n < max_exact, n, np.minimum(large, REL_BUCKETS - 1))
    return [int(np.argmax(bucket >= j)) for j in range(1, REL_BUCKETS)]


T5_THRESHOLDS = _t5_thresholds()
assert TQ == TK and T5_THRESHOLDS[-1] <= TK, "key blocks before i-1 sit in the last bucket"

_NT = (((1,), (1,)), ((), ()))


def _const_spec(shape):
    nd = len(shape)
    return pl.BlockSpec(shape, lambda *_: (0,) * nd, pipeline_mode=pl.Buffered(1))


def _params(n_axes=1):
    return pltpu.CompilerParams(dimension_semantics=("arbitrary",) * n_axes,
                                vmem_limit_bytes=VMEM_LIMIT)


def _rms(x, g):
    return x * lax.rsqrt(jnp.mean(x * x, axis=-1, keepdims=True) + RMS_EPS) * g


def _sigmoid(x):
    return 1.0 / (1.0 + jnp.exp(-x))


def _dot(a, b):
    return jnp.dot(a, b, preferred_element_type=F32)


_TOKEN_TILE = pl.BlockSpec((TM, D_MODEL), lambda t: (t, 0))


def _ffn_kernel(x_ref, g_ref, wi_ref, wo_ref, o_ref):
    x = x_ref[...]
    h = _rms(x, g_ref[0:1, :]).astype(BF16)
    y = jnp.zeros((TM, D_MODEL), F32)
    for c in range(D_FF // FF_CHUNK):
        lo, hi = c * FF_CHUNK, (c + 1) * FF_CHUNK
        a = _dot(h, wi_ref[:, lo:hi])
        u = _dot(h, wi_ref[:, D_FF + lo:D_FF + hi])
        act = (a * _sigmoid(a) * u).astype(BF16)
        y = y + _dot(act, wo_ref[lo:hi, :])
    o_ref[...] = x + FFN_RESIDUAL * _rms(y, g_ref[1:2, :])


def _ffn(x, g2, w_in, w_out):
    return pl.pallas_call(
        _ffn_kernel,
        grid=(TOKENS // TM,),
        in_specs=[_TOKEN_TILE, _const_spec((2, D_MODEL)), _const_spec((D_MODEL, 2 * D_FF)),
                  _const_spec((D_FF, D_MODEL))],
        out_specs=_TOKEN_TILE,
        out_shape=jax.ShapeDtypeStruct((TOKENS, D_MODEL), F32),
        compiler_params=_params(),
        name="ffn",
    )(x, g2, w_in, w_out)


def _qkv_kernel(x_ref, g_ref, wqt_ref, wk_ref, wvt_ref, qt_ref, k_ref, vt_ref):
    h = _rms(x_ref[...], g_ref[...]).astype(BF16)
    qt = lax.dot_general(wqt_ref[...], h, _NT, preferred_element_type=F32)
    qt_ref[...] = (qt * (HEAD_DIM ** -0.5 * LOG2E)).astype(BF16)
    k_ref[...] = _dot(h, wk_ref[...]).astype(BF16)
    vt_ref[...] = lax.dot_general(wvt_ref[...], h, _NT, preferred_element_type=F32).astype(BF16)


def _qkv(x, g, wq_t, wk, wv_t):
    feat_tile = pl.BlockSpec((D_MODEL, TM), lambda t: (0, t))
    feat = jax.ShapeDtypeStruct((D_MODEL, TOKENS), BF16)
    w_spec = _const_spec((D_MODEL, D_MODEL))
    return pl.pallas_call(
        _qkv_kernel,
        grid=(TOKENS // TM,),
        in_specs=[_TOKEN_TILE, _const_spec((1, D_MODEL)), w_spec, w_spec, w_spec],
        out_specs=(feat_tile, _TOKEN_TILE, feat_tile),
        out_shape=(feat, jax.ShapeDtypeStruct((TOKENS, D_MODEL), BF16), feat),
        compiler_params=_params(),
        name="qkv",
    )(x, g, wq_t, wk, wv_t)


def _bias_kernel(rel_ref, o_ref):
    c = pl.program_id(0)
    r = pl.program_id(1)
    ki = lax.broadcasted_iota(jnp.int32, (TK, TQ), 0)
    qi = lax.broadcasted_iota(jnp.int32, (TK, TQ), 1)
    d = qi - ki + r * TK
    val = jnp.full((TK, TQ), rel_ref[0, c], F32)
    for j, thr in enumerate(T5_THRESHOLDS, start=1):
        val = jnp.where(d >= thr, rel_ref[j, c], val)
    val = (val - rel_ref[REL_BUCKETS - 1, c]) * LOG2E
    o_ref[...] = jnp.where(d >= 0, val, NEG_INF)


def _bias_tiles(rel_bias):
    tiles = pl.pallas_call(
        _bias_kernel,
        grid=(2 * N_HEADS, BIAS_TILES),
        in_specs=[pl.BlockSpec(memory_space=pltpu.SMEM)],
        out_specs=pl.BlockSpec((None, None, TK, TQ), lambda c, r: (c, r, 0, 0)),
        out_shape=jax.ShapeDtypeStruct((2 * N_HEADS, BIAS_TILES, TK, TQ), F32),
        compiler_params=_params(2),
        name="t5_bias",
    )(rel_bias)
    return tiles.reshape(N_HEADS, 2, BIAS_TILES, TK, TQ)


def _attn_kernel(qt_ref, k_ref, vt_ref, bias_ref, lam_ref, sg_ref, o_ref, m_ref, l_ref, acc_ref,
                 *, lambda_init):
    i = pl.program_id(2)
    qt = qt_ref[...]
    row = lax.broadcasted_iota(jnp.int32, qt.shape, 0)
    zero = jnp.zeros_like(qt)
    q_maps = (jnp.where(row < HEAD_DIM, qt, zero), jnp.where(row >= HEAD_DIM, qt, zero))

    m_ref[...] = jnp.full(m_ref.shape, NEG_INF, F32)
    l_ref[...] = jnp.zeros(l_ref.shape, F32)
    acc_ref[...] = jnp.zeros(acc_ref.shape, F32)

    def block(j, bias_tile):
        off = pl.multiple_of(j * TK, TK)
        kb = k_ref[pl.ds(off, TK), :]
        vb = vt_ref[:, pl.ds(off, TK)]
        for mi in range(2):
            s = _dot(kb, q_maps[mi])
            if bias_tile is not None:
                s = s + bias_ref[mi, bias_tile]
            m_prev = m_ref[mi]
            m_new = jnp.maximum(m_prev, jnp.max(s, axis=0, keepdims=True))
            alpha = jnp.exp2(m_prev - m_new)
            p = jnp.exp2(s - m_new)
            l_ref[mi] = alpha * l_ref[mi] + jnp.sum(p, axis=0, keepdims=True)
            acc_ref[mi] = alpha * acc_ref[mi] + _dot(vb, p.astype(BF16))
            m_ref[mi] = m_new

    def far_block(j, carry):
        block(j, None)
        return carry

    lax.fori_loop(0, i - 1, far_block, 0)

    @pl.when(i > 0)
    def _():
        block(i - 1, 1)

    block(i, 0)

    lv = lam_ref[...]
    lam = (jnp.exp(jnp.sum(lv[0:1] * lv[1:2], keepdims=True))
           - jnp.exp(jnp.sum(lv[2:3] * lv[3:4], keepdims=True)) + lambda_init)
    ot = acc_ref[0] * (1.0 / l_ref[0]) - lam * (acc_ref[1] * (1.0 / l_ref[1]))
    o = _rms(ot.T, sg_ref[...]) * (1.0 - lambda_init)
    o_ref[...] = o.astype(BF16)


def _attention(qt, k, vt, bias, lam_vecs, subln_g, lambda_init):
    nq = SEQ // TQ
    return pl.pallas_call(
        functools.partial(_attn_kernel, lambda_init=lambda_init),
        grid=(BATCH, N_HEADS, nq),
        in_specs=[pl.BlockSpec((2 * HEAD_DIM, TQ), lambda b, h, i: (h, b * nq + i)),
                  pl.BlockSpec((SEQ, 2 * HEAD_DIM), lambda b, h, i: (b, h)),
                  pl.BlockSpec((2 * HEAD_DIM, SEQ), lambda b, h, i: (h, b)),
                  pl.BlockSpec((None, 2, BIAS_TILES, TK, TQ), lambda b, h, i: (h, 0, 0, 0, 0)),
                  pl.BlockSpec((4, HEAD_DIM), lambda b, h, i: (0, 0)),
                  pl.BlockSpec((1, 2 * HEAD_DIM), lambda b, h, i: (0, 0))],
        out_specs=pl.BlockSpec((TQ, 2 * HEAD_DIM), lambda b, h, i: (b * nq + i, h)),
        out_shape=jax.ShapeDtypeStruct((TOKENS, D_MODEL), BF16),
        scratch_shapes=[pltpu.VMEM((2, 1, TQ), F32), pltpu.VMEM((2, 1, TQ), F32),
                        pltpu.VMEM((2, 2 * HEAD_DIM, TQ), F32)],
        compiler_params=_params(3),
        name="diff_attn",
    )(qt, k, vt, bias, lam_vecs, subln_g)


def _proj_kernel(a_ref, x_ref, w_ref, g_ref, o_ref):
    o_ref[...] = x_ref[...] + _rms(_dot(a_ref[...], w_ref[...]), g_ref[...])


def _out_proj(a, x, w, g):
    return pl.pallas_call(
        _proj_kernel,
        grid=(TOKENS // TM,),
        in_specs=[_TOKEN_TILE, _TOKEN_TILE, _const_spec((D_MODEL, D_MODEL)),
                  _const_spec((1, D_MODEL))],
        out_specs=_TOKEN_TILE,
        out_shape=jax.ShapeDtypeStruct((TOKENS, D_MODEL), F32),
        compiler_params=_params(),
        name="attn_out",
    )(a, x, w, g)


_SUPER_TILE = pl.BlockSpec((OCTETS, CH_PER_TILE, OCT_W),
                           lambda b, c: (0, b * TILES_PER_SEQ + c, 0))
_SEQ_TOKEN_TILE = pl.BlockSpec((TM, D_MODEL), lambda b, c: (b * TILES_PER_SEQ + c, 0))


def _ssm_norm_kernel(x_ref, g_ref, o_ref, h_scr):
    h = _rms(x_ref[...], g_ref[...])
    for j in range(OCTETS):
        h_scr[j] = h[:, j * LANES:(j + 1) * LANES]
    for j in range(OCTETS):
        for s in range(SSM_L):
            rows = h_scr[j, pl.ds(s, CH_PER_TILE, stride=SSM_L), :]
            o_ref[j, :, s * LANES:(s + 1) * LANES] = rows.astype(BF16)


def _ssm_norm(x, g):
    return pl.pallas_call(
        _ssm_norm_kernel,
        grid=(BATCH, TILES_PER_SEQ),
        in_specs=[_SEQ_TOKEN_TILE, pl.BlockSpec((1, D_MODEL), lambda b, c: (0, 0))],
        out_specs=_SUPER_TILE,
        out_shape=jax.ShapeDtypeStruct((OCTETS, SSM_ROWS, OCT_W), BF16),
        scratch_shapes=[pltpu.VMEM((OCTETS, TM, LANES), F32)],
        compiler_params=_params(2),
        name="ssm_norm",
    )(x, g)


def _ssm_operators(lam_re, lam_im, log_dt, b_re, b_im, c_re, c_im, d_skip):
    hp = lax.Precision.HIGHEST
    lam_re, lam_im, log_dt, b_re, b_im, c_re, c_im, d_skip = lax.optimization_barrier(
        (lam_re, lam_im, log_dt, b_re, b_im, c_re, c_im, d_skip))
    dt = jnp.exp(log_dt)[:, None]
    zr, zi = lam_re * dt, lam_im * dt
    ks = jnp.arange(SSM_L + 1, dtype=F32)[:, None, None]
    mag = jnp.exp(ks * zr)
    pr, pi = mag * jnp.cos(ks * zi), mag * jnp.sin(ks * zi)
    nr = jnp.expm1(zr) * jnp.cos(zi) - 2.0 * jnp.sin(0.5 * zi) ** 2
    ni = pi[1]
    den = lam_re * lam_re + lam_im * lam_im
    fr, fi = (nr * lam_re + ni * lam_im) / den, (ni * lam_re - nr * lam_im) / den
    bb_re = fr[..., None] * b_re - fi[..., None] * b_im
    bb_im = fr[..., None] * b_im + fi[..., None] * b_re
    ca_re = c_re[None] * pr[:SSM_L, :, None, :] - c_im[None] * pi[:SSM_L, :, None, :]
    ca_im = c_re[None] * pi[:SSM_L, :, None, :] + c_im[None] * pr[:SSM_L, :, None, :]

    kern = (jnp.einsum('tgop,gph->tgoh', ca_re, bb_re, precision=hp)
            - jnp.einsum('tgop,gph->tgoh', ca_im, bb_im, precision=hp))
    eye_h = jnp.eye(GROUP_CH, dtype=F32)
    kern = kern.at[0].add(d_skip.reshape(GROUPS, GROUP_CH)[:, :, None] * eye_h[None])

    same = jnp.eye(OCT_GROUPS, dtype=F32)
    kj = kern.reshape(SSM_L, OCTETS, OCT_GROUPS, GROUP_CH, GROUP_CH)
    kj = lax.optimization_barrier(kj.transpose(1, 2, 4, 0, 3))
    imp = kj[:, :, :, :, None, :] * same[None, :, None, None, :, None]
    imp = imp.reshape(OCTETS, LANES, OCT_W)

    def block_rows(t):
        t = lax.optimization_barrier(t)
        return (t[:, :, :, None, :] * same[None, :, None, :, None]).reshape(OCTETS, LANES, OCT_STATE)

    b_rows = lambda t: block_rows(t.reshape(OCTETS, OCT_GROUPS, SSM_STATE, GROUP_CH)
                                  .transpose(0, 1, 3, 2))
    c_rows = lambda t: block_rows(t.reshape(OCTETS, OCT_GROUPS, GROUP_CH, SSM_STATE))
    bmat = jnp.stack([b_rows(bb_re), b_rows(bb_im)], axis=1)
    cmat = jnp.stack([c_rows(c_re), c_rows(c_im)], axis=1)

    kr = jnp.arange(SSM_L - 1, -1, -1, dtype=F32)[:, None, None]
    mag_r = jnp.exp(kr * zr)
    per_oct = lambda t: t.reshape(t.shape[:-2] + (OCTETS, OCT_STATE))
    pw_b = per_oct(jnp.stack([mag_r * jnp.cos(kr * zi), mag_r * jnp.sin(kr * zi)]))
    pw_c = per_oct(jnp.stack([pr[1:], pi[1:]]))
    a_chunk = per_oct(jnp.stack([pr[SSM_L], pi[SSM_L]]))
    return (imp.astype(BF16), bmat, cmat, pw_b.transpose(2, 0, 1, 3), pw_c.transpose(2, 0, 1, 3),
            a_chunk.transpose(1, 0, 2))


def _scaled_blocks(mat_ref, pw_ref, out_ref, im_sign):
    m_re, m_im = mat_ref[0], mat_ref[1]
    for s in range(SSM_L):
        p_re, p_im = pw_ref[0, s:s + 1, :], pw_ref[1, s:s + 1, :]
        rows = slice(s * LANES, (s + 1) * LANES)
        out_ref[rows, 0:OCT_STATE] = (m_re * p_re - m_im * p_im).astype(BF16)
        out_ref[rows, OCT_STATE:2 * OCT_STATE] = (im_sign * (m_re * p_im + m_im * p_re)).astype(BF16)


def _s5_state_kernel(u_ref, bmat_ref, pw_ref, a_ref, xs_ref, bpow_scr, s_scr, x_scr):
    _scaled_blocks(bmat_ref, pw_ref, bpow_scr, 1.0)
    for r in range(0, SSM_ROWS, SSM_RT):
        s_loc = _dot(u_ref[r:r + SSM_RT, :], bpow_scr[...])
        for k in range(STATE_COLS):
            s_scr[k, r:r + SSM_RT, :] = s_loc[:, k * LANES:(k + 1) * LANES]
    a = a_ref[...]
    col = lambda r, k: jnp.broadcast_to(a[r:r + 1, k * LANES:(k + 1) * LANES], (BATCH, LANES))
    ar = [col(0, k) for k in range(RE_COLS)]
    ai = [col(1, k) for k in range(RE_COLS)]

    def step(c, carry):
        rows = pl.ds(c, BATCH, stride=SSM_NC)
        nxt_re, nxt_im = [], []
        for k in range(RE_COLS):
            xr, xi = carry[k], carry[RE_COLS + k]
            x_scr[k, rows, :] = xr
            x_scr[RE_COLS + k, rows, :] = xi
            nxt_re.append(ar[k] * xr - ai[k] * xi + s_scr[k, rows, :])
            nxt_im.append(ar[k] * xi + ai[k] * xr + s_scr[RE_COLS + k, rows, :])
        return tuple(nxt_re + nxt_im)

    zeros = jnp.zeros((BATCH, LANES), F32)
    lax.fori_loop(0, SSM_NC, step, (zeros,) * STATE_COLS, unroll=4)
    for k in range(STATE_COLS):
        xs_ref[:, k * LANES:(k + 1) * LANES] = x_scr[k].astype(BF16)


def _s5_states(u_oct, bmat, pw_b, a_chunk):
    per_oct = lambda *tail: pl.BlockSpec((None,) + tail, lambda j: (j,) + (0,) * len(tail))
    return pl.pallas_call(
        _s5_state_kernel,
        grid=(OCTETS,),
        in_specs=[per_oct(SSM_ROWS, OCT_W), per_oct(2, LANES, OCT_STATE),
                  per_oct(2, SSM_L, OCT_STATE), per_oct(2, OCT_STATE)],
        out_specs=per_oct(SSM_ROWS, 2 * OCT_STATE),
        out_shape=jax.ShapeDtypeStruct((OCTETS, SSM_ROWS, 2 * OCT_STATE), BF16),
        scratch_shapes=[pltpu.VMEM((OCT_W, 2 * OCT_STATE), BF16),
                        pltpu.VMEM((STATE_COLS, SSM_ROWS, LANES), F32),
                        pltpu.VMEM((STATE_COLS, SSM_ROWS, LANES), F32)],
        compiler_params=_params(),
        name="s5_states",
    )(u_oct, bmat, pw_b, a_chunk)


def _s5_out_kernel(u_ref, xs_ref, imp_ref, cmat_ref, pw_ref, y_ref, toep_scr, cpow_scr):
    @pl.when(pl.program_id(1) == 0)
    def _():
        toep_scr[...] = jnp.zeros(toep_scr.shape, BF16)
        for s in range(SSM_L):
            toep_scr[s * LANES:(s + 1) * LANES, s * LANES:] = imp_ref[:, :OCT_W - s * LANES]
        _scaled_blocks(cmat_ref, pw_ref, cpow_scr, -1.0)

    y = _dot(u_ref[...], toep_scr[...]) + lax.dot_general(
        xs_ref[...], cpow_scr[...], _NT, preferred_element_type=F32)
    y_ref[...] = y.astype(BF16)


def _s5_output(u_oct, xs, imp, cmat, pw_c):
    rows = lambda w: pl.BlockSpec((None, SSM_RT, w), lambda j, r: (j, r, 0))
    per_oct = lambda *tail: pl.BlockSpec((None,) + tail, lambda j, r: (j,) + (0,) * len(tail))
    return pl.pallas_call(
        _s5_out_kernel,
        grid=(OCTETS, SSM_ROWS // SSM_RT),
        in_specs=[rows(OCT_W), rows(2 * OCT_STATE), per_oct(LANES, OCT_W),
                  per_oct(2, LANES, OCT_STATE), per_oct(2, SSM_L, OCT_STATE)],
        out_specs=rows(OCT_W),
        out_shape=jax.ShapeDtypeStruct((OCTETS, SSM_ROWS, OCT_W), BF16),
        scratch_shapes=[pltpu.VMEM((OCT_W, OCT_W), BF16), pltpu.VMEM((OCT_W, 2 * OCT_STATE), BF16)],
        compiler_params=_params(2),
        name="s5_output",
    )(u_oct, xs, imp, cmat, pw_c)


def _gelu_tanh(x):
    return 0.5 * x * (1.0 + jnp.tanh(math.sqrt(2.0 / math.pi) * (x + 0.044715 * (x * x * x))))


def _glu_kernel(y_ref, x_ref, w_ref, b_ref, g_ref, o_ref, y_scr):
    for j in range(OCTETS):
        for s in range(SSM_L):
            y_scr[j, pl.ds(s, CH_PER_TILE, stride=SSM_L), :] = (
                y_ref[j, :, s * LANES:(s + 1) * LANES].astype(F32))
    y = jnp.concatenate([y_scr[j] for j in range(OCTETS)], axis=1)
    z = _dot(_gelu_tanh(y).astype(BF16), w_ref[...]) + b_ref[...]
    m = z[:, :D_MODEL] * _sigmoid(z[:, D_MODEL:])
    o_ref[...] = x_ref[...] + _rms(m, g_ref[...])


def _glu(y_oct, x, w, b, g):
    const = lambda shape: pl.BlockSpec(shape, lambda b, c: (0, 0), pipeline_mode=pl.Buffered(1))
    return pl.pallas_call(
        _glu_kernel,
        grid=(BATCH, TILES_PER_SEQ),
        in_specs=[_SUPER_TILE, _SEQ_TOKEN_TILE, const((D_MODEL, 2 * D_MODEL)),
                  const((1, 2 * D_MODEL)), const((1, D_MODEL))],
        out_specs=_SEQ_TOKEN_TILE,
        out_shape=jax.ShapeDtypeStruct((TOKENS, D_MODEL), F32),
        scratch_shapes=[pltpu.VMEM((OCTETS, TM, LANES), F32)],
        compiler_params=_params(2),
        name="s5_glu",
    )(y_oct, x, w, b, g)


def _ple_kernel(x_ref, p_ref, g_ref, wg_ref, wp_ref, o_ref):
    x = x_ref[...]
    gate = _sigmoid(_dot(_rms(x, g_ref[0:1, :]).astype(BF16), wg_ref[...]))
    emb = _dot(p_ref[...].astype(BF16), wp_ref[...])
    o_ref[...] = x + _rms(gate * emb, g_ref[1:2, :])


def _ple(x, p, layer, g2, w_gate, w_proj):
    return pl.pallas_call(
        _ple_kernel,
        grid=(TOKENS // TM,),
        in_specs=[_TOKEN_TILE, pl.BlockSpec((None, TM, PLE_DIM), lambda t: (layer, t, 0)),
                  _const_spec((2, D_MODEL)), _const_spec((D_MODEL, D_MODEL)),
                  _const_spec((PLE_DIM, D_MODEL))],
        out_specs=_TOKEN_TILE,
        out_shape=jax.ShapeDtypeStruct((TOKENS, D_MODEL), F32),
        compiler_params=_params(),
        name="ple",
    )(x, p, g2, w_gate, w_proj)


def kernel(x, p, norm_g, ffn_w_in, ffn_w_out, attn_w_qkv, attn_w_o, attn_lam, attn_subln_g,
           rel_bias, ssm_lam_re, ssm_lam_im, ssm_log_dt, ssm_b_re, ssm_b_im, ssm_c_re, ssm_c_im,
           ssm_d, ssm_w_glu, ssm_b_glu, ple_w_proj, ple_w_gate):
    x = x.reshape(TOKENS, D_MODEL)
    p = p.reshape(DEPTH, TOKENS, PLE_DIM)
    bias = _bias_tiles(rel_bias)
    for i in range(DEPTH):
        g = norm_g[i]
        j = i // N_MIXERS

        x = _ffn(x, g[0:2], ffn_w_in[i, 0].astype(BF16), ffn_w_out[i, 0].astype(BF16))

        if i % N_MIXERS == 0:
            lambda_init = 0.8 - 0.6 * math.exp(-0.3 * i)
            w = attn_w_qkv[j].astype(BF16)
            qt, k, vt = _qkv(x, g[2:3], w[:, :D_MODEL].T, w[:, D_MODEL:2 * D_MODEL],
                             w[:, 2 * D_MODEL:].T)
            o = _attention(qt, k, vt, bias, attn_lam[j], attn_subln_g[j].reshape(1, -1),
                           lambda_init)
            x = _out_proj(o, x, attn_w_o[j].astype(BF16), g[3:4])
        else:
            imp, bmat, cmat, pw_b, pw_c, a_chunk = _ssm_operators(
                ssm_lam_re[j], ssm_lam_im[j], ssm_log_dt[j], ssm_b_re[j], ssm_b_im[j],
                ssm_c_re[j], ssm_c_im[j], ssm_d[j])
            u_oct = _ssm_norm(x, g[2:3])
            xs = _s5_states(u_oct, bmat, pw_b, a_chunk)
            y_oct = _s5_output(u_oct, xs, imp, cmat, pw_c)
            x = _glu(y_oct, x, ssm_w_glu[j].astype(BF16), ssm_b_glu[j].reshape(1, -1), g[3:4])

        x = _ffn(x, g[4:6], ffn_w_in[i, 1].astype(BF16), ffn_w_out[i, 1].astype(BF16))
        x = _ple(x, p, i, g[6:8], ple_w_gate[i].astype(BF16), ple_w_proj[i].astype(BF16))
    return x.reshape(BATCH, SEQ, D_MODEL)
```

```python
import functools
import math

import numpy as np
import jax
import jax.numpy as jnp
from jax import lax
from jax.experimental import pallas as pl
from jax.experimental.pallas import tpu as pltpu

D_MODEL = 1024
BATCH = 8
SEQ = 4096
DEPTH = 4
N_MIXERS = 2
HEAD_DIM = 64
N_HEADS = D_MODEL // (2 * HEAD_DIM)
REL_BUCKETS = 32
REL_MAX_DIST = 128
GROUP_CH = 16
GROUPS = D_MODEL // GROUP_CH
SSM_STATE = 64
D_FF = 2816
FFN_RESIDUAL = 0.5
PLE_DIM = 256
RMS_EPS = 1e-6
NEG_INF = -1e30

TOKENS = BATCH * SEQ
F32 = jnp.float32
BF16 = jnp.bfloat16
LANES = 128

V7X_VMEM_BYTES = 64 * 1024 * 1024
VMEM_LIMIT = V7X_VMEM_BYTES - 8 * 1024 * 1024

TM = 512
TILES_PER_SEQ = SEQ // TM
FF_CHUNK = D_FF // 2
TQ = 512
TK = 512
BIAS_TILES = 3
TQ_HALF = TQ // 2
SUM_ROWS = 16
LOG2E = math.log2(math.e)

SSM_L = 16
SSM_NC = SEQ // SSM_L
OCTETS = D_MODEL // LANES
OCT_GROUPS = LANES // GROUP_CH
OCT_W = SSM_L * LANES
OCT_STATE = OCT_GROUPS * SSM_STATE
SSM_ROWS = BATCH * SSM_NC
SSM_RT = 512
CH_PER_TILE = TM // SSM_L
RE_COLS = OCT_STATE // LANES
STATE_COLS = 2 * RE_COLS


def _t5_thresholds():
    n = np.arange(0, 4 * REL_MAX_DIST)
    max_exact = REL_BUCKETS // 2
    nf = np.maximum(n, 1).astype(np.float64)
    large = max_exact + (np.log(nf / max_exact) / math.log(REL_MAX_DIST / max_exact)
                         * (REL_BUCKETS - max_exact)).astype(np.int32)
    bucket = np.where(n < max_exact, n, np.minimum(large, REL_BUCKETS - 1))
    return [int(np.argmax(bucket >= j)) for j in range(1, REL_BUCKETS)]


T5_THRESHOLDS = _t5_thresholds()
assert TQ == TK and T5_THRESHOLDS[-1] <= TK, "key blocks before i-1 sit in the last bucket"

_NT = (((1,), (1,)), ((), ()))


def _const_spec(shape):
    nd = len(shape)
    return pl.BlockSpec(shape, lambda *_: (0,) * nd, pipeline_mode=pl.Buffered(1))


def _params(n_axes=1):
    return pltpu.CompilerParams(dimension_semantics=("arbitrary",) * n_axes,
                                vmem_limit_bytes=VMEM_LIMIT)


def _rms(x, g):
    return x * lax.rsqrt(jnp.mean(x * x, axis=-1, keepdims=True) + RMS_EPS) * g


def _sigmoid(x):
    return 1.0 / (1.0 + jnp.exp(-x))


def _dot(a, b):
    return jnp.dot(a, b, preferred_element_type=F32)


_TOKEN_TILE = pl.BlockSpec((TM, D_MODEL), lambda t: (t, 0))


def _ffn_kernel(x_ref, g_ref, wi_ref, wo_ref, o_ref):
    x = x_ref[...]
    h = _rms(x, g_ref[0:1, :]).astype(BF16)
    y = jnp.zeros((TM, D_MODEL), F32)
    for c in range(D_FF // FF_CHUNK):
        lo, hi = c * FF_CHUNK, (c + 1) * FF_CHUNK
        a = _dot(h, wi_ref[:, lo:hi])
        u = _dot(h, wi_ref[:, D_FF + lo:D_FF + hi])
        act = (a * _sigmoid(a) * u).astype(BF16)
        y = y + _dot(act, wo_ref[lo:hi, :])
    o_ref[...] = x + FFN_RESIDUAL * _rms(y, g_ref[1:2, :])


def _ffn(x, g2, w_in, w_out):
    return pl.pallas_call(
        _ffn_kernel,
        grid=(TOKENS // TM,),
        in_specs=[_TOKEN_TILE, _const_spec((2, D_MODEL)), _const_spec((D_MODEL, 2 * D_FF)),
                  _const_spec((D_FF, D_MODEL))],
        out_specs=_TOKEN_TILE,
        out_shape=jax.ShapeDtypeStruct((TOKENS, D_MODEL), F32),
        compiler_params=_params(),
        name="ffn",
    )(x, g2, w_in, w_out)


def _qkv_kernel(x_ref, g_ref, wqt_ref, wk_ref, wvt_ref, qt_ref, k_ref, vt_ref):
    h = _rms(x_ref[...], g_ref[...]).astype(BF16)
    qt = lax.dot_general(wqt_ref[...], h, _NT, preferred_element_type=F32)
    qt_ref[...] = (qt * (HEAD_DIM ** -0.5 * LOG2E)).astype(BF16)
    k_ref[...] = _dot(h, wk_ref[...]).astype(BF16)
    vt_ref[...] = lax.dot_general(wvt_ref[...], h, _NT, preferred_element_type=F32).astype(BF16)


def _qkv(x, g, wq_t, wk, wv_t):
    feat_tile = pl.BlockSpec((D_MODEL, TM), lambda t: (0, t))
    feat = jax.ShapeDtypeStruct((D_MODEL, TOKENS), BF16)
    w_spec = _const_spec((D_MODEL, D_MODEL))
    return pl.pallas_call(
        _qkv_kernel,
        grid=(TOKENS // TM,),
        in_specs=[_TOKEN_TILE, _const_spec((1, D_MODEL)), w_spec, w_spec, w_spec],
        out_specs=(feat_tile, _TOKEN_TILE, feat_tile),
        out_shape=(feat, jax.ShapeDtypeStruct((TOKENS, D_MODEL), BF16), feat),
        compiler_params=_params(),
        name="qkv",
    )(x, g, wq_t, wk, wv_t)


def _bias_kernel(rel_ref, o_ref):
    c = pl.program_id(0)
    r = pl.program_id(1)
    ki = lax.broadcasted_iota(jnp.int32, (TK, TQ), 0)
    qi = lax.broadcasted_iota(jnp.int32, (TK, TQ), 1)
    d = qi - ki + r * TK
    val = jnp.full((TK, TQ), rel_ref[0, c], F32)
    for j, thr in enumerate(T5_THRESHOLDS, start=1):
        val = jnp.where(d >= thr, rel_ref[j, c], val)
    val = (val - rel_ref[REL_BUCKETS - 1, c]) * LOG2E
    o_ref[...] = jnp.where(d >= 0, val, NEG_INF)


def _bias_tiles(rel_bias):
    tiles = pl.pallas_call(
        _bias_kernel,
        grid=(2 * N_HEADS, BIAS_TILES),
        in_specs=[pl.BlockSpec(memory_space=pltpu.SMEM)],
        out_specs=pl.BlockSpec((None, None, TK, TQ), lambda c, r: (c, r, 0, 0)),
        out_shape=jax.ShapeDtypeStruct((2 * N_HEADS, BIAS_TILES, TK, TQ), F32),
        compiler_params=_params(2),
        name="t5_bias",
    )(rel_bias)
    return tiles.reshape(N_HEADS, 2, BIAS_TILES, TK, TQ)


def _attn_kernel(qt_ref, k_ref, vt_ref, bias_ref, lam_ref, sg_ref, o_ref, m_ref, acc_ref,
                 s_buf, p_buf, a_buf, *, lambda_init):
    i = pl.program_id(2)
    qt = qt_ref[...]
    row = lax.broadcasted_iota(jnp.int32, qt.shape, 0)
    zero = jnp.zeros_like(qt)
    q_maps = (jnp.where(row < HEAD_DIM, qt, zero), jnp.where(row >= HEAD_DIM, qt, zero))

    m_ref[...] = jnp.full(m_ref.shape, NEG_INF, F32)
    acc_ref[...] = jnp.zeros(acc_ref.shape, F32)
    sum_rows = jnp.where(lax.broadcasted_iota(jnp.int32, (SUM_ROWS, TK), 0) == 0, 1.0, 0.0).astype(BF16)

    h0, h1 = slice(0, TQ_HALF), slice(TQ_HALF, TQ)

    def keys(j):
        return k_ref[pl.ds(pl.multiple_of(j * TK, TK), TK), :]

    def vals(j):
        vb = vt_ref[:, pl.ds(pl.multiple_of(j * TK, TK), TK)]
        return jnp.concatenate([vb, sum_rows], axis=0)

    def scores(kb, j, mi, cols):
        tile = jnp.clip(i - j, 0, BIAS_TILES - 1)
        return _dot(kb, q_maps[mi][:, cols]) + bias_ref[mi, tile, :, cols]

    def softmax(mi, cols, s):
        m_prev = m_ref[mi, :, cols]
        m_new = jnp.maximum(m_prev, jnp.max(s, axis=0, keepdims=True))
        alpha = jnp.exp2(m_prev - m_new)
        p = jnp.exp2(s - m_new)
        m_ref[mi, :, cols] = m_new
        return p.astype(BF16), alpha

    def values(vb, mi, cols, p, alpha):
        acc_ref[mi, :, cols] = alpha * acc_ref[mi, :, cols] + _dot(vb, p)

    kb0 = keys(0)
    for mi in range(2):
        s_buf[mi] = scores(kb0, 0, mi, h0)
        p_buf[mi] = jnp.zeros((TK, TQ_HALF), BF16)
        a_buf[mi] = jnp.ones((1, TQ_HALF), F32)

    def trip(j, carry):
        kb, vb = keys(j), vals(j)
        j_prev = jnp.maximum(j - 1, 0)
        j_next = jnp.minimum(j + 1, i)
        vb_prev, kb_next = vals(j_prev), keys(j_next)
        s2 = scores(kb, j, 0, h1)
        p0, a0 = softmax(0, h0, s_buf[0])
        values(vb_prev, 0, h1, p_buf[0], a_buf[0])
        s3 = scores(kb, j, 1, h1)
        p1, a1 = softmax(1, h0, s_buf[1])
        values(vb_prev, 1, h1, p_buf[1], a_buf[1])
        values(vb, 0, h0, p0, a0)
        p_buf[0], a_buf[0] = softmax(0, h1, s2)
        s_buf[0] = scores(kb_next, j_next, 0, h0)
        values(vb, 1, h0, p1, a1)
        p_buf[1], a_buf[1] = softmax(1, h1, s3)
        s_buf[1] = scores(kb_next, j_next, 1, h0)
        return carry

    lax.fori_loop(0, i + 1, trip, 0)
    vb_last = vals(i)
    for mi in range(2):
        values(vb_last, mi, h1, p_buf[mi], a_buf[mi])

    lv = lam_ref[...]
    lam = (jnp.exp(jnp.sum(lv[0:1] * lv[1:2], keepdims=True))
           - jnp.exp(jnp.sum(lv[2:3] * lv[3:4], keepdims=True)) + lambda_init)
    d = 2 * HEAD_DIM
    ot = (acc_ref[0, 0:d, :] * (1.0 / acc_ref[0, d:d + 1, :])
          - lam * (acc_ref[1, 0:d, :] * (1.0 / acc_ref[1, d:d + 1, :])))
    o = _rms(ot.T, sg_ref[...]) * (1.0 - lambda_init)
    o_ref[...] = o.astype(BF16)


def _attention(qt, k, vt, bias, lam_vecs, subln_g, lambda_init):
    nq = SEQ // TQ
    return pl.pallas_call(
        functools.partial(_attn_kernel, lambda_init=lambda_init),
        grid=(BATCH, N_HEADS, nq),
        in_specs=[pl.BlockSpec((2 * HEAD_DIM, TQ), lambda b, h, i: (h, b * nq + i)),
                  pl.BlockSpec((SEQ, 2 * HEAD_DIM), lambda b, h, i: (b, h)),
                  pl.BlockSpec((2 * HEAD_DIM, SEQ), lambda b, h, i: (h, b)),
                  pl.BlockSpec((None, 2, BIAS_TILES, TK, TQ), lambda b, h, i: (h, 0, 0, 0, 0)),
                  pl.BlockSpec((4, HEAD_DIM), lambda b, h, i: (0, 0)),
                  pl.BlockSpec((1, 2 * HEAD_DIM), lambda b, h, i: (0, 0))],
        out_specs=pl.BlockSpec((TQ, 2 * HEAD_DIM), lambda b, h, i: (b * nq + i, h)),
        out_shape=jax.ShapeDtypeStruct((TOKENS, D_MODEL), BF16),
        scratch_shapes=[pltpu.VMEM((2, 1, TQ), F32),
                        pltpu.VMEM((2, 2 * HEAD_DIM + SUM_ROWS, TQ), F32),
                        pltpu.VMEM((2, TK, TQ_HALF), F32), pltpu.VMEM((2, TK, TQ_HALF), BF16),
                        pltpu.VMEM((2, 1, TQ_HALF), F32)],
        compiler_params=_params(3),
        name="diff_attn",
    )(qt, k, vt, bias, lam_vecs, subln_g)


def _proj_kernel(a_ref, x_ref, w_ref, g_ref, o_ref):
    o_ref[...] = x_ref[...] + _rms(_dot(a_ref[...], w_ref[...]), g_ref[...])


def _out_proj(a, x, w, g):
    return pl.pallas_call(
        _proj_kernel,
        grid=(TOKENS // TM,),
        in_specs=[_TOKEN_TILE, _TOKEN_TILE, _const_spec((D_MODEL, D_MODEL)),
                  _const_spec((1, D_MODEL))],
        out_specs=_TOKEN_TILE,
        out_shape=jax.ShapeDtypeStruct((TOKENS, D_MODEL), F32),
        compiler_params=_params(),
        name="attn_out",
    )(a, x, w, g)


_SUPER_TILE = pl.BlockSpec((OCTETS, CH_PER_TILE, OCT_W),
                           lambda b, c: (0, b * TILES_PER_SEQ + c, 0))
_SEQ_TOKEN_TILE = pl.BlockSpec((TM, D_MODEL), lambda b, c: (b * TILES_PER_SEQ + c, 0))


def _ssm_norm_kernel(x_ref, g_ref, o_ref, h_scr):
    h = _rms(x_ref[...], g_ref[...])
    for j in range(OCTETS):
        h_scr[j] = h[:, j * LANES:(j + 1) * LANES]
    for j in range(OCTETS):
        for s in range(SSM_L):
            rows = h_scr[j, pl.ds(s, CH_PER_TILE, stride=SSM_L), :]
            o_ref[j, :, s * LANES:(s + 1) * LANES] = rows.astype(BF16)


def _ssm_norm(x, g):
    return pl.pallas_call(
        _ssm_norm_kernel,
        grid=(BATCH, TILES_PER_SEQ),
        in_specs=[_SEQ_TOKEN_TILE, pl.BlockSpec((1, D_MODEL), lambda b, c: (0, 0))],
        out_specs=_SUPER_TILE,
        out_shape=jax.ShapeDtypeStruct((OCTETS, SSM_ROWS, OCT_W), BF16),
        scratch_shapes=[pltpu.VMEM((OCTETS, TM, LANES), F32)],
        compiler_params=_params(2),
        name="ssm_norm",
    )(x, g)


def _ssm_operators(lam_re, lam_im, log_dt, b_re, b_im, c_re, c_im, d_skip):
    hp = lax.Precision.HIGHEST
    lam_re, lam_im, log_dt, b_re, b_im, c_re, c_im, d_skip = lax.optimization_barrier(
        (lam_re, lam_im, log_dt, b_re, b_im, c_re, c_im, d_skip))
    dt = jnp.exp(log_dt)[:, None]
    zr, zi = lam_re * dt, lam_im * dt
    ks = jnp.arange(SSM_L + 1, dtype=F32)[:, None, None]
    mag = jnp.exp(ks * zr)
    pr, pi = mag * jnp.cos(ks * zi), mag * jnp.sin(ks * zi)
    nr = jnp.expm1(zr) * jnp.cos(zi) - 2.0 * jnp.sin(0.5 * zi) ** 2
    ni = pi[1]
    den = lam_re * lam_re + lam_im * lam_im
    fr, fi = (nr * lam_re + ni * lam_im) / den, (ni * lam_re - nr * lam_im) / den
    bb_re = fr[..., None] * b_re - fi[..., None] * b_im
    bb_im = fr[..., None] * b_im + fi[..., None] * b_re
    ca_re = c_re[None] * pr[:SSM_L, :, None, :] - c_im[None] * pi[:SSM_L, :, None, :]
    ca_im = c_re[None] * pi[:SSM_L, :, None, :] + c_im[None] * pr[:SSM_L, :, None, :]

    kern = (jnp.einsum('tgop,gph->tgoh', ca_re, bb_re, precision=hp)
            - jnp.einsum('tgop,gph->tgoh', ca_im, bb_im, precision=hp))
    eye_h = jnp.eye(GROUP_CH, dtype=F32)
    kern = kern.at[0].add(d_skip.reshape(GROUPS, GROUP_CH)[:, :, None] * eye_h[None])

    same = jnp.eye(OCT_GROUPS, dtype=F32)
    kj = kern.reshape(SSM_L, OCTETS, OCT_GROUPS, GROUP_CH, GROUP_CH)
    kj = lax.optimization_barrier(kj.transpose(1, 2, 4, 0, 3))
    imp = kj[:, :, :, :, None, :] * same[None, :, None, None, :, None]
    imp = imp.reshape(OCTETS, LANES, OCT_W)

    def block_rows(t):
        t = lax.optimization_barrier(t)
        return (t[:, :, :, None, :] * same[None, :, None, :, None]).reshape(OCTETS, LANES, OCT_STATE)

    b_rows = lambda t: block_rows(t.reshape(OCTETS, OCT_GROUPS, SSM_STATE, GROUP_CH)
                                  .transpose(0, 1, 3, 2))
    c_rows = lambda t: block_rows(t.reshape(OCTETS, OCT_GROUPS, GROUP_CH, SSM_STATE))
    bmat = jnp.stack([b_rows(bb_re), b_rows(bb_im)], axis=1)
    cmat = jnp.stack([c_rows(c_re), c_rows(c_im)], axis=1)

    kr = jnp.arange(SSM_L - 1, -1, -1, dtype=F32)[:, None, None]
    mag_r = jnp.exp(kr * zr)
    per_oct = lambda t: t.reshape(t.shape[:-2] + (OCTETS, OCT_STATE))
    pw_b = per_oct(jnp.stack([mag_r * jnp.cos(kr * zi), mag_r * jnp.sin(kr * zi)]))
    pw_c = per_oct(jnp.stack([pr[1:], pi[1:]]))
    a_chunk = per_oct(jnp.stack([pr[SSM_L], pi[SSM_L]]))
    return (imp.astype(BF16), bmat, cmat, pw_b.transpose(2, 0, 1, 3), pw_c.transpose(2, 0, 1, 3),
            a_chunk.transpose(1, 0, 2))


def _scaled_blocks(mat_ref, pw_ref, out_ref, im_sign):
    def block(s, carry):
        m_re, m_im = mat_ref[0], mat_ref[1]
        p_re, p_im = pw_ref[0, pl.ds(s, 1), :], pw_ref[1, pl.ds(s, 1), :]
        rows = pl.ds(pl.multiple_of(s * LANES, LANES), LANES)
        out_ref[rows, 0:OCT_STATE] = (m_re * p_re - m_im * p_im).astype(BF16)
        out_ref[rows, OCT_STATE:2 * OCT_STATE] = (im_sign * (m_re * p_im + m_im * p_re)).astype(BF16)
        return carry

    lax.fori_loop(0, SSM_L, block, 0)


def _s5_state_kernel(u_ref, bmat_ref, pw_ref, a_ref, xs_ref, bpow_scr, s_scr, x_scr):
    _scaled_blocks(bmat_ref, pw_ref, bpow_scr, 1.0)
    for r in range(0, SSM_ROWS, SSM_RT):
        s_loc = _dot(u_ref[r:r + SSM_RT, :], bpow_scr[...])
        for k in range(STATE_COLS):
            s_scr[k, r:r + SSM_RT, :] = s_loc[:, k * LANES:(k + 1) * LANES]
    a = a_ref[...]
    col = lambda r, k: jnp.broadcast_to(a[r:r + 1, k * LANES:(k + 1) * LANES], (BATCH, LANES))
    ar = [col(0, k) for k in range(RE_COLS)]
    ai = [col(1, k) for k in range(RE_COLS)]

    def step(c, carry):
        rows = pl.ds(c, BATCH, stride=SSM_NC)
        nxt_re, nxt_im = [], []
        for k in range(RE_COLS):
            xr, xi = carry[k], carry[RE_COLS + k]
            x_scr[k, rows, :] = xr
            x_scr[RE_COLS + k, rows, :] = xi
            nxt_re.append(ar[k] * xr - ai[k] * xi + s_scr[k, rows, :])
            nxt_im.append(ar[k] * xi + ai[k] * xr + s_scr[RE_COLS + k, rows, :])
        return tuple(nxt_re + nxt_im)

    zeros = jnp.zeros((BATCH, LANES), F32)
    lax.fori_loop(0, SSM_NC, step, (zeros,) * STATE_COLS, unroll=4)
    for k in range(STATE_COLS):
        xs_ref[:, k * LANES:(k + 1) * LANES] = x_scr[k].astype(BF16)


def _s5_states(u_oct, bmat, pw_b, a_chunk):
    per_oct = lambda *tail: pl.BlockSpec((None,) + tail, lambda j: (j,) + (0,) * len(tail))
    return pl.pallas_call(
        _s5_state_kernel,
        grid=(OCTETS,),
        in_specs=[per_oct(SSM_ROWS, OCT_W), per_oct(2, LANES, OCT_STATE),
                  per_oct(2, SSM_L, OCT_STATE), per_oct(2, OCT_STATE)],
        out_specs=per_oct(SSM_ROWS, 2 * OCT_STATE),
        out_shape=jax.ShapeDtypeStruct((OCTETS, SSM_ROWS, 2 * OCT_STATE), BF16),
        scratch_shapes=[pltpu.VMEM((OCT_W, 2 * OCT_STATE), BF16),
                        pltpu.VMEM((STATE_COLS, SSM_ROWS, LANES), F32),
                        pltpu.VMEM((STATE_COLS, SSM_ROWS, LANES), F32)],
        compiler_params=_params(),
        name="s5_states",
    )(u_oct, bmat, pw_b, a_chunk)


def _s5_out_kernel(u_ref, xs_ref, imp_ref, cmat_ref, pw_ref, y_ref, toep_scr, cpow_scr):
    @pl.when(pl.program_id(1) == 0)
    def _():
        toep_scr[...] = jnp.zeros(toep_scr.shape, BF16)
        for s in range(SSM_L):
            toep_scr[s * LANES:(s + 1) * LANES, s * LANES:] = imp_ref[:, :OCT_W - s * LANES]
        _scaled_blocks(cmat_ref, pw_ref, cpow_scr, -1.0)

    y = _dot(u_ref[...], toep_scr[...]) + lax.dot_general(
        xs_ref[...], cpow_scr[...], _NT, preferred_element_type=F32)
    y_ref[...] = y.astype(BF16)


def _s5_output(u_oct, xs, imp, cmat, pw_c):
    rows = lambda w: pl.BlockSpec((None, SSM_RT, w), lambda j, r: (j, r, 0))
    per_oct = lambda *tail: pl.BlockSpec((None,) + tail, lambda j, r: (j,) + (0,) * len(tail))
    return pl.pallas_call(
        _s5_out_kernel,
        grid=(OCTETS, SSM_ROWS // SSM_RT),
        in_specs=[rows(OCT_W), rows(2 * OCT_STATE), per_oct(LANES, OCT_W),
                  per_oct(2, LANES, OCT_STATE), per_oct(2, SSM_L, OCT_STATE)],
        out_specs=rows(OCT_W),
        out_shape=jax.ShapeDtypeStruct((OCTETS, SSM_ROWS, OCT_W), BF16),
        scratch_shapes=[pltpu.VMEM((OCT_W, OCT_W), BF16), pltpu.VMEM((OCT_W, 2 * OCT_STATE), BF16)],
        compiler_params=_params(2),
        name="s5_output",
    )(u_oct, xs, imp, cmat, pw_c)


def _gelu_tanh(x):
    return 0.5 * x * (1.0 + jnp.tanh(math.sqrt(2.0 / math.pi) * (x + 0.044715 * (x * x * x))))


def _glu_kernel(y_ref, x_ref, w_ref, b_ref, g_ref, o_ref, y_scr):
    for j in range(OCTETS):
        for s in range(SSM_L):
            y_scr[j, pl.ds(s, CH_PER_TILE, stride=SSM_L), :] = (
                y_ref[j, :, s * LANES:(s + 1) * LANES].astype(F32))
    y = jnp.concatenate([y_scr[j] for j in range(OCTETS)], axis=1)
    z = _dot(_gelu_tanh(y).astype(BF16), w_ref[...]) + b_ref[...]
    m = z[:, :D_MODEL] * _sigmoid(z[:, D_MODEL:])
    o_ref[...] = x_ref[...] + _rms(m, g_ref[...])


def _glu(y_oct, x, w, b, g):
    const = lambda shape: pl.BlockSpec(shape, lambda b, c: (0, 0), pipeline_mode=pl.Buffered(1))
    return pl.pallas_call(
        _glu_kernel,
        grid=(BATCH, TILES_PER_SEQ),
        in_specs=[_SUPER_TILE, _SEQ_TOKEN_TILE, const((D_MODEL, 2 * D_MODEL)),
                  const((1, 2 * D_MODEL)), const((1, D_MODEL))],
        out_specs=_SEQ_TOKEN_TILE,
        out_shape=jax.ShapeDtypeStruct((TOKENS, D_MODEL), F32),
        scratch_shapes=[pltpu.VMEM((OCTETS, TM, LANES), F32)],
        compiler_params=_params(2),
        name="s5_glu",
    )(y_oct, x, w, b, g)


def _ple_kernel(x_ref, p_ref, g_ref, wg_ref, wp_ref, o_ref):
    x = x_ref[...]
    gate = _sigmoid(_dot(_rms(x, g_ref[0:1, :]).astype(BF16), wg_ref[...]))
    emb = _dot(p_ref[...].astype(BF16), wp_ref[...])
    o_ref[...] = x + _rms(gate * emb, g_ref[1:2, :])


def _ple(x, p, layer, g2, w_gate, w_proj):
    return pl.pallas_call(
        _ple_kernel,
        grid=(TOKENS // TM,),
        in_specs=[_TOKEN_TILE, pl.BlockSpec((None, TM, PLE_DIM), lambda t: (layer, t, 0)),
                  _const_spec((2, D_MODEL)), _const_spec((D_MODEL, D_MODEL)),
                  _const_spec((PLE_DIM, D_MODEL))],
        out_specs=_TOKEN_TILE,
        out_shape=jax.ShapeDtypeStruct((TOKENS, D_MODEL), F32),
        compiler_params=_params(),
        name="ple",
    )(x, p, g2, w_gate, w_proj)


def kernel(x, p, norm_g, ffn_w_in, ffn_w_out, attn_w_qkv, attn_w_o, attn_lam, attn_subln_g,
           rel_bias, ssm_lam_re, ssm_lam_im, ssm_log_dt, ssm_b_re, ssm_b_im, ssm_c_re, ssm_c_im,
           ssm_d, ssm_w_glu, ssm_b_glu, ple_w_proj, ple_w_gate):
    x = x.reshape(TOKENS, D_MODEL)
    p = p.reshape(DEPTH, TOKENS, PLE_DIM)
    bias = _bias_tiles(rel_bias)
    for i in range(DEPTH):
        g = norm_g[i]
        j = i // N_MIXERS

        x = _ffn(x, g[0:2], ffn_w_in[i, 0].astype(BF16), ffn_w_out[i, 0].astype(BF16))

        if i % N_MIXERS == 0:
            lambda_init = 0.8 - 0.6 * math.exp(-0.3 * i)
            w = attn_w_qkv[j].astype(BF16)
            qt, k, vt = _qkv(x, g[2:3], w[:, :D_MODEL].T, w[:, D_MODEL:2 * D_MODEL],
                             w[:, 2 * D_MODEL:].T)
            o = _attention(qt, k, vt, bias, attn_lam[j], attn_subln_g[j].reshape(1, -1),
                           lambda_init)
            x = _out_proj(o, x, attn_w_o[j].astype(BF16), g[3:4])
        else:
            imp, bmat, cmat, pw_b, pw_c, a_chunk = _ssm_operators(
                ssm_lam_re[j], ssm_lam_im[j], ssm_log_dt[j], ssm_b_re[j], ssm_b_im[j],
                ssm_c_re[j], ssm_c_im[j], ssm_d[j])
            u_oct = _ssm_norm(x, g[2:3])
            xs = _s5_states(u_oct, bmat, pw_b, a_chunk)
            y_oct = _s5_output(u_oct, xs, imp, cmat, pw_c)
            x = _glu(y_oct, x, ssm_w_glu[j].astype(BF16), ssm_b_glu[j].reshape(1, -1), g[3:4])

        x = _ffn(x, g[4:6], ffn_w_in[i, 1].astype(BF16), ffn_w_out[i, 1].astype(BF16))
        x = _ple(x, p, i, g[6:8], ple_w_gate[i].astype(BF16), ple_w_proj[i].astype(BF16))
    return x.reshape(BATCH, SEQ, D_MODEL)
```

```python
import functools
import math

import numpy as np
import jax
import jax.numpy as jnp
from jax import lax
from jax.experimental import pallas as pl
from jax.experimental.pallas import tpu as pltpu

D_MODEL = 1024
BATCH = 8
SEQ = 4096
DEPTH = 4
N_MIXERS = 2
HEAD_DIM = 64
N_HEADS = D_MODEL // (2 * HEAD_DIM)
REL_BUCKETS = 32
REL_MAX_DIST = 128
GROUP_CH = 16
GROUPS = D_MODEL // GROUP_CH
SSM_STATE = 64
D_FF = 2816
FFN_RESIDUAL = 0.5
PLE_DIM = 256
RMS_EPS = 1e-6
NEG_INF = -1e30

TOKENS = BATCH * SEQ
F32 = jnp.float32
BF16 = jnp.bfloat16
LANES = 128

V7X_VMEM_BYTES = 64 * 1024 * 1024
VMEM_LIMIT = V7X_VMEM_BYTES - 8 * 1024 * 1024

TM = 512
TILES_PER_SEQ = SEQ // TM
MXU_TILE = 256
FF_SPLITS = (0, 6 * MXU_TILE, D_FF)
assert D_FF % MXU_TILE == 0
TQ = 512
TK = 512
BIAS_TILES = 3
TQ_HALF = TQ // 2
SUM_ROWS = 16
LOG2E = math.log2(math.e)

SSM_L = 16
SSM_NC = SEQ // SSM_L
OCTETS = D_MODEL // LANES
OCT_GROUPS = LANES // GROUP_CH
OCT_W = SSM_L * LANES
OCT_STATE = OCT_GROUPS * SSM_STATE
SSM_ROWS = BATCH * SSM_NC
SSM_RT = 512
CH_PER_TILE = TM // SSM_L
RE_COLS = OCT_STATE // LANES
STATE_COLS = 2 * RE_COLS


def _t5_thresholds():
    n = np.arange(0, 4 * REL_MAX_DIST)
    max_exact = REL_BUCKETS // 2
    nf = np.maximum(n, 1).astype(np.float64)
    large = max_exact + (np.log(nf / max_exact) / math.log(REL_MAX_DIST / max_exact)
                         * (REL_BUCKETS - max_exact)).astype(np.int32)
    bucket = np.where(n < max_exact, n, np.minimum(large, REL_BUCKETS - 1))
    return [int(np.argmax(bucket >= j)) for j in range(1, REL_BUCKETS)]


T5_THRESHOLDS = _t5_thresholds()
assert TQ == TK and T5_THRESHOLDS[-1] <= TK, "key blocks before i-1 sit in the last bucket"

_NT = (((1,), (1,)), ((), ()))


def _const_spec(shape):
    nd = len(shape)
    return pl.BlockSpec(shape, lambda *_: (0,) * nd, pipeline_mode=pl.Buffered(1))


def _params(n_axes=1):
    return pltpu.CompilerParams(dimension_semantics=("arbitrary",) * n_axes,
                                vmem_limit_bytes=VMEM_LIMIT)


def _rms(x, g):
    return x * lax.rsqrt(jnp.mean(x * x, axis=-1, keepdims=True) + RMS_EPS) * g


def _sigmoid(x):
    return 1.0 / (1.0 + jnp.exp(-x))


def _dot(a, b):
    return jnp.dot(a, b, preferred_element_type=F32)


_TOKEN_TILE = pl.BlockSpec((TM, D_MODEL), lambda t: (t, 0))


def _ffn_kernel(x_ref, g_ref, wi_ref, wo_ref, o_ref):
    x = x_ref[...]
    h = _rms(x, g_ref[0:1, :]).astype(BF16)
    y = jnp.zeros((TM, D_MODEL), F32)
    for lo, hi in zip(FF_SPLITS[:-1], FF_SPLITS[1:]):
        a = _dot(h, wi_ref[:, lo:hi])
        u = _dot(h, wi_ref[:, D_FF + lo:D_FF + hi])
        act = (a * _sigmoid(a) * u).astype(BF16)
        y = y + _dot(act, wo_ref[lo:hi, :])
    o_ref[...] = x + FFN_RESIDUAL * _rms(y, g_ref[1:2, :])


def _ffn(x, g2, w_in, w_out):
    return pl.pallas_call(
        _ffn_kernel,
        grid=(TOKENS // TM,),
        in_specs=[_TOKEN_TILE, _const_spec((2, D_MODEL)), _const_spec((D_MODEL, 2 * D_FF)),
                  _const_spec((D_FF, D_MODEL))],
        out_specs=_TOKEN_TILE,
        out_shape=jax.ShapeDtypeStruct((TOKENS, D_MODEL), F32),
        compiler_params=_params(),
        name="ffn",
    )(x, g2, w_in, w_out)


def _qkv_kernel(x_ref, g_ref, wqt_ref, wk_ref, wvt_ref, qt_ref, k_ref, vt_ref):
    h = _rms(x_ref[...], g_ref[...]).astype(BF16)
    qt = lax.dot_general(wqt_ref[...], h, _NT, preferred_element_type=F32)
    qt_ref[...] = (qt * (HEAD_DIM ** -0.5 * LOG2E)).astype(BF16)
    k_ref[...] = _dot(h, wk_ref[...]).astype(BF16)
    vt_ref[...] = lax.dot_general(wvt_ref[...], h, _NT, preferred_element_type=F32).astype(BF16)


def _qkv(x, g, wq_t, wk, wv_t):
    feat_tile = pl.BlockSpec((D_MODEL, TM), lambda t: (0, t))
    feat = jax.ShapeDtypeStruct((D_MODEL, TOKENS), BF16)
    w_spec = _const_spec((D_MODEL, D_MODEL))
    return pl.pallas_call(
        _qkv_kernel,
        grid=(TOKENS // TM,),
        in_specs=[_TOKEN_TILE, _const_spec((1, D_MODEL)), w_spec, w_spec, w_spec],
        out_specs=(feat_tile, _TOKEN_TILE, feat_tile),
        out_shape=(feat, jax.ShapeDtypeStruct((TOKENS, D_MODEL), BF16), feat),
        compiler_params=_params(),
        name="qkv",
    )(x, g, wq_t, wk, wv_t)


def _bias_kernel(rel_ref, o_ref):
    c = pl.program_id(0)
    r = pl.program_id(1)
    ki = lax.broadcasted_iota(jnp.int32, (TK, TQ), 0)
    qi = lax.broadcasted_iota(jnp.int32, (TK, TQ), 1)
    d = qi - ki + r * TK
    val = jnp.full((TK, TQ), rel_ref[0, c], F32)
    for j, thr in enumerate(T5_THRESHOLDS, start=1):
        val = jnp.where(d >= thr, rel_ref[j, c], val)
    val = (val - rel_ref[REL_BUCKETS - 1, c]) * LOG2E
    o_ref[...] = jnp.where(d >= 0, val, NEG_INF).astype(BF16)


def _bias_tiles(rel_bias):
    tiles = pl.pallas_call(
        _bias_kernel,
        grid=(2 * N_HEADS, BIAS_TILES),
        in_specs=[pl.BlockSpec(memory_space=pltpu.SMEM)],
        out_specs=pl.BlockSpec((None, None, TK, TQ), lambda c, r: (c, r, 0, 0)),
        out_shape=jax.ShapeDtypeStruct((2 * N_HEADS, BIAS_TILES, TK, TQ), BF16),
        compiler_params=_params(2),
        name="t5_bias",
    )(rel_bias)
    return tiles.reshape(N_HEADS, 2, BIAS_TILES, TK, TQ)


def _attn_kernel(qt_ref, k_ref, vt_ref, bias_ref, lam_ref, sg_ref, o_ref, m_ref, acc_ref,
                 s_buf, p_buf, a_buf, *, lambda_init):
    i = pl.program_id(2)
    qt = qt_ref[...]
    row = lax.broadcasted_iota(jnp.int32, qt.shape, 0)
    zero = jnp.zeros_like(qt)
    q_maps = (jnp.where(row < HEAD_DIM, qt, zero), jnp.where(row >= HEAD_DIM, qt, zero))

    m_ref[...] = jnp.full(m_ref.shape, NEG_INF, F32)
    acc_ref[...] = jnp.zeros(acc_ref.shape, F32)
    sum_rows = jnp.where(lax.broadcasted_iota(jnp.int32, (SUM_ROWS, TK), 0) == 0, 1.0, 0.0).astype(BF16)

    h0, h1 = slice(0, TQ_HALF), slice(TQ_HALF, TQ)

    def keys(j):
        return k_ref[pl.ds(pl.multiple_of(j * TK, TK), TK), :]

    def vals(j):
        vb = vt_ref[:, pl.ds(pl.multiple_of(j * TK, TK), TK)]
        return jnp.concatenate([vb, sum_rows], axis=0)

    def scores(kb, j, mi, cols):
        tile = jnp.clip(i - j, 0, BIAS_TILES - 1)
        return _dot(kb, q_maps[mi][:, cols]).astype(BF16) + bias_ref[mi, tile, :, cols]

    def softmax(mi, cols, s):
        m_prev = m_ref[mi, :, cols]
        m_new = jnp.maximum(m_prev, jnp.max(s, axis=0, keepdims=True).astype(F32))
        alpha = jnp.exp2(m_prev - m_new)
        p = jnp.exp2(s - m_new.astype(BF16))
        m_ref[mi, :, cols] = m_new
        return p, alpha

    def values(vb, mi, cols, p, alpha):
        acc_ref[mi, :, cols] = alpha * acc_ref[mi, :, cols] + _dot(vb, p)

    kb0 = keys(0)
    for mi in range(2):
        s_buf[mi] = scores(kb0, 0, mi, h0)
        p_buf[mi] = jnp.zeros((TK, TQ_HALF), BF16)
        a_buf[mi] = jnp.ones((1, TQ_HALF), F32)

    def trip(j, carry):
        kb, vb = keys(j), vals(j)
        j_prev = jnp.maximum(j - 1, 0)
        j_next = jnp.minimum(j + 1, i)
        vb_prev, kb_next = vals(j_prev), keys(j_next)
        s2 = scores(kb, j, 0, h1)
        p0, a0 = softmax(0, h0, s_buf[0])
        values(vb_prev, 0, h1, p_buf[0], a_buf[0])
        s3 = scores(kb, j, 1, h1)
        p1, a1 = softmax(1, h0, s_buf[1])
        values(vb_prev, 1, h1, p_buf[1], a_buf[1])
        values(vb, 0, h0, p0, a0)
        p_buf[0], a_buf[0] = softmax(0, h1, s2)
        s_buf[0] = scores(kb_next, j_next, 0, h0)
        values(vb, 1, h0, p1, a1)
        p_buf[1], a_buf[1] = softmax(1, h1, s3)
        s_buf[1] = scores(kb_next, j_next, 1, h0)
        return carry

    lax.fori_loop(0, i + 1, trip, 0)
    vb_last = vals(i)
    for mi in range(2):
        values(vb_last, mi, h1, p_buf[mi], a_buf[mi])

    lv = lam_ref[...]
    lam = (jnp.exp(jnp.sum(lv[0:1] * lv[1:2], keepdims=True))
           - jnp.exp(jnp.sum(lv[2:3] * lv[3:4], keepdims=True)) + lambda_init)
    d = 2 * HEAD_DIM
    ot = (acc_ref[0, 0:d, :] * (1.0 / acc_ref[0, d:d + 1, :])
          - lam * (acc_ref[1, 0:d, :] * (1.0 / acc_ref[1, d:d + 1, :])))
    o = _rms(ot.T, sg_ref[...]) * (1.0 - lambda_init)
    o_ref[...] = o.astype(BF16)


def _attention(qt, k, vt, bias, lam_vecs, subln_g, lambda_init):
    nq = SEQ // TQ
    return pl.pallas_call(
        functools.partial(_attn_kernel, lambda_init=lambda_init),
        grid=(BATCH, N_HEADS, nq),
        in_specs=[pl.BlockSpec((2 * HEAD_DIM, TQ), lambda b, h, i: (h, b * nq + i)),
                  pl.BlockSpec((SEQ, 2 * HEAD_DIM), lambda b, h, i: (b, h)),
                  pl.BlockSpec((2 * HEAD_DIM, SEQ), lambda b, h, i: (h, b)),
                  pl.BlockSpec((None, 2, BIAS_TILES, TK, TQ), lambda b, h, i: (h, 0, 0, 0, 0)),
                  pl.BlockSpec((4, HEAD_DIM), lambda b, h, i: (0, 0)),
                  pl.BlockSpec((1, 2 * HEAD_DIM), lambda b, h, i: (0, 0))],
        out_specs=pl.BlockSpec((TQ, 2 * HEAD_DIM), lambda b, h, i: (b * nq + i, h)),
        out_shape=jax.ShapeDtypeStruct((TOKENS, D_MODEL), BF16),
        scratch_shapes=[pltpu.VMEM((2, 1, TQ), F32),
                        pltpu.VMEM((2, 2 * HEAD_DIM + SUM_ROWS, TQ), F32),
                        pltpu.VMEM((2, TK, TQ_HALF), BF16), pltpu.VMEM((2, TK, TQ_HALF), BF16),
                        pltpu.VMEM((2, 1, TQ_HALF), F32)],
        compiler_params=_params(3),
        name="diff_attn",
    )(qt, k, vt, bias, lam_vecs, subln_g)


def _proj_kernel(a_ref, x_ref, w_ref, g_ref, o_ref):
    o_ref[...] = x_ref[...] + _rms(_dot(a_ref[...], w_ref[...]), g_ref[...])


def _out_proj(a, x, w, g):
    return pl.pallas_call(
        _proj_kernel,
        grid=(TOKENS // TM,),
        in_specs=[_TOKEN_TILE, _TOKEN_TILE, _const_spec((D_MODEL, D_MODEL)),
                  _const_spec((1, D_MODEL))],
        out_specs=_TOKEN_TILE,
        out_shape=jax.ShapeDtypeStruct((TOKENS, D_MODEL), F32),
        compiler_params=_params(),
        name="attn_out",
    )(a, x, w, g)


_SUPER_TILE = pl.BlockSpec((OCTETS, CH_PER_TILE, OCT_W),
                           lambda b, c: (0, b * TILES_PER_SEQ + c, 0))
_SEQ_TOKEN_TILE = pl.BlockSpec((TM, D_MODEL), lambda b, c: (b * TILES_PER_SEQ + c, 0))


def _ssm_norm_kernel(x_ref, g_ref, o_ref, h_scr):
    h = _rms(x_ref[...], g_ref[...])
    for j in range(OCTETS):
        h_scr[j] = h[:, j * LANES:(j + 1) * LANES]
    for j in range(OCTETS):
        for s in range(SSM_L):
            rows = h_scr[j, pl.ds(s, CH_PER_TILE, stride=SSM_L), :]
            o_ref[j, :, s * LANES:(s + 1) * LANES] = rows.astype(BF16)


def _ssm_norm(x, g):
    return pl.pallas_call(
        _ssm_norm_kernel,
        grid=(BATCH, TILES_PER_SEQ),
        in_specs=[_SEQ_TOKEN_TILE, pl.BlockSpec((1, D_MODEL), lambda b, c: (0, 0))],
        out_specs=_SUPER_TILE,
        out_shape=jax.ShapeDtypeStruct((OCTETS, SSM_ROWS, OCT_W), BF16),
        scratch_shapes=[pltpu.VMEM((OCTETS, TM, LANES), F32)],
        compiler_params=_params(2),
        name="ssm_norm",
    )(x, g)


def _ssm_operators(lam_re, lam_im, log_dt, b_re, b_im, c_re, c_im, d_skip):
    hp = lax.Precision.HIGHEST
    lam_re, lam_im, log_dt, b_re, b_im, c_re, c_im, d_skip = lax.optimization_barrier(
        (lam_re, lam_im, log_dt, b_re, b_im, c_re, c_im, d_skip))
    dt = jnp.exp(log_dt)[:, None]
    zr, zi = lam_re * dt, lam_im * dt
    ks = jnp.arange(SSM_L + 1, dtype=F32)[:, None, None]
    mag = jnp.exp(ks * zr)
    pr, pi = mag * jnp.cos(ks * zi), mag * jnp.sin(ks * zi)
    nr = jnp.expm1(zr) * jnp.cos(zi) - 2.0 * jnp.sin(0.5 * zi) ** 2
    ni = pi[1]
    den = lam_re * lam_re + lam_im * lam_im
    fr, fi = (nr * lam_re + ni * lam_im) / den, (ni * lam_re - nr * lam_im) / den
    bb_re = fr[..., None] * b_re - fi[..., None] * b_im
    bb_im = fr[..., None] * b_im + fi[..., None] * b_re
    ca_re = c_re[None] * pr[:SSM_L, :, None, :] - c_im[None] * pi[:SSM_L, :, None, :]
    ca_im = c_re[None] * pi[:SSM_L, :, None, :] + c_im[None] * pr[:SSM_L, :, None, :]

    kern = (jnp.einsum('tgop,gph->tgoh', ca_re, bb_re, precision=hp)
            - jnp.einsum('tgop,gph->tgoh', ca_im, bb_im, precision=hp))
    eye_h = jnp.eye(GROUP_CH, dtype=F32)
    kern = kern.at[0].add(d_skip.reshape(GROUPS, GROUP_CH)[:, :, None] * eye_h[None])

    same = jnp.eye(OCT_GROUPS, dtype=F32)
    kj = kern.reshape(SSM_L, OCTETS, OCT_GROUPS, GROUP_CH, GROUP_CH)
    kj = lax.optimization_barrier(kj.transpose(1, 2, 4, 0, 3))
    imp = kj[:, :, :, :, None, :] * same[None, :, None, None, :, None]
    imp = imp.reshape(OCTETS, LANES, OCT_W)

    def block_rows(t):
        t = lax.optimization_barrier(t)
        return (t[:, :, :, None, :] * same[None, :, None, :, None]).reshape(OCTETS, LANES, OCT_STATE)

    b_rows = lambda t: block_rows(t.reshape(OCTETS, OCT_GROUPS, SSM_STATE, GROUP_CH)
                                  .transpose(0, 1, 3, 2))
    c_rows = lambda t: block_rows(t.reshape(OCTETS, OCT_GROUPS, GROUP_CH, SSM_STATE))
    bmat = jnp.stack([b_rows(bb_re), b_rows(bb_im)], axis=1)
    cmat = jnp.stack([c_rows(c_re), c_rows(c_im)], axis=1)

    kr = jnp.arange(SSM_L - 1, -1, -1, dtype=F32)[:, None, None]
    mag_r = jnp.exp(kr * zr)
    per_oct = lambda t: t.reshape(t.shape[:-2] + (OCTETS, OCT_STATE))
    pw_b = per_oct(jnp.stack([mag_r * jnp.cos(kr * zi), mag_r * jnp.sin(kr * zi)]))
    pw_c = per_oct(jnp.stack([pr[1:], pi[1:]]))
    a_chunk = per_oct(jnp.stack([pr[SSM_L], pi[SSM_L]]))
    return (imp.astype(BF16), bmat, cmat, pw_b.transpose(2, 0, 1, 3), pw_c.transpose(2, 0, 1, 3),
            a_chunk.transpose(1, 0, 2))


def _scaled_blocks(mat_ref, pw_ref, out_ref, im_sign):
    def block(s, carry):
        m_re, m_im = mat_ref[0], mat_ref[1]
        p_re, p_im = pw_ref[0, pl.ds(s, 1), :], pw_ref[1, pl.ds(s, 1), :]
        rows = pl.ds(pl.multiple_of(s * LANES, LANES), LANES)
        out_ref[rows, 0:OCT_STATE] = (m_re * p_re - m_im * p_im).astype(BF16)
        out_ref[rows, OCT_STATE:2 * OCT_STATE] = (im_sign * (m_re * p_im + m_im * p_re)).astype(BF16)
        return carry

    lax.fori_loop(0, SSM_L, block, 0)


def _s5_state_kernel(u_ref, bmat_ref, pw_ref, a_ref, xs_ref, bpow_scr, s_scr, x_scr):
    _scaled_blocks(bmat_ref, pw_ref, bpow_scr, 1.0)
    for r in range(0, SSM_ROWS, SSM_RT):
        s_loc = _dot(u_ref[r:r + SSM_RT, :], bpow_scr[...])
        for k in range(STATE_COLS):
            s_scr[k, r:r + SSM_RT, :] = s_loc[:, k * LANES:(k + 1) * LANES]
    a = a_ref[...]
    col = lambda r, k: jnp.broadcast_to(a[r:r + 1, k * LANES:(k + 1) * LANES], (BATCH, LANES))
    ar = [col(0, k) for k in range(RE_COLS)]
    ai = [col(1, k) for k in range(RE_COLS)]

    def step(c, carry):
        rows = pl.ds(c, BATCH, stride=SSM_NC)
        nxt_re, nxt_im = [], []
        for k in range(RE_COLS):
            xr, xi = carry[k], carry[RE_COLS + k]
            x_scr[k, rows, :] = xr
            x_scr[RE_COLS + k, rows, :] = xi
            nxt_re.append(ar[k] * xr - ai[k] * xi + s_scr[k, rows, :])
            nxt_im.append(ar[k] * xi + ai[k] * xr + s_scr[RE_COLS + k, rows, :])
        return tuple(nxt_re + nxt_im)

    zeros = jnp.zeros((BATCH, LANES), F32)
    lax.fori_loop(0, SSM_NC, step, (zeros,) * STATE_COLS, unroll=4)
    for k in range(STATE_COLS):
        xs_ref[:, k * LANES:(k + 1) * LANES] = x_scr[k].astype(BF16)


def _s5_states(u_oct, bmat, pw_b, a_chunk):
    per_oct = lambda *tail: pl.BlockSpec((None,) + tail, lambda j: (j,) + (0,) * len(tail))
    return pl.pallas_call(
        _s5_state_kernel,
        grid=(OCTETS,),
        in_specs=[per_oct(SSM_ROWS, OCT_W), per_oct(2, LANES, OCT_STATE),
                  per_oct(2, SSM_L, OCT_STATE), per_oct(2, OCT_STATE)],
        out_specs=per_oct(SSM_ROWS, 2 * OCT_STATE),
        out_shape=jax.ShapeDtypeStruct((OCTETS, SSM_ROWS, 2 * OCT_STATE), BF16),
        scratch_shapes=[pltpu.VMEM((OCT_W, 2 * OCT_STATE), BF16),
                        pltpu.VMEM((STATE_COLS, SSM_ROWS, LANES), F32),
                        pltpu.VMEM((STATE_COLS, SSM_ROWS, LANES), F32)],
        compiler_params=_params(),
        name="s5_states",
    )(u_oct, bmat, pw_b, a_chunk)


def _s5_out_kernel(u_ref, xs_ref, imp_ref, cmat_ref, pw_ref, y_ref, toep_scr, cpow_scr):
    @pl.when(pl.program_id(1) == 0)
    def _():
        toep_scr[...] = jnp.zeros(toep_scr.shape, BF16)
        for s in range(SSM_L):
            toep_scr[s * LANES:(s + 1) * LANES, s * LANES:] = imp_ref[:, :OCT_W - s * LANES]
        _scaled_blocks(cmat_ref, pw_ref, cpow_scr, -1.0)

    y = _dot(u_ref[...], toep_scr[...]) + lax.dot_general(
        xs_ref[...], cpow_scr[...], _NT, preferred_element_type=F32)
    y_ref[...] = y.astype(BF16)


def _s5_output(u_oct, xs, imp, cmat, pw_c):
    rows = lambda w: pl.BlockSpec((None, SSM_RT, w), lambda j, r: (j, r, 0))
    per_oct = lambda *tail: pl.BlockSpec((None,) + tail, lambda j, r: (j,) + (0,) * len(tail))
    return pl.pallas_call(
        _s5_out_kernel,
        grid=(OCTETS, SSM_ROWS // SSM_RT),
        in_specs=[rows(OCT_W), rows(2 * OCT_STATE), per_oct(LANES, OCT_W),
                  per_oct(2, LANES, OCT_STATE), per_oct(2, SSM_L, OCT_STATE)],
        out_specs=rows(OCT_W),
        out_shape=jax.ShapeDtypeStruct((OCTETS, SSM_ROWS, OCT_W), BF16),
        scratch_shapes=[pltpu.VMEM((OCT_W, OCT_W), BF16), pltpu.VMEM((OCT_W, 2 * OCT_STATE), BF16)],
        compiler_params=_params(2),
        name="s5_output",
    )(u_oct, xs, imp, cmat, pw_c)


def _gelu_tanh(x):
    return 0.5 * x * (1.0 + jnp.tanh(math.sqrt(2.0 / math.pi) * (x + 0.044715 * (x * x * x))))


def _glu_kernel(y_ref, x_ref, w_ref, b_ref, g_ref, o_ref, y_scr):
    for j in range(OCTETS):
        for s in range(SSM_L):
            y_scr[j, pl.ds(s, CH_PER_TILE, stride=SSM_L), :] = (
                y_ref[j, :, s * LANES:(s + 1) * LANES].astype(F32))
    y = jnp.concatenate([y_scr[j] for j in range(OCTETS)], axis=1)
    z = _dot(_gelu_tanh(y).astype(BF16), w_ref[...]) + b_ref[...]
    m = z[:, :D_MODEL] * _sigmoid(z[:, D_MODEL:])
    o_ref[...] = x_ref[...] + _rms(m, g_ref[...])


def _glu(y_oct, x, w, b, g):
    const = lambda shape: pl.BlockSpec(shape, lambda b, c: (0, 0), pipeline_mode=pl.Buffered(1))
    return pl.pallas_call(
        _glu_kernel,
        grid=(BATCH, TILES_PER_SEQ),
        in_specs=[_SUPER_TILE, _SEQ_TOKEN_TILE, const((D_MODEL, 2 * D_MODEL)),
                  const((1, 2 * D_MODEL)), const((1, D_MODEL))],
        out_specs=_SEQ_TOKEN_TILE,
        out_shape=jax.ShapeDtypeStruct((TOKENS, D_MODEL), F32),
        scratch_shapes=[pltpu.VMEM((OCTETS, TM, LANES), F32)],
        compiler_params=_params(2),
        name="s5_glu",
    )(y_oct, x, w, b, g)


def _ple_kernel(x_ref, p_ref, g_ref, wg_ref, wp_ref, o_ref):
    x = x_ref[...]
    gate = _sigmoid(_dot(_rms(x, g_ref[0:1, :]).astype(BF16), wg_ref[...]))
    emb = _dot(p_ref[...].astype(BF16), wp_ref[...])
    o_ref[...] = x + _rms(gate * emb, g_ref[1:2, :])


def _ple(x, p, layer, g2, w_gate, w_proj):
    return pl.pallas_call(
        _ple_kernel,
        grid=(TOKENS // TM,),
        in_specs=[_TOKEN_TILE, pl.BlockSpec((None, TM, PLE_DIM), lambda t: (layer, t, 0)),
                  _const_spec((2, D_MODEL)), _const_spec((D_MODEL, D_MODEL)),
                  _const_spec((PLE_DIM, D_MODEL))],
        out_specs=_TOKEN_TILE,
        out_shape=jax.ShapeDtypeStruct((TOKENS, D_MODEL), F32),
        compiler_params=_params(),
        name="ple",
    )(x, p, g2, w_gate, w_proj)


def kernel(x, p, norm_g, ffn_w_in, ffn_w_out, attn_w_qkv, attn_w_o, attn_lam, attn_subln_g,
           rel_bias, ssm_lam_re, ssm_lam_im, ssm_log_dt, ssm_b_re, ssm_b_im, ssm_c_re, ssm_c_im,
           ssm_d, ssm_w_glu, ssm_b_glu, ple_w_proj, ple_w_gate):
    x = x.reshape(TOKENS, D_MODEL)
    p = p.reshape(DEPTH, TOKENS, PLE_DIM)
    bias = _bias_tiles(rel_bias)
    for i in range(DEPTH):
        g = norm_g[i]
        j = i // N_MIXERS

        x = _ffn(x, g[0:2], ffn_w_in[i, 0].astype(BF16), ffn_w_out[i, 0].astype(BF16))

        if i % N_MIXERS == 0:
            lambda_init = 0.8 - 0.6 * math.exp(-0.3 * i)
            w = attn_w_qkv[j].astype(BF16)
            qt, k, vt = _qkv(x, g[2:3], w[:, :D_MODEL].T, w[:, D_MODEL:2 * D_MODEL],
                             w[:, 2 * D_MODEL:].T)
            o = _attention(qt, k, vt, bias, attn_lam[j], attn_subln_g[j].reshape(1, -1),
                           lambda_init)
            x = _out_proj(o, x, attn_w_o[j].astype(BF16), g[3:4])
        else:
            imp, bmat, cmat, pw_b, pw_c, a_chunk = _ssm_operators(
                ssm_lam_re[j], ssm_lam_im[j], ssm_log_dt[j], ssm_b_re[j], ssm_b_im[j],
                ssm_c_re[j], ssm_c_im[j], ssm_d[j])
            u_oct = _ssm_norm(x, g[2:3])
            xs = _s5_states(u_oct, bmat, pw_b, a_chunk)
            y_oct = _s5_output(u_oct, xs, imp, cmat, pw_c)
            x = _glu(y_oct, x, ssm_w_glu[j].astype(BF16), ssm_b_glu[j].reshape(1, -1), g[3:4])

        x = _ffn(x, g[4:6], ffn_w_in[i, 1].astype(BF16), ffn_w_out[i, 1].astype(BF16))
        x = _ple(x, p, i, g[6:8], ple_w_gate[i].astype(BF16), ple_w_proj[i].astype(BF16))
    return x.reshape(BATCH, SEQ, D_MODEL)
```

```python
import functools
import math

import numpy as np
import jax
import jax.numpy as jnp
from jax import lax
from jax.experimental import pallas as pl
from jax.experimental.pallas import tpu as pltpu

D_MODEL = 1024
BATCH = 8
SEQ = 4096
DEPTH = 4
N_MIXERS = 2
HEAD_DIM = 64
N_HEADS = D_MODEL // (2 * HEAD_DIM)
REL_BUCKETS = 32
REL_MAX_DIST = 128
GROUP_CH = 16
GROUPS = D_MODEL // GROUP_CH
SSM_STATE = 64
D_FF = 2816
FFN_RESIDUAL = 0.5
PLE_DIM = 256
N_NORMS = 8
RMS_EPS = 1e-6
NEG_INF = -1e30

TOKENS = BATCH * SEQ
F32 = jnp.float32
BF16 = jnp.bfloat16
LANES = 128

V7X_VMEM_BYTES = 64 * 1024 * 1024
VMEM_LIMIT = V7X_VMEM_BYTES - 8 * 1024 * 1024

TM = 512
TILES_PER_SEQ = SEQ // TM
MXU_TILE = 256
FF_SPLITS = (0, 6 * MXU_TILE, D_FF)
assert D_FF % MXU_TILE == 0
TQ = 512
TK = 512
BIAS_TILES = 3
TQ_HALF = TQ // 2
SUM_ROWS = 16
LOG2E = math.log2(math.e)

SSM_L = 16
SSM_NC = SEQ // SSM_L
OCTETS = D_MODEL // LANES
OCT_GROUPS = LANES // GROUP_CH
OCT_W = SSM_L * LANES
OCT_STATE = OCT_GROUPS * SSM_STATE
SSM_ROWS = BATCH * SSM_NC
SSM_RT = 512
CH_PER_TILE = TM // SSM_L
RE_COLS = OCT_STATE // LANES
STATE_COLS = 2 * RE_COLS


def _t5_thresholds():
    n = np.arange(0, 4 * REL_MAX_DIST)
    max_exact = REL_BUCKETS // 2
    nf = np.maximum(n, 1).astype(np.float64)
    large = max_exact + (np.log(nf / max_exact) / math.log(REL_MAX_DIST / max_exact)
                         * (REL_BUCKETS - max_exact)).astype(np.int32)
    bucket = np.where(n < max_exact, n, np.minimum(large, REL_BUCKETS - 1))
    return [int(np.argmax(bucket >= j)) for j in range(1, REL_BUCKETS)]


T5_THRESHOLDS = _t5_thresholds()
assert TQ == TK and T5_THRESHOLDS[-1] <= TK, "key blocks before i-1 sit in the last bucket"

_NT = (((1,), (1,)), ((), ()))


def _const_spec(shape):
    nd = len(shape)
    return pl.BlockSpec(shape, lambda *_: (0,) * nd, pipeline_mode=pl.Buffered(1))


def _params(n_axes=1):
    return pltpu.CompilerParams(dimension_semantics=("arbitrary",) * n_axes,
                                vmem_limit_bytes=VMEM_LIMIT)


def _rms(x, g):
    return x * lax.rsqrt(jnp.mean(x * x, axis=-1, keepdims=True) + RMS_EPS) * g


def _sigmoid(x):
    return 1.0 / (1.0 + jnp.exp(-x))


def _dot(a, b):
    return jnp.dot(a, b, preferred_element_type=F32)


_TOKEN_TILE = pl.BlockSpec((TM, D_MODEL), lambda b, c: (b * TILES_PER_SEQ + c, 0))
_FEATURE_TILE = pl.BlockSpec((D_MODEL, TM), lambda b, c: (0, b * TILES_PER_SEQ + c))
_SUPER_TILE = pl.BlockSpec((OCTETS, CH_PER_TILE, OCT_W),
                           lambda b, c: (0, b * TILES_PER_SEQ + c, 0))
_TOKEN_GRID = (BATCH, TILES_PER_SEQ)
_FFN_SPECS = [_const_spec((N_NORMS, D_MODEL)), _const_spec((D_MODEL, 2 * D_FF)),
              _const_spec((D_FF, D_MODEL))]


def _ffn_residual(x, g_pre, g_post, wi_ref, wo_ref):
    h = _rms(x, g_pre).astype(BF16)
    y = jnp.zeros((TM, D_MODEL), F32)
    for lo, hi in zip(FF_SPLITS[:-1], FF_SPLITS[1:]):
        a = _dot(h, wi_ref[:, lo:hi])
        u = _dot(h, wi_ref[:, D_FF + lo:D_FF + hi])
        act = (a * _sigmoid(a) * u).astype(BF16)
        y = y + _dot(act, wo_ref[lo:hi, :])
    return x + FFN_RESIDUAL * _rms(y, g_post)


def _to_super_rows(h, o_ref, h_scr):
    for j in range(OCTETS):
        h_scr[j] = h[:, j * LANES:(j + 1) * LANES]
    for j in range(OCTETS):
        for s in range(SSM_L):
            rows = h_scr[j, pl.ds(s, CH_PER_TILE, stride=SSM_L), :]
            o_ref[j, :, s * LANES:(s + 1) * LANES] = rows.astype(BF16)


def _from_super_rows(y_ref, y_scr):
    for j in range(OCTETS):
        for s in range(SSM_L):
            y_scr[j, pl.ds(s, CH_PER_TILE, stride=SSM_L), :] = (
                y_ref[j, :, s * LANES:(s + 1) * LANES].astype(F32))
    return jnp.concatenate([y_scr[j] for j in range(OCTETS)], axis=1)


def _head_attn_kernel(x_ref, g_ref, wi_ref, wo_ref, wqt_ref, wk_ref, wvt_ref,
                      xo_ref, qt_ref, k_ref, vt_ref):
    x = _ffn_residual(x_ref[...], g_ref[0:1, :], g_ref[1:2, :], wi_ref, wo_ref)
    xo_ref[...] = x
    h = _rms(x, g_ref[2:3, :]).astype(BF16)
    qt = lax.dot_general(wqt_ref[...], h, _NT, preferred_element_type=F32)
    qt_ref[...] = (qt * (HEAD_DIM ** -0.5 * LOG2E)).astype(BF16)
    k_ref[...] = _dot(h, wk_ref[...]).astype(BF16)
    vt_ref[...] = lax.dot_general(wvt_ref[...], h, _NT, preferred_element_type=F32).astype(BF16)


def _head_ssm_kernel(x_ref, g_ref, wi_ref, wo_ref, xo_ref, u_ref, h_scr):
    x = _ffn_residual(x_ref[...], g_ref[0:1, :], g_ref[1:2, :], wi_ref, wo_ref)
    xo_ref[...] = x
    _to_super_rows(_rms(x, g_ref[2:3, :]), u_ref, h_scr)


def _layer_head(x, g, w_in, w_out, qkv=None):
    x_out = jax.ShapeDtypeStruct((TOKENS, D_MODEL), F32)
    if qkv is None:
        return pl.pallas_call(
            _head_ssm_kernel,
            grid=_TOKEN_GRID,
            in_specs=[_TOKEN_TILE] + _FFN_SPECS,
            out_specs=(_TOKEN_TILE, _SUPER_TILE),
            out_shape=(x_out, jax.ShapeDtypeStruct((OCTETS, SSM_ROWS, OCT_W), BF16)),
            scratch_shapes=[pltpu.VMEM((OCTETS, TM, LANES), F32)],
            compiler_params=_params(2),
            name="ffn_s5in",
        )(x, g, w_in, w_out)
    feat = jax.ShapeDtypeStruct((D_MODEL, TOKENS), BF16)
    return pl.pallas_call(
        _head_attn_kernel,
        grid=_TOKEN_GRID,
        in_specs=[_TOKEN_TILE] + _FFN_SPECS + [_const_spec((D_MODEL, D_MODEL))] * 3,
        out_specs=(_TOKEN_TILE, _FEATURE_TILE, _TOKEN_TILE, _FEATURE_TILE),
        out_shape=(x_out, feat, jax.ShapeDtypeStruct((TOKENS, D_MODEL), BF16), feat),
        compiler_params=_params(2),
        name="ffn_qkv",
    )(x, g, w_in, w_out, *qkv)


def _bias_kernel(rel_ref, o_ref):
    c = pl.program_id(0)
    r = pl.program_id(1)
    ki = lax.broadcasted_iota(jnp.int32, (TK, TQ), 0)
    qi = lax.broadcasted_iota(jnp.int32, (TK, TQ), 1)
    d = qi - ki + r * TK
    val = jnp.full((TK, TQ), rel_ref[0, c], F32)
    for j, thr in enumerate(T5_THRESHOLDS, start=1):
        val = jnp.where(d >= thr, rel_ref[j, c], val)
    val = (val - rel_ref[REL_BUCKETS - 1, c]) * LOG2E
    o_ref[...] = jnp.where(d >= 0, val, NEG_INF).astype(BF16)


def _bias_tiles(rel_bias):
    tiles = pl.pallas_call(
        _bias_kernel,
        grid=(2 * N_HEADS, BIAS_TILES),
        in_specs=[pl.BlockSpec(memory_space=pltpu.SMEM)],
        out_specs=pl.BlockSpec((None, None, TK, TQ), lambda c, r: (c, r, 0, 0)),
        out_shape=jax.ShapeDtypeStruct((2 * N_HEADS, BIAS_TILES, TK, TQ), BF16),
        compiler_params=_params(2),
        name="t5_bias",
    )(rel_bias)
    return tiles.reshape(N_HEADS, 2, BIAS_TILES, TK, TQ)


def _attn_kernel(qt_ref, k_ref, vt_ref, bias_ref, lam_ref, sg_ref, o_ref, m_ref, acc_ref,
                 s_buf, p_buf, a_buf, *, lambda_init):
    i = pl.program_id(2)
    qt = qt_ref[...]
    row = lax.broadcasted_iota(jnp.int32, qt.shape, 0)
    zero = jnp.zeros_like(qt)
    q_maps = (jnp.where(row < HEAD_DIM, qt, zero), jnp.where(row >= HEAD_DIM, qt, zero))

    m_ref[...] = jnp.full(m_ref.shape, NEG_INF, F32)
    acc_ref[...] = jnp.zeros(acc_ref.shape, F32)
    sum_rows = jnp.where(lax.broadcasted_iota(jnp.int32, (SUM_ROWS, TK), 0) == 0, 1.0, 0.0).astype(BF16)

    h0, h1 = slice(0, TQ_HALF), slice(TQ_HALF, TQ)

    def keys(j):
        return k_ref[pl.ds(pl.multiple_of(j * TK, TK), TK), :]

    def vals(j):
        vb = vt_ref[:, pl.ds(pl.multiple_of(j * TK, TK), TK)]
        return jnp.concatenate([vb, sum_rows], axis=0)

    def scores(kb, j, mi, cols):
        tile = jnp.clip(i - j, 0, BIAS_TILES - 1)
        return _dot(kb, q_maps[mi][:, cols]).astype(BF16) + bias_ref[mi, tile, :, cols]

    def softmax(mi, cols, s):
        m_prev = m_ref[mi, :, cols]
        m_new = jnp.maximum(m_prev, jnp.max(s, axis=0, keepdims=True).astype(F32))
        alpha = jnp.exp2(m_prev - m_new)
        p = jnp.exp2(s - m_new.astype(BF16))
        m_ref[mi, :, cols] = m_new
        return p, alpha

    def values(vb, mi, cols, p, alpha):
        acc_ref[mi, :, cols] = alpha * acc_ref[mi, :, cols] + _dot(vb, p)

    kb0 = keys(0)
    for mi in range(2):
        s_buf[mi] = scores(kb0, 0, mi, h0)
        p_buf[mi] = jnp.zeros((TK, TQ_HALF), BF16)
        a_buf[mi] = jnp.ones((1, TQ_HALF), F32)

    def trip(j, carry):
        kb, vb = keys(j), vals(j)
        j_prev = jnp.maximum(j - 1, 0)
        j_next = jnp.minimum(j + 1, i)
        vb_prev, kb_next = vals(j_prev), keys(j_next)
        s2 = scores(kb, j, 0, h1)
        p0, a0 = softmax(0, h0, s_buf[0])
        values(vb_prev, 0, h1, p_buf[0], a_buf[0])
        s3 = scores(kb, j, 1, h1)
        p1, a1 = softmax(1, h0, s_buf[1])
        values(vb_prev, 1, h1, p_buf[1], a_buf[1])
        values(vb, 0, h0, p0, a0)
        p_buf[0], a_buf[0] = softmax(0, h1, s2)
        s_buf[0] = scores(kb_next, j_next, 0, h0)
        values(vb, 1, h0, p1, a1)
        p_buf[1], a_buf[1] = softmax(1, h1, s3)
        s_buf[1] = scores(kb_next, j_next, 1, h0)
        return carry

    lax.fori_loop(0, i + 1, trip, 0)
    vb_last = vals(i)
    for mi in range(2):
        values(vb_last, mi, h1, p_buf[mi], a_buf[mi])

    lv = lam_ref[...]
    lam = (jnp.exp(jnp.sum(lv[0:1] * lv[1:2], keepdims=True))
           - jnp.exp(jnp.sum(lv[2:3] * lv[3:4], keepdims=True)) + lambda_init)
    d = 2 * HEAD_DIM
    ot = (acc_ref[0, 0:d, :] * (1.0 / acc_ref[0, d:d + 1, :])
          - lam * (acc_ref[1, 0:d, :] * (1.0 / acc_ref[1, d:d + 1, :])))
    o = _rms(ot.T, sg_ref[...]) * (1.0 - lambda_init)
    o_ref[...] = o.astype(BF16)


def _attention(qt, k, vt, bias, lam_vecs, subln_g, lambda_init):
    nq = SEQ // TQ
    return pl.pallas_call(
        functools.partial(_attn_kernel, lambda_init=lambda_init),
        grid=(BATCH, N_HEADS, nq),
        in_specs=[pl.BlockSpec((2 * HEAD_DIM, TQ), lambda b, h, i: (h, b * nq + i)),
                  pl.BlockSpec((SEQ, 2 * HEAD_DIM), lambda b, h, i: (b, h)),
                  pl.BlockSpec((2 * HEAD_DIM, SEQ), lambda b, h, i: (h, b)),
                  pl.BlockSpec((None, 2, BIAS_TILES, TK, TQ), lambda b, h, i: (h, 0, 0, 0, 0)),
                  pl.BlockSpec((4, HEAD_DIM), lambda b, h, i: (0, 0)),
                  pl.BlockSpec((1, 2 * HEAD_DIM), lambda b, h, i: (0, 0))],
        out_specs=pl.BlockSpec((TQ, 2 * HEAD_DIM), lambda b, h, i: (b * nq + i, h)),
        out_shape=jax.ShapeDtypeStruct((TOKENS, D_MODEL), BF16),
        scratch_shapes=[pltpu.VMEM((2, 1, TQ), F32),
                        pltpu.VMEM((2, 2 * HEAD_DIM + SUM_ROWS, TQ), F32),
                        pltpu.VMEM((2, TK, TQ_HALF), BF16), pltpu.VMEM((2, TK, TQ_HALF), BF16),
                        pltpu.VMEM((2, 1, TQ_HALF), F32)],
        compiler_params=_params(3),
        name="diff_attn",
    )(qt, k, vt, bias, lam_vecs, subln_g)


def _ssm_operators(lam_re, lam_im, log_dt, b_re, b_im, c_re, c_im, d_skip):
    hp = lax.Precision.HIGHEST
    lam_re, lam_im, log_dt, b_re, b_im, c_re, c_im, d_skip = lax.optimization_barrier(
        (lam_re, lam_im, log_dt, b_re, b_im, c_re, c_im, d_skip))
    dt = jnp.exp(log_dt)[:, None]
    zr, zi = lam_re * dt, lam_im * dt
    ks = jnp.arange(SSM_L + 1, dtype=F32)[:, None, None]
    mag = jnp.exp(ks * zr)
    pr, pi = mag * jnp.cos(ks * zi), mag * jnp.sin(ks * zi)
    nr = jnp.expm1(zr) * jnp.cos(zi) - 2.0 * jnp.sin(0.5 * zi) ** 2
    ni = pi[1]
    den = lam_re * lam_re + lam_im * lam_im
    fr, fi = (nr * lam_re + ni * lam_im) / den, (ni * lam_re - nr * lam_im) / den
    bb_re = fr[..., None] * b_re - fi[..., None] * b_im
    bb_im = fr[..., None] * b_im + fi[..., None] * b_re
    ca_re = c_re[None] * pr[:SSM_L, :, None, :] - c_im[None] * pi[:SSM_L, :, None, :]
    ca_im = c_re[None] * pi[:SSM_L, :, None, :] + c_im[None] * pr[:SSM_L, :, None, :]

    kern = (jnp.einsum('tgop,gph->tgoh', ca_re, bb_re, precision=hp)
            - jnp.einsum('tgop,gph->tgoh', ca_im, bb_im, precision=hp))
    eye_h = jnp.eye(GROUP_CH, dtype=F32)
    kern = kern.at[0].add(d_skip.reshape(GROUPS, GROUP_CH)[:, :, None] * eye_h[None])

    same = jnp.eye(OCT_GROUPS, dtype=F32)
    kj = kern.reshape(SSM_L, OCTETS, OCT_GROUPS, GROUP_CH, GROUP_CH)
    kj = lax.optimization_barrier(kj.transpose(1, 2, 4, 0, 3))
    imp = kj[:, :, :, :, None, :] * same[None, :, None, None, :, None]
    imp = imp.reshape(OCTETS, LANES, OCT_W)

    def block_rows(t):
        t = lax.optimization_barrier(t)
        return (t[:, :, :, None, :] * same[None, :, None, :, None]).reshape(OCTETS, LANES, OCT_STATE)

    b_rows = lambda t: block_rows(t.reshape(OCTETS, OCT_GROUPS, SSM_STATE, GROUP_CH)
                                  .transpose(0, 1, 3, 2))
    c_rows = lambda t: block_rows(t.reshape(OCTETS, OCT_GROUPS, GROUP_CH, SSM_STATE))
    bmat = jnp.stack([b_rows(bb_re), b_rows(bb_im)], axis=1)
    cmat = jnp.stack([c_rows(c_re), c_rows(c_im)], axis=1)

    kr = jnp.arange(SSM_L - 1, -1, -1, dtype=F32)[:, None, None]
    mag_r = jnp.exp(kr * zr)
    per_oct = lambda t: t.reshape(t.shape[:-2] + (OCTETS, OCT_STATE))
    pw_b = per_oct(jnp.stack([mag_r * jnp.cos(kr * zi), mag_r * jnp.sin(kr * zi)]))
    pw_c = per_oct(jnp.stack([pr[1:], pi[1:]]))
    a_chunk = per_oct(jnp.stack([pr[SSM_L], pi[SSM_L]]))
    return (imp.astype(BF16), bmat, cmat, pw_b.transpose(2, 0, 1, 3), pw_c.transpose(2, 0, 1, 3),
            a_chunk.transpose(1, 0, 2))


def _scaled_blocks(mat_ref, pw_ref, out_ref, im_sign):
    def block(s, carry):
        m_re, m_im = mat_ref[0], mat_ref[1]
        p_re, p_im = pw_ref[0, pl.ds(s, 1), :], pw_ref[1, pl.ds(s, 1), :]
        rows = pl.ds(pl.multiple_of(s * LANES, LANES), LANES)
        out_ref[rows, 0:OCT_STATE] = (m_re * p_re - m_im * p_im).astype(BF16)
        out_ref[rows, OCT_STATE:2 * OCT_STATE] = (im_sign * (m_re * p_im + m_im * p_re)).astype(BF16)
        return carry

    lax.fori_loop(0, SSM_L, block, 0)


def _s5_state_kernel(u_ref, bmat_ref, pw_ref, a_ref, xs_ref, bpow_scr, s_scr, x_scr):
    _scaled_blocks(bmat_ref, pw_ref, bpow_scr, 1.0)
    for r in range(0, SSM_ROWS, SSM_RT):
        s_loc = _dot(u_ref[r:r + SSM_RT, :], bpow_scr[...])
        for k in range(STATE_COLS):
            s_scr[k, r:r + SSM_RT, :] = s_loc[:, k * LANES:(k + 1) * LANES]
    a = a_ref[...]
    col = lambda r, k: jnp.broadcast_to(a[r:r + 1, k * LANES:(k + 1) * LANES], (BATCH, LANES))
    ar = [col(0, k) for k in range(RE_COLS)]
    ai = [col(1, k) for k in range(RE_COLS)]

    def step(c, carry):
        rows = pl.ds(c, BATCH, stride=SSM_NC)
        nxt_re, nxt_im = [], []
        for k in range(RE_COLS):
            xr, xi = carry[k], carry[RE_COLS + k]
            x_scr[k, rows, :] = xr
            x_scr[RE_COLS + k, rows, :] = xi
            nxt_re.append(ar[k] * xr - ai[k] * xi + s_scr[k, rows, :])
            nxt_im.append(ar[k] * xi + ai[k] * xr + s_scr[RE_COLS + k, rows, :])
        return tuple(nxt_re + nxt_im)

    zeros = jnp.zeros((BATCH, LANES), F32)
    lax.fori_loop(0, SSM_NC, step, (zeros,) * STATE_COLS, unroll=4)
    for k in range(STATE_COLS):
        xs_ref[:, k * LANES:(k + 1) * LANES] = x_scr[k].astype(BF16)


def _s5_states(u_oct, bmat, pw_b, a_chunk):
    per_oct = lambda *tail: pl.BlockSpec((None,) + tail, lambda j: (j,) + (0,) * len(tail))
    return pl.pallas_call(
        _s5_state_kernel,
        grid=(OCTETS,),
        in_specs=[per_oct(SSM_ROWS, OCT_W), per_oct(2, LANES, OCT_STATE),
                  per_oct(2, SSM_L, OCT_STATE), per_oct(2, OCT_STATE)],
        out_specs=per_oct(SSM_ROWS, 2 * OCT_STATE),
        out_shape=jax.ShapeDtypeStruct((OCTETS, SSM_ROWS, 2 * OCT_STATE), BF16),
        scratch_shapes=[pltpu.VMEM((OCT_W, 2 * OCT_STATE), BF16),
                        pltpu.VMEM((STATE_COLS, SSM_ROWS, LANES), F32),
                        pltpu.VMEM((STATE_COLS, SSM_ROWS, LANES), F32)],
        compiler_params=_params(),
        name="s5_states",
    )(u_oct, bmat, pw_b, a_chunk)


def _s5_out_kernel(u_ref, xs_ref, imp_ref, cmat_ref, pw_ref, y_ref, toep_scr, cpow_scr):
    @pl.when(pl.program_id(1) == 0)
    def _():
        toep_scr[...] = jnp.zeros(toep_scr.shape, BF16)
        for s in range(SSM_L):
            toep_scr[s * LANES:(s + 1) * LANES, s * LANES:] = imp_ref[:, :OCT_W - s * LANES]
        _scaled_blocks(cmat_ref, pw_ref, cpow_scr, -1.0)

    y = _dot(u_ref[...], toep_scr[...]) + lax.dot_general(
        xs_ref[...], cpow_scr[...], _NT, preferred_element_type=F32)
    y_ref[...] = y.astype(BF16)


def _s5_output(u_oct, xs, imp, cmat, pw_c):
    rows = lambda w: pl.BlockSpec((None, SSM_RT, w), lambda j, r: (j, r, 0))
    per_oct = lambda *tail: pl.BlockSpec((None,) + tail, lambda j, r: (j,) + (0,) * len(tail))
    return pl.pallas_call(
        _s5_out_kernel,
        grid=(OCTETS, SSM_ROWS // SSM_RT),
        in_specs=[rows(OCT_W), rows(2 * OCT_STATE), per_oct(LANES, OCT_W),
                  per_oct(2, LANES, OCT_STATE), per_oct(2, SSM_L, OCT_STATE)],
        out_specs=rows(OCT_W),
        out_shape=jax.ShapeDtypeStruct((OCTETS, SSM_ROWS, OCT_W), BF16),
        scratch_shapes=[pltpu.VMEM((OCT_W, OCT_W), BF16), pltpu.VMEM((OCT_W, 2 * OCT_STATE), BF16)],
        compiler_params=_params(2),
        name="s5_output",
    )(u_oct, xs, imp, cmat, pw_c)


def _gelu_tanh(x):
    return 0.5 * x * (1.0 + jnp.tanh(math.sqrt(2.0 / math.pi) * (x + 0.044715 * (x * x * x))))


def _tail(x, mixed, g_ref, wi_ref, wo_ref, p_ref, wg_ref, wp_ref, o_ref):
    x = x + _rms(mixed, g_ref[3:4, :])
    x = _ffn_residual(x, g_ref[4:5, :], g_ref[5:6, :], wi_ref, wo_ref)
    gate = _sigmoid(_dot(_rms(x, g_ref[6:7, :]).astype(BF16), wg_ref[...]))
    emb = _dot(p_ref[...].astype(BF16), wp_ref[...])
    o_ref[...] = x + _rms(gate * emb, g_ref[7:8, :])


def _tail_attn_kernel(a_ref, x_ref, wm_ref, g_ref, wi_ref, wo_ref, p_ref, wg_ref, wp_ref, o_ref):
    mixed = _dot(a_ref[...], wm_ref[...])
    _tail(x_ref[...], mixed, g_ref, wi_ref, wo_ref, p_ref, wg_ref, wp_ref, o_ref)


def _tail_ssm_kernel(y_ref, x_ref, wm_ref, bm_ref, g_ref, wi_ref, wo_ref, p_ref, wg_ref, wp_ref,
                     o_ref, y_scr):
    y = _from_super_rows(y_ref, y_scr)
    z = _dot(_gelu_tanh(y).astype(BF16), wm_ref[...]) + bm_ref[...]
    mixed = z[:, :D_MODEL] * _sigmoid(z[:, D_MODEL:])
    _tail(x_ref[...], mixed, g_ref, wi_ref, wo_ref, p_ref, wg_ref, wp_ref, o_ref)


def _layer_tail(mix, x, p, layer, g, w_mix, b_mix, w_in, w_out, w_gate, w_proj):
    ple_specs = [pl.BlockSpec((None, TM, PLE_DIM), lambda b, c: (layer, b * TILES_PER_SEQ + c, 0)),
                 _const_spec((D_MODEL, D_MODEL)), _const_spec((PLE_DIM, D_MODEL))]
    common = dict(grid=_TOKEN_GRID, out_specs=_TOKEN_TILE,
                  out_shape=jax.ShapeDtypeStruct((TOKENS, D_MODEL), F32),
                  compiler_params=_params(2))
    if b_mix is None:
        return pl.pallas_call(
            _tail_attn_kernel,
            in_specs=[_TOKEN_TILE, _TOKEN_TILE, _const_spec((D_MODEL, D_MODEL))] + _FFN_SPECS
            + ple_specs,
            name="attn_out_ffn_ple", **common,
        )(mix, x, w_mix, g, w_in, w_out, p, w_gate, w_proj)
    return pl.pallas_call(
        _tail_ssm_kernel,
        in_specs=[_SUPER_TILE, _TOKEN_TILE, _const_spec((D_MODEL, 2 * D_MODEL)),
                  _const_spec((1, 2 * D_MODEL))] + _FFN_SPECS + ple_specs,
        scratch_shapes=[pltpu.VMEM((OCTETS, TM, LANES), F32)],
        name="s5_glu_ffn_ple", **common,
    )(mix, x, w_mix, b_mix, g, w_in, w_out, p, w_gate, w_proj)


def kernel(x, p, norm_g, ffn_w_in, ffn_w_out, attn_w_qkv, attn_w_o, attn_lam, attn_subln_g,
           rel_bias, ssm_lam_re, ssm_lam_im, ssm_log_dt, ssm_b_re, ssm_b_im, ssm_c_re, ssm_c_im,
           ssm_d, ssm_w_glu, ssm_b_glu, ple_w_proj, ple_w_gate):
    x = x.reshape(TOKENS, D_MODEL)
    p = p.reshape(DEPTH, TOKENS, PLE_DIM)
    bias = _bias_tiles(rel_bias)
    bf = lambda w: w.astype(BF16)
    for i in range(DEPTH):
        g = norm_g[i]
        j = i // N_MIXERS
        if i % N_MIXERS == 0:
            lambda_init = 0.8 - 0.6 * math.exp(-0.3 * i)
            w = bf(attn_w_qkv[j])
            x, qt, k, vt = _layer_head(x, g, bf(ffn_w_in[i, 0]), bf(ffn_w_out[i, 0]),
                                       (w[:, :D_MODEL].T, w[:, D_MODEL:2 * D_MODEL],
                                        w[:, 2 * D_MODEL:].T))
            mix = _attention(qt, k, vt, bias, attn_lam[j], attn_subln_g[j].reshape(1, -1),
                             lambda_init)
            w_mix, b_mix = bf(attn_w_o[j]), None
        else:
            imp, bmat, cmat, pw_b, pw_c, a_chunk = _ssm_operators(
                ssm_lam_re[j], ssm_lam_im[j], ssm_log_dt[j], ssm_b_re[j], ssm_b_im[j],
                ssm_c_re[j], ssm_c_im[j], ssm_d[j])
            x, u_oct = _layer_head(x, g, bf(ffn_w_in[i, 0]), bf(ffn_w_out[i, 0]))
            xs = _s5_states(u_oct, bmat, pw_b, a_chunk)
            mix = _s5_output(u_oct, xs, imp, cmat, pw_c)
            w_mix, b_mix = bf(ssm_w_glu[j]), ssm_b_glu[j].reshape(1, -1)
        x = _layer_tail(mix, x, p, i, g, w_mix, b_mix, bf(ffn_w_in[i, 1]), bf(ffn_w_out[i, 1]),
                        bf(ple_w_gate[i]), bf(ple_w_proj[i]))
    return x.reshape(BATCH, SEQ, D_MODEL)
```

```python
import functools
import math

import numpy as np
import jax
import jax.numpy as jnp
from jax import lax
from jax.experimental import pallas as pl
from jax.experimental.pallas import tpu as pltpu

D_MODEL = 1024
BATCH = 8
SEQ = 4096
DEPTH = 4
N_MIXERS = 2
HEAD_DIM = 64
N_HEADS = D_MODEL // (2 * HEAD_DIM)
REL_BUCKETS = 32
REL_MAX_DIST = 128
GROUP_CH = 16
GROUPS = D_MODEL // GROUP_CH
SSM_STATE = 64
D_FF = 2816
FFN_RESIDUAL = 0.5
PLE_DIM = 256
N_NORMS = 8
RMS_EPS = 1e-6
NEG_INF = -1e30

TOKENS = BATCH * SEQ
F32 = jnp.float32
BF16 = jnp.bfloat16
LANES = 128

V7X_VMEM_BYTES = 64 * 1024 * 1024
VMEM_LIMIT = V7X_VMEM_BYTES - 8 * 1024 * 1024

TM = 512
TILES_PER_SEQ = SEQ // TM
MXU_TILE = 256
FF_SPLITS = (0, 6 * MXU_TILE, D_FF)
assert D_FF % MXU_TILE == 0
TQ = 512
TK = 512
BIAS_TILES = 3
TQ_HALF = TQ // 2
SUM_ROWS = 16
HEADS_PER_STEP = 2
EARLY = 2
LATE = 2
LOG2E = math.log2(math.e)

SSM_L = 16
SSM_NC = SEQ // SSM_L
OCTETS = D_MODEL // LANES
OCT_GROUPS = LANES // GROUP_CH
OCT_W = SSM_L * LANES
OCT_STATE = OCT_GROUPS * SSM_STATE
SSM_ROWS = BATCH * SSM_NC
SSM_RT = 512
CH_PER_TILE = TM // SSM_L
RE_COLS = OCT_STATE // LANES
STATE_COLS = 2 * RE_COLS


def _t5_thresholds():
    n = np.arange(0, 4 * REL_MAX_DIST)
    max_exact = REL_BUCKETS // 2
    nf = np.maximum(n, 1).astype(np.float64)
    large = max_exact + (np.log(nf / max_exact) / math.log(REL_MAX_DIST / max_exact)
                         * (REL_BUCKETS - max_exact)).astype(np.int32)
    bucket = np.where(n < max_exact, n, np.minimum(large, REL_BUCKETS - 1))
    return [int(np.argmax(bucket >= j)) for j in range(1, REL_BUCKETS)]


T5_THRESHOLDS = _t5_thresholds()
assert TQ == TK and T5_THRESHOLDS[-1] <= TK, "key blocks before i-1 sit in the last bucket"

_NT = (((1,), (1,)), ((), ()))


def _const_spec(shape):
    nd = len(shape)
    return pl.BlockSpec(shape, lambda *_: (0,) * nd, pipeline_mode=pl.Buffered(1))


def _params(n_axes=1):
    return pltpu.CompilerParams(dimension_semantics=("arbitrary",) * n_axes,
                                vmem_limit_bytes=VMEM_LIMIT)


def _rms(x, g):
    return x * lax.rsqrt(jnp.mean(x * x, axis=-1, keepdims=True) + RMS_EPS) * g


def _sigmoid(x):
    return 1.0 / (1.0 + jnp.exp(-x))


def _dot(a, b):
    return jnp.dot(a, b, preferred_element_type=F32)


_TOKEN_TILE = pl.BlockSpec((TM, D_MODEL), lambda b, c: (b * TILES_PER_SEQ + c, 0))
_FEATURE_TILE = pl.BlockSpec((D_MODEL, TM), lambda b, c: (0, b * TILES_PER_SEQ + c))
_SUPER_TILE = pl.BlockSpec((OCTETS, CH_PER_TILE, OCT_W),
                           lambda b, c: (0, b * TILES_PER_SEQ + c, 0))
_TOKEN_GRID = (BATCH, TILES_PER_SEQ)
_FFN_SPECS = [_const_spec((N_NORMS, D_MODEL)), _const_spec((D_MODEL, 2 * D_FF)),
              _const_spec((D_FF, D_MODEL))]


def _ffn_residual(x, g_pre, g_post, wi_ref, wo_ref):
    h = _rms(x, g_pre).astype(BF16)
    y = jnp.zeros((TM, D_MODEL), F32)
    for lo, hi in zip(FF_SPLITS[:-1], FF_SPLITS[1:]):
        a = _dot(h, wi_ref[:, lo:hi])
        u = _dot(h, wi_ref[:, D_FF + lo:D_FF + hi])
        act = (a * _sigmoid(a) * u).astype(BF16)
        y = y + _dot(act, wo_ref[lo:hi, :])
    return x + FFN_RESIDUAL * _rms(y, g_post)


def _to_super_rows(h, o_ref, h_scr):
    for j in range(OCTETS):
        h_scr[j] = h[:, j * LANES:(j + 1) * LANES]
    for j in range(OCTETS):
        for s in range(SSM_L):
            rows = h_scr[j, pl.ds(s, CH_PER_TILE, stride=SSM_L), :]
            o_ref[j, :, s * LANES:(s + 1) * LANES] = rows.astype(BF16)


def _from_super_rows(y_ref, y_scr):
    for j in range(OCTETS):
        for s in range(SSM_L):
            y_scr[j, pl.ds(s, CH_PER_TILE, stride=SSM_L), :] = (
                y_ref[j, :, s * LANES:(s + 1) * LANES].astype(F32))
    return jnp.concatenate([y_scr[j] for j in range(OCTETS)], axis=1)


def _head_attn_kernel(x_ref, g_ref, wi_ref, wo_ref, wqt_ref, wk_ref, wvt_ref,
                      xo_ref, qt_ref, k_ref, vt_ref):
    x = _ffn_residual(x_ref[...], g_ref[0:1, :], g_ref[1:2, :], wi_ref, wo_ref)
    xo_ref[...] = x
    h = _rms(x, g_ref[2:3, :]).astype(BF16)
    qt = lax.dot_general(wqt_ref[...], h, _NT, preferred_element_type=F32)
    qt_ref[...] = (qt * (HEAD_DIM ** -0.5 * LOG2E)).astype(BF16)
    k_ref[...] = _dot(h, wk_ref[...]).astype(BF16)
    vt_ref[...] = lax.dot_general(wvt_ref[...], h, _NT, preferred_element_type=F32).astype(BF16)


def _head_ssm_kernel(x_ref, g_ref, wi_ref, wo_ref, xo_ref, u_ref, h_scr):
    x = _ffn_residual(x_ref[...], g_ref[0:1, :], g_ref[1:2, :], wi_ref, wo_ref)
    xo_ref[...] = x
    _to_super_rows(_rms(x, g_ref[2:3, :]), u_ref, h_scr)


def _layer_head(x, g, w_in, w_out, qkv=None):
    x_out = jax.ShapeDtypeStruct((TOKENS, D_MODEL), F32)
    if qkv is None:
        return pl.pallas_call(
            _head_ssm_kernel,
            grid=_TOKEN_GRID,
            in_specs=[_TOKEN_TILE] + _FFN_SPECS,
            out_specs=(_TOKEN_TILE, _SUPER_TILE),
            out_shape=(x_out, jax.ShapeDtypeStruct((OCTETS, SSM_ROWS, OCT_W), BF16)),
            scratch_shapes=[pltpu.VMEM((OCTETS, TM, LANES), F32)],
            compiler_params=_params(2),
            name="ffn_s5in",
        )(x, g, w_in, w_out)
    feat = jax.ShapeDtypeStruct((D_MODEL, TOKENS), BF16)
    return pl.pallas_call(
        _head_attn_kernel,
        grid=_TOKEN_GRID,
        in_specs=[_TOKEN_TILE] + _FFN_SPECS + [_const_spec((D_MODEL, D_MODEL))] * 3,
        out_specs=(_TOKEN_TILE, _FEATURE_TILE, _TOKEN_TILE, _FEATURE_TILE),
        out_shape=(x_out, feat, jax.ShapeDtypeStruct((TOKENS, D_MODEL), BF16), feat),
        compiler_params=_params(2),
        name="ffn_qkv",
    )(x, g, w_in, w_out, *qkv)


def _bias_kernel(rel_ref, o_ref):
    c = pl.program_id(0)
    r = pl.program_id(1)
    ki = lax.broadcasted_iota(jnp.int32, (TK, TQ), 0)
    qi = lax.broadcasted_iota(jnp.int32, (TK, TQ), 1)
    d = qi - ki + r * TK
    val = jnp.full((TK, TQ), rel_ref[0, c], F32)
    for j, thr in enumerate(T5_THRESHOLDS, start=1):
        val = jnp.where(d >= thr, rel_ref[j, c], val)
    val = (val - rel_ref[REL_BUCKETS - 1, c]) * LOG2E
    o_ref[...] = jnp.where(d >= 0, val, NEG_INF).astype(BF16)


def _bias_tiles(rel_bias):
    tiles = pl.pallas_call(
        _bias_kernel,
        grid=(2 * N_HEADS, BIAS_TILES),
        in_specs=[pl.BlockSpec(memory_space=pltpu.SMEM)],
        out_specs=pl.BlockSpec((None, None, TK, TQ), lambda c, r: (c, r, 0, 0)),
        out_shape=jax.ShapeDtypeStruct((2 * N_HEADS, BIAS_TILES, TK, TQ), BF16),
        compiler_params=_params(2),
        name="t5_bias",
    )(rel_bias)
    return tiles.reshape(N_HEADS, 2, BIAS_TILES, TK, TQ)


def _attn_kernel(qt_ref, k_ref, vt_ref, bias_ref, lam_ref, sg_ref, o_ref, m_ref, acc_ref,
                 s_buf, p_buf, a_buf, *, lambda_init):
    i = pl.program_id(2)
    d = 2 * HEAD_DIM
    chains = [(hd, mi, slice(hf * TQ_HALF, (hf + 1) * TQ_HALF))
              for hd in range(HEADS_PER_STEP) for hf in range(2) for mi in range(2)]
    n = len(chains)

    qt = qt_ref[...]
    row = lax.broadcasted_iota(jnp.int32, (d, TQ), 0)
    zero = jnp.zeros((d, TQ), BF16)
    q_maps = [[jnp.where(row < HEAD_DIM, qt[hd * d:(hd + 1) * d], zero),
               jnp.where(row >= HEAD_DIM, qt[hd * d:(hd + 1) * d], zero)]
              for hd in range(HEADS_PER_STEP)]

    m_ref[...] = jnp.full(m_ref.shape, NEG_INF, F32)
    acc_ref[...] = jnp.zeros(acc_ref.shape, F32)
    sum_rows = jnp.where(lax.broadcasted_iota(jnp.int32, (SUM_ROWS, TK), 0) == 0, 1.0, 0.0).astype(BF16)

    def keys(j):
        return k_ref[pl.ds(pl.multiple_of(j * TK, TK), TK), :]

    def vals(j):
        vb = vt_ref[:, pl.ds(pl.multiple_of(j * TK, TK), TK)]
        return [jnp.concatenate([vb[hd * d:(hd + 1) * d], sum_rows], axis=0)
                for hd in range(HEADS_PER_STEP)]

    def scores(kb, j, chain):
        hd, mi, cols = chain
        tile = jnp.clip(i - j, 0, BIAS_TILES - 1)
        s = _dot(kb[:, hd * d:(hd + 1) * d], q_maps[hd][mi][:, cols])
        return s.astype(BF16) + bias_ref[hd, mi, tile, :, cols]

    def softmax(chain, s):
        hd, mi, cols = chain
        m_prev = m_ref[hd, mi, :, cols]
        m_new = jnp.maximum(m_prev, jnp.max(s, axis=0, keepdims=True).astype(F32))
        alpha = jnp.exp2(m_prev - m_new)
        p = jnp.exp2(s - m_new.astype(BF16))
        m_ref[hd, mi, :, cols] = m_new
        return p, alpha

    def values(vb, chain, p, alpha):
        hd, mi, cols = chain
        acc_ref[hd, mi, :, cols] = alpha * acc_ref[hd, mi, :, cols] + _dot(vb[hd], p)

    kb0 = keys(0)
    for e in range(EARLY):
        s_buf[e] = scores(kb0, 0, chains[e])
    for t in range(LATE):
        p_buf[t] = jnp.zeros((TK, TQ_HALF), BF16)
        a_buf[t] = jnp.ones((1, TQ_HALF), F32)

    def trip(j, carry):
        kb, vb = keys(j), vals(j)
        j_next = jnp.minimum(j + 1, i)
        kb_next, vb_prev = keys(j_next), vals(jnp.maximum(j - 1, 0))
        s_tiles, p_tiles = {}, {}
        for slot in range(n):
            ahead = slot + EARLY
            if ahead < n:
                s_tiles[ahead] = scores(kb, j, chains[ahead])
            else:
                s_next = scores(kb_next, j_next, chains[ahead - n])
            s = s_buf[slot] if slot < EARLY else s_tiles.pop(slot)
            p_tiles[slot] = softmax(chains[slot], s)
            if ahead >= n:
                s_buf[ahead - n] = s_next
            behind = slot - LATE
            if behind >= 0:
                values(vb, chains[behind], *p_tiles.pop(behind))
            else:
                values(vb_prev, chains[n + behind], p_buf[slot], a_buf[slot])
            if slot >= n - LATE:
                p_buf[slot - (n - LATE)], a_buf[slot - (n - LATE)] = p_tiles.pop(slot)
        return carry

    lax.fori_loop(0, i + 1, trip, 0)
    vb_last = vals(i)
    for t in range(LATE):
        values(vb_last, chains[n - LATE + t], p_buf[t], a_buf[t])

    lv = lam_ref[...]
    lam = (jnp.exp(jnp.sum(lv[0:1] * lv[1:2], keepdims=True))
           - jnp.exp(jnp.sum(lv[2:3] * lv[3:4], keepdims=True)) + lambda_init)
    for hd in range(HEADS_PER_STEP):
        ot = (acc_ref[hd, 0, 0:d, :] * (1.0 / acc_ref[hd, 0, d:d + 1, :])
              - lam * (acc_ref[hd, 1, 0:d, :] * (1.0 / acc_ref[hd, 1, d:d + 1, :])))
        o = _rms(ot.T, sg_ref[...]) * (1.0 - lambda_init)
        o_ref[:, hd * d:(hd + 1) * d] = o.astype(BF16)


def _attention(qt, k, vt, bias, lam_vecs, subln_g, lambda_init):
    nq = SEQ // TQ
    wide = HEADS_PER_STEP * 2 * HEAD_DIM
    return pl.pallas_call(
        functools.partial(_attn_kernel, lambda_init=lambda_init),
        grid=(BATCH, N_HEADS // HEADS_PER_STEP, nq),
        in_specs=[pl.BlockSpec((wide, TQ), lambda b, h, i: (h, b * nq + i)),
                  pl.BlockSpec((SEQ, wide), lambda b, h, i: (b, h)),
                  pl.BlockSpec((wide, SEQ), lambda b, h, i: (h, b)),
                  pl.BlockSpec((HEADS_PER_STEP, 2, BIAS_TILES, TK, TQ),
                               lambda b, h, i: (h, 0, 0, 0, 0)),
                  pl.BlockSpec((4, HEAD_DIM), lambda b, h, i: (0, 0)),
                  pl.BlockSpec((1, 2 * HEAD_DIM), lambda b, h, i: (0, 0))],
        out_specs=pl.BlockSpec((TQ, wide), lambda b, h, i: (b * nq + i, h)),
        out_shape=jax.ShapeDtypeStruct((TOKENS, D_MODEL), BF16),
        scratch_shapes=[pltpu.VMEM((HEADS_PER_STEP, 2, 1, TQ), F32),
                        pltpu.VMEM((HEADS_PER_STEP, 2, 2 * HEAD_DIM + SUM_ROWS, TQ), F32),
                        pltpu.VMEM((EARLY, TK, TQ_HALF), BF16), pltpu.VMEM((LATE, TK, TQ_HALF), BF16),
                        pltpu.VMEM((LATE, 1, TQ_HALF), F32)],
        compiler_params=_params(3),
        name="diff_attn",
    )(qt, k, vt, bias, lam_vecs, subln_g)


def _ssm_operators(lam_re, lam_im, log_dt, b_re, b_im, c_re, c_im, d_skip):
    hp = lax.Precision.HIGHEST
    lam_re, lam_im, log_dt, b_re, b_im, c_re, c_im, d_skip = lax.optimization_barrier(
        (lam_re, lam_im, log_dt, b_re, b_im, c_re, c_im, d_skip))
    dt = jnp.exp(log_dt)[:, None]
    zr, zi = lam_re * dt, lam_im * dt
    ks = jnp.arange(SSM_L + 1, dtype=F32)[:, None, None]
    mag = jnp.exp(ks * zr)
    pr, pi = mag * jnp.cos(ks * zi), mag * jnp.sin(ks * zi)
    nr = jnp.expm1(zr) * jnp.cos(zi) - 2.0 * jnp.sin(0.5 * zi) ** 2
    ni = pi[1]
    den = lam_re * lam_re + lam_im * lam_im
    fr, fi = (nr * lam_re + ni * lam_im) / den, (ni * lam_re - nr * lam_im) / den
    bb_re = fr[..., None] * b_re - fi[..., None] * b_im
    bb_im = fr[..., None] * b_im + fi[..., None] * b_re
    ca_re = c_re[None] * pr[:SSM_L, :, None, :] - c_im[None] * pi[:SSM_L, :, None, :]
    ca_im = c_re[None] * pi[:SSM_L, :, None, :] + c_im[None] * pr[:SSM_L, :, None, :]

    kern = (jnp.einsum('tgop,gph->tgoh', ca_re, bb_re, precision=hp)
            - jnp.einsum('tgop,gph->tgoh', ca_im, bb_im, precision=hp))
    eye_h = jnp.eye(GROUP_CH, dtype=F32)
    kern = kern.at[0].add(d_skip.reshape(GROUPS, GROUP_CH)[:, :, None] * eye_h[None])

    same = jnp.eye(OCT_GROUPS, dtype=F32)
    kj = kern.reshape(SSM_L, OCTETS, OCT_GROUPS, GROUP_CH, GROUP_CH)
    kj = lax.optimization_barrier(kj.transpose(1, 2, 4, 0, 3))
    imp = kj[:, :, :, :, None, :] * same[None, :, None, None, :, None]
    imp = imp.reshape(OCTETS, LANES, OCT_W)

    def block_rows(t):
        t = lax.optimization_barrier(t)
        return (t[:, :, :, None, :] * same[None, :, None, :, None]).reshape(OCTETS, LANES, OCT_STATE)

    b_rows = lambda t: block_rows(t.reshape(OCTETS, OCT_GROUPS, SSM_STATE, GROUP_CH)
                                  .transpose(0, 1, 3, 2))
    c_rows = lambda t: block_rows(t.reshape(OCTETS, OCT_GROUPS, GROUP_CH, SSM_STATE))
    bmat = jnp.stack([b_rows(bb_re), b_rows(bb_im)], axis=1)
    cmat = jnp.stack([c_rows(c_re), c_rows(c_im)], axis=1)

    kr = jnp.arange(SSM_L - 1, -1, -1, dtype=F32)[:, None, None]
    mag_r = jnp.exp(kr * zr)
    per_oct = lambda t: t.reshape(t.shape[:-2] + (OCTETS, OCT_STATE))
    pw_b = per_oct(jnp.stack([mag_r * jnp.cos(kr * zi), mag_r * jnp.sin(kr * zi)]))
    pw_c = per_oct(jnp.stack([pr[1:], pi[1:]]))
    a_chunk = per_oct(jnp.stack([pr[SSM_L], pi[SSM_L]]))
    return (imp.astype(BF16), bmat, cmat, pw_b.transpose(2, 0, 1, 3), pw_c.transpose(2, 0, 1, 3),
            a_chunk.transpose(1, 0, 2))


def _scaled_blocks(mat_ref, pw_ref, out_ref, im_sign):
    def block(s, carry):
        m_re, m_im = mat_ref[0], mat_ref[1]
        p_re, p_im = pw_ref[0, pl.ds(s, 1), :], pw_ref[1, pl.ds(s, 1), :]
        rows = pl.ds(pl.multiple_of(s * LANES, LANES), LANES)
        out_ref[rows, 0:OCT_STATE] = (m_re * p_re - m_im * p_im).astype(BF16)
        out_ref[rows, OCT_STATE:2 * OCT_STATE] = (im_sign * (m_re * p_im + m_im * p_re)).astype(BF16)
        return carry

    lax.fori_loop(0, SSM_L, block, 0)


def _s5_state_kernel(u_ref, bmat_ref, pw_ref, a_ref, xs_ref, bpow_scr, s_scr, x_scr):
    _scaled_blocks(bmat_ref, pw_ref, bpow_scr, 1.0)
    for r in range(0, SSM_ROWS, SSM_RT):
        s_loc = _dot(u_ref[r:r + SSM_RT, :], bpow_scr[...])
        for k in range(STATE_COLS):
            s_scr[k, r:r + SSM_RT, :] = s_loc[:, k * LANES:(k + 1) * LANES]
    a = a_ref[...]
    col = lambda r, k: jnp.broadcast_to(a[r:r + 1, k * LANES:(k + 1) * LANES], (BATCH, LANES))
    ar = [col(0, k) for k in range(RE_COLS)]
    ai = [col(1, k) for k in range(RE_COLS)]

    def step(c, carry):
        rows = pl.ds(c, BATCH, stride=SSM_NC)
        nxt_re, nxt_im = [], []
        for k in range(RE_COLS):
            xr, xi = carry[k], carry[RE_COLS + k]
            x_scr[k, rows, :] = xr
            x_scr[RE_COLS + k, rows, :] = xi
            nxt_re.append(ar[k] * xr - ai[k] * xi + s_scr[k, rows, :])
            nxt_im.append(ar[k] * xi + ai[k] * xr + s_scr[RE_COLS + k, rows, :])
        return tuple(nxt_re + nxt_im)

    zeros = jnp.zeros((BATCH, LANES), F32)
    lax.fori_loop(0, SSM_NC, step, (zeros,) * STATE_COLS, unroll=4)
    for k in range(STATE_COLS):
        xs_ref[:, k * LANES:(k + 1) * LANES] = x_scr[k].astype(BF16)


def _s5_states(u_oct, bmat, pw_b, a_chunk):
    per_oct = lambda *tail: pl.BlockSpec((None,) + tail, lambda j: (j,) + (0,) * len(tail))
    return pl.pallas_call(
        _s5_state_kernel,
        grid=(OCTETS,),
        in_specs=[per_oct(SSM_ROWS, OCT_W), per_oct(2, LANES, OCT_STATE),
                  per_oct(2, SSM_L, OCT_STATE), per_oct(2, OCT_STATE)],
        out_specs=per_oct(SSM_ROWS, 2 * OCT_STATE),
        out_shape=jax.ShapeDtypeStruct((OCTETS, SSM_ROWS, 2 * OCT_STATE), BF16),
        scratch_shapes=[pltpu.VMEM((OCT_W, 2 * OCT_STATE), BF16),
                        pltpu.VMEM((STATE_COLS, SSM_ROWS, LANES), F32),
                        pltpu.VMEM((STATE_COLS, SSM_ROWS, LANES), F32)],
        compiler_params=_params(),
        name="s5_states",
    )(u_oct, bmat, pw_b, a_chunk)


def _s5_out_kernel(u_ref, xs_ref, imp_ref, cmat_ref, pw_ref, y_ref, toep_scr, cpow_scr):
    @pl.when(pl.program_id(1) == 0)
    def _():
        toep_scr[...] = jnp.zeros(toep_scr.shape, BF16)
        for s in range(SSM_L):
            toep_scr[s * LANES:(s + 1) * LANES, s * LANES:] = imp_ref[:, :OCT_W - s * LANES]
        _scaled_blocks(cmat_ref, pw_ref, cpow_scr, -1.0)

    y = _dot(u_ref[...], toep_scr[...]) + lax.dot_general(
        xs_ref[...], cpow_scr[...], _NT, preferred_element_type=F32)
    y_ref[...] = y.astype(BF16)


def _s5_output(u_oct, xs, imp, cmat, pw_c):
    rows = lambda w: pl.BlockSpec((None, SSM_RT, w), lambda j, r: (j, r, 0))
    per_oct = lambda *tail: pl.BlockSpec((None,) + tail, lambda j, r: (j,) + (0,) * len(tail))
    return pl.pallas_call(
        _s5_out_kernel,
        grid=(OCTETS, SSM_ROWS // SSM_RT),
        in_specs=[rows(OCT_W), rows(2 * OCT_STATE), per_oct(LANES, OCT_W),
                  per_oct(2, LANES, OCT_STATE), per_oct(2, SSM_L, OCT_STATE)],
        out_specs=rows(OCT_W),
        out_shape=jax.ShapeDtypeStruct((OCTETS, SSM_ROWS, OCT_W), BF16),
        scratch_shapes=[pltpu.VMEM((OCT_W, OCT_W), BF16), pltpu.VMEM((OCT_W, 2 * OCT_STATE), BF16)],
        compiler_params=_params(2),
        name="s5_output",
    )(u_oct, xs, imp, cmat, pw_c)


def _gelu_tanh(x):
    return 0.5 * x * (1.0 + jnp.tanh(math.sqrt(2.0 / math.pi) * (x + 0.044715 * (x * x * x))))


def _tail(x, mixed, g_ref, wi_ref, wo_ref, p_ref, wg_ref, wp_ref, o_ref):
    x = x + _rms(mixed, g_ref[3:4, :])
    x = _ffn_residual(x, g_ref[4:5, :], g_ref[5:6, :], wi_ref, wo_ref)
    gate = _sigmoid(_dot(_rms(x, g_ref[6:7, :]).astype(BF16), wg_ref[...]))
    emb = _dot(p_ref[...].astype(BF16), wp_ref[...])
    o_ref[...] = x + _rms(gate * emb, g_ref[7:8, :])


def _tail_attn_kernel(a_ref, x_ref, wm_ref, g_ref, wi_ref, wo_ref, p_ref, wg_ref, wp_ref, o_ref):
    mixed = _dot(a_ref[...], wm_ref[...])
    _tail(x_ref[...], mixed, g_ref, wi_ref, wo_ref, p_ref, wg_ref, wp_ref, o_ref)


def _tail_ssm_kernel(y_ref, x_ref, wm_ref, bm_ref, g_ref, wi_ref, wo_ref, p_ref, wg_ref, wp_ref,
                     o_ref, y_scr):
    y = _from_super_rows(y_ref, y_scr)
    z = _dot(_gelu_tanh(y).astype(BF16), wm_ref[...]) + bm_ref[...]
    mixed = z[:, :D_MODEL] * _sigmoid(z[:, D_MODEL:])
    _tail(x_ref[...], mixed, g_ref, wi_ref, wo_ref, p_ref, wg_ref, wp_ref, o_ref)


def _layer_tail(mix, x, p, layer, g, w_mix, b_mix, w_in, w_out, w_gate, w_proj):
    ple_specs = [pl.BlockSpec((None, TM, PLE_DIM), lambda b, c: (layer, b * TILES_PER_SEQ + c, 0)),
                 _const_spec((D_MODEL, D_MODEL)), _const_spec((PLE_DIM, D_MODEL))]
    common = dict(grid=_TOKEN_GRID, out_specs=_TOKEN_TILE,
                  out_shape=jax.ShapeDtypeStruct((TOKENS, D_MODEL), F32),
                  compiler_params=_params(2))
    if b_mix is None:
        return pl.pallas_call(
            _tail_attn_kernel,
            in_specs=[_TOKEN_TILE, _TOKEN_TILE, _const_spec((D_MODEL, D_MODEL))] + _FFN_SPECS
            + ple_specs,
            name="attn_out_ffn_ple", **common,
        )(mix, x, w_mix, g, w_in, w_out, p, w_gate, w_proj)
    return pl.pallas_call(
        _tail_ssm_kernel,
        in_specs=[_SUPER_TILE, _TOKEN_TILE, _const_spec((D_MODEL, 2 * D_MODEL)),
                  _const_spec((1, 2 * D_MODEL))] + _FFN_SPECS + ple_specs,
        scratch_shapes=[pltpu.VMEM((OCTETS, TM, LANES), F32)],
        name="s5_glu_ffn_ple", **common,
    )(mix, x, w_mix, b_mix, g, w_in, w_out, p, w_gate, w_proj)


def kernel(x, p, norm_g, ffn_w_in, ffn_w_out, attn_w_qkv, attn_w_o, attn_lam, attn_subln_g,
           rel_bias, ssm_lam_re, ssm_lam_im, ssm_log_dt, ssm_b_re, ssm_b_im, ssm_c_re, ssm_c_im,
           ssm_d, ssm_w_glu, ssm_b_glu, ple_w_proj, ple_w_gate):
    x = x.reshape(TOKENS, D_MODEL)
    p = p.reshape(DEPTH, TOKENS, PLE_DIM)
    bias = _bias_tiles(rel_bias)
    bf = lambda w: w.astype(BF16)
    for i in range(DEPTH):
        g = norm_g[i]
        j = i // N_MIXERS
        if i % N_MIXERS == 0:
            lambda_init = 0.8 - 0.6 * math.exp(-0.3 * i)
            w = bf(attn_w_qkv[j])
            x, qt, k, vt = _layer_head(x, g, bf(ffn_w_in[i, 0]), bf(ffn_w_out[i, 0]),
                                       (w[:, :D_MODEL].T, w[:, D_MODEL:2 * D_MODEL],
                                        w[:, 2 * D_MODEL:].T))
            mix = _attention(qt, k, vt, bias, attn_lam[j], attn_subln_g[j].reshape(1, -1),
                             lambda_init)
            w_mix, b_mix = bf(attn_w_o[j]), None
        else:
            imp, bmat, cmat, pw_b, pw_c, a_chunk = _ssm_operators(
                ssm_lam_re[j], ssm_lam_im[j], ssm_log_dt[j], ssm_b_re[j], ssm_b_im[j],
                ssm_c_re[j], ssm_c_im[j], ssm_d[j])
            x, u_oct = _layer_head(x, g, bf(ffn_w_in[i, 0]), bf(ffn_w_out[i, 0]))
            xs = _s5_states(u_oct, bmat, pw_b, a_chunk)
            mix = _s5_output(u_oct, xs, imp, cmat, pw_c)
            w_mix, b_mix = bf(ssm_w_glu[j]), ssm_b_glu[j].reshape(1, -1)
        x = _layer_tail(mix, x, p, i, g, w_mix, b_mix, bf(ffn_w_in[i, 1]), bf(ffn_w_out[i, 1]),
                        bf(ple_w_gate[i]), bf(ple_w_proj[i]))
    return x.reshape(BATCH, SEQ, D_MODEL)
```

```python
import functools
import math

import numpy as np
import jax
import jax.numpy as jnp
from jax import lax
from jax.experimental import pallas as pl
from jax.experimental.pallas import tpu as pltpu

D_MODEL = 1024
BATCH = 8
SEQ = 4096
DEPTH = 4
N_MIXERS = 2
HEAD_DIM = 64
N_HEADS = D_MODEL // (2 * HEAD_DIM)
REL_BUCKETS = 32
REL_MAX_DIST = 128
GROUP_CH = 16
GROUPS = D_MODEL // GROUP_CH
SSM_STATE = 64
D_FF = 2816
FFN_RESIDUAL = 0.5
PLE_DIM = 256
N_NORMS = 8
RMS_EPS = 1e-6
NEG_INF = -1e30

TOKENS = BATCH * SEQ
F32 = jnp.float32
BF16 = jnp.bfloat16
LANES = 128

V7X_VMEM_BYTES = 64 * 1024 * 1024
VMEM_LIMIT = V7X_VMEM_BYTES - 8 * 1024 * 1024

TM = 512
TILES_PER_SEQ = SEQ // TM
MXU_TILE = 256
FF_SPLITS = (0, 6 * MXU_TILE, D_FF)
assert D_FF % MXU_TILE == 0
TQ = 512
TK = 512
BIAS_TILES = 3
TQ_HALF = TQ // 2
SUM_ROWS = 16
HEADS_PER_STEP = 4
EARLY = 2
LATE = 2
LOG2E = math.log2(math.e)

SSM_L = 16
SSM_NC = SEQ // SSM_L
OCTETS = D_MODEL // LANES
OCT_GROUPS = LANES // GROUP_CH
OCT_W = SSM_L * LANES
OCT_STATE = OCT_GROUPS * SSM_STATE
SSM_ROWS = BATCH * SSM_NC
SSM_RT = 512
CH_PER_TILE = TM // SSM_L
RE_COLS = OCT_STATE // LANES
STATE_COLS = 2 * RE_COLS


def _t5_thresholds():
    n = np.arange(0, 4 * REL_MAX_DIST)
    max_exact = REL_BUCKETS // 2
    nf = np.maximum(n, 1).astype(np.float64)
    large = max_exact + (np.log(nf / max_exact) / math.log(REL_MAX_DIST / max_exact)
                         * (REL_BUCKETS - max_exact)).astype(np.int32)
    bucket = np.where(n < max_exact, n, np.minimum(large, REL_BUCKETS - 1))
    return [int(np.argmax(bucket >= j)) for j in range(1, REL_BUCKETS)]


T5_THRESHOLDS = _t5_thresholds()
assert TQ == TK and T5_THRESHOLDS[-1] <= TK, "key blocks before i-1 sit in the last bucket"

_NT = (((1,), (1,)), ((), ()))


def _const_spec(shape):
    nd = len(shape)
    return pl.BlockSpec(shape, lambda *_: (0,) * nd, pipeline_mode=pl.Buffered(1))


def _params(n_axes=1):
    return pltpu.CompilerParams(dimension_semantics=("arbitrary",) * n_axes,
                                vmem_limit_bytes=VMEM_LIMIT)


def _rms(x, g):
    return x * lax.rsqrt(jnp.mean(x * x, axis=-1, keepdims=True) + RMS_EPS) * g


def _sigmoid(x):
    return 1.0 / (1.0 + jnp.exp(-x))


def _dot(a, b):
    return jnp.dot(a, b, preferred_element_type=F32)


_TOKEN_TILE = pl.BlockSpec((TM, D_MODEL), lambda b, c: (b * TILES_PER_SEQ + c, 0))
_FEATURE_TILE = pl.BlockSpec((D_MODEL, TM), lambda b, c: (0, b * TILES_PER_SEQ + c))
_SUPER_TILE = pl.BlockSpec((OCTETS, CH_PER_TILE, OCT_W),
                           lambda b, c: (0, b * TILES_PER_SEQ + c, 0))
_TOKEN_GRID = (BATCH, TILES_PER_SEQ)
_FFN_SPECS = [_const_spec((N_NORMS, D_MODEL)), _const_spec((D_MODEL, 2 * D_FF)),
              _const_spec((D_FF, D_MODEL))]


def _ffn_residual(x, g_pre, g_post, wi_ref, wo_ref):
    h = _rms(x, g_pre).astype(BF16)
    y = jnp.zeros((TM, D_MODEL), F32)
    for lo, hi in zip(FF_SPLITS[:-1], FF_SPLITS[1:]):
        a = _dot(h, wi_ref[:, lo:hi])
        u = _dot(h, wi_ref[:, D_FF + lo:D_FF + hi])
        act = (a * _sigmoid(a) * u).astype(BF16)
        y = y + _dot(act, wo_ref[lo:hi, :])
    return x + FFN_RESIDUAL * _rms(y, g_post)


def _to_super_rows(h, o_ref, h_scr):
    for j in range(OCTETS):
        h_scr[j] = h[:, j * LANES:(j + 1) * LANES]
    for j in range(OCTETS):
        for s in range(SSM_L):
            rows = h_scr[j, pl.ds(s, CH_PER_TILE, stride=SSM_L), :]
            o_ref[j, :, s * LANES:(s + 1) * LANES] = rows.astype(BF16)


def _from_super_rows(y_ref, y_scr):
    for j in range(OCTETS):
        for s in range(SSM_L):
            y_scr[j, pl.ds(s, CH_PER_TILE, stride=SSM_L), :] = (
                y_ref[j, :, s * LANES:(s + 1) * LANES].astype(F32))
    return jnp.concatenate([y_scr[j] for j in range(OCTETS)], axis=1)


def _head_attn_kernel(x_ref, g_ref, wi_ref, wo_ref, wqt_ref, wk_ref, wvt_ref,
                      xo_ref, qt_ref, k_ref, vt_ref):
    x = _ffn_residual(x_ref[...], g_ref[0:1, :], g_ref[1:2, :], wi_ref, wo_ref)
    xo_ref[...] = x
    h = _rms(x, g_ref[2:3, :]).astype(BF16)
    qt = lax.dot_general(wqt_ref[...], h, _NT, preferred_element_type=F32)
    qt_ref[...] = (qt * (HEAD_DIM ** -0.5 * LOG2E)).astype(BF16)
    k_ref[...] = _dot(h, wk_ref[...]).astype(BF16)
    vt_ref[...] = lax.dot_general(wvt_ref[...], h, _NT, preferred_element_type=F32).astype(BF16)


def _head_ssm_kernel(x_ref, g_ref, wi_ref, wo_ref, xo_ref, u_ref, h_scr):
    x = _ffn_residual(x_ref[...], g_ref[0:1, :], g_ref[1:2, :], wi_ref, wo_ref)
    xo_ref[...] = x
    _to_super_rows(_rms(x, g_ref[2:3, :]), u_ref, h_scr)


def _layer_head(x, g, w_in, w_out, qkv=None):
    x_out = jax.ShapeDtypeStruct((TOKENS, D_MODEL), F32)
    if qkv is None:
        return pl.pallas_call(
            _head_ssm_kernel,
            grid=_TOKEN_GRID,
            in_specs=[_TOKEN_TILE] + _FFN_SPECS,
            out_specs=(_TOKEN_TILE, _SUPER_TILE),
            out_shape=(x_out, jax.ShapeDtypeStruct((OCTETS, SSM_ROWS, OCT_W), BF16)),
            scratch_shapes=[pltpu.VMEM((OCTETS, TM, LANES), F32)],
            compiler_params=_params(2),
            name="ffn_s5in",
        )(x, g, w_in, w_out)
    feat = jax.ShapeDtypeStruct((D_MODEL, TOKENS), BF16)
    return pl.pallas_call(
        _head_attn_kernel,
        grid=_TOKEN_GRID,
        in_specs=[_TOKEN_TILE] + _FFN_SPECS + [_const_spec((D_MODEL, D_MODEL))] * 3,
        out_specs=(_TOKEN_TILE, _FEATURE_TILE, _TOKEN_TILE, _FEATURE_TILE),
        out_shape=(x_out, feat, jax.ShapeDtypeStruct((TOKENS, D_MODEL), BF16), feat),
        compiler_params=_params(2),
        name="ffn_qkv",
    )(x, g, w_in, w_out, *qkv)


def _bias_kernel(rel_ref, o_ref):
    c = pl.program_id(0)
    r = pl.program_id(1)
    ki = lax.broadcasted_iota(jnp.int32, (TK, TQ), 0)
    qi = lax.broadcasted_iota(jnp.int32, (TK, TQ), 1)
    d = qi - ki + r * TK
    val = jnp.full((TK, TQ), rel_ref[0, c], F32)
    for j, thr in enumerate(T5_THRESHOLDS, start=1):
        val = jnp.where(d >= thr, rel_ref[j, c], val)
    val = (val - rel_ref[REL_BUCKETS - 1, c]) * LOG2E
    o_ref[...] = jnp.where(d >= 0, val, NEG_INF).astype(BF16)


def _bias_tiles(rel_bias):
    tiles = pl.pallas_call(
        _bias_kernel,
        grid=(2 * N_HEADS, BIAS_TILES),
        in_specs=[pl.BlockSpec(memory_space=pltpu.SMEM)],
        out_specs=pl.BlockSpec((None, None, TK, TQ), lambda c, r: (c, r, 0, 0)),
        out_shape=jax.ShapeDtypeStruct((2 * N_HEADS, BIAS_TILES, TK, TQ), BF16),
        compiler_params=_params(2),
        name="t5_bias",
    )(rel_bias)
    return tiles.reshape(N_HEADS, 2, BIAS_TILES, TK, TQ)


def _attn_kernel(qt_ref, k_ref, vt_ref, bias_ref, lam_ref, sg_ref, o_ref, m_ref, acc_ref,
                 s_buf, p_buf, a_buf, *, lambda_init):
    i = pl.program_id(2)
    d = 2 * HEAD_DIM
    chains = [(hd, mi, slice(hf * TQ_HALF, (hf + 1) * TQ_HALF))
              for hd in range(HEADS_PER_STEP) for hf in range(2) for mi in range(2)]
    n = len(chains)

    qt = qt_ref[...]
    row = lax.broadcasted_iota(jnp.int32, (d, TQ), 0)
    zero = jnp.zeros((d, TQ), BF16)
    q_maps = [[jnp.where(row < HEAD_DIM, qt[hd * d:(hd + 1) * d], zero),
               jnp.where(row >= HEAD_DIM, qt[hd * d:(hd + 1) * d], zero)]
              for hd in range(HEADS_PER_STEP)]

    m_ref[...] = jnp.full(m_ref.shape, NEG_INF, F32)
    acc_ref[...] = jnp.zeros(acc_ref.shape, F32)
    sum_rows = jnp.where(lax.broadcasted_iota(jnp.int32, (SUM_ROWS, TK), 0) == 0, 1.0, 0.0).astype(BF16)

    def keys(j):
        return k_ref[pl.ds(pl.multiple_of(j * TK, TK), TK), :]

    def vals(j):
        vb = vt_ref[:, pl.ds(pl.multiple_of(j * TK, TK), TK)]
        return [jnp.concatenate([vb[hd * d:(hd + 1) * d], sum_rows], axis=0)
                for hd in range(HEADS_PER_STEP)]

    def scores(kb, j, chain):
        hd, mi, cols = chain
        tile = jnp.clip(i - j, 0, BIAS_TILES - 1)
        s = _dot(kb[:, hd * d:(hd + 1) * d], q_maps[hd][mi][:, cols])
        return s.astype(BF16) + bias_ref[hd, mi, tile, :, cols]

    def softmax(chain, s):
        hd, mi, cols = chain
        m_prev = m_ref[hd, mi, :, cols]
        m_new = jnp.maximum(m_prev, jnp.max(s, axis=0, keepdims=True).astype(F32))
        alpha = jnp.exp2(m_prev - m_new)
        p = jnp.exp2(s - m_new.astype(BF16))
        m_ref[hd, mi, :, cols] = m_new
        return p, alpha

    def values(vb, chain, p, alpha):
        hd, mi, cols = chain
        acc_ref[hd, mi, :, cols] = alpha * acc_ref[hd, mi, :, cols] + _dot(vb[hd], p)

    kb0 = keys(0)
    for e in range(EARLY):
        s_buf[e] = scores(kb0, 0, chains[e])
    for t in range(LATE):
        p_buf[t] = jnp.zeros((TK, TQ_HALF), BF16)
        a_buf[t] = jnp.ones((1, TQ_HALF), F32)

    def trip(j, carry):
        kb, vb = keys(j), vals(j)
        j_next = jnp.minimum(j + 1, i)
        kb_next, vb_prev = keys(j_next), vals(jnp.maximum(j - 1, 0))
        s_tiles, p_tiles = {}, {}
        for slot in range(n):
            ahead = slot + EARLY
            if ahead < n:
                s_tiles[ahead] = scores(kb, j, chains[ahead])
            else:
                s_next = scores(kb_next, j_next, chains[ahead - n])
            s = s_buf[slot] if slot < EARLY else s_tiles.pop(slot)
            p_tiles[slot] = softmax(chains[slot], s)
            if ahead >= n:
                s_buf[ahead - n] = s_next
            behind = slot - LATE
            if behind >= 0:
                values(vb, chains[behind], *p_tiles.pop(behind))
            else:
                values(vb_prev, chains[n + behind], p_buf[slot], a_buf[slot])
            if slot >= n - LATE:
                p_buf[slot - (n - LATE)], a_buf[slot - (n - LATE)] = p_tiles.pop(slot)
        return carry

    lax.fori_loop(0, i + 1, trip, 0)
    vb_last = vals(i)
    for t in range(LATE):
        values(vb_last, chains[n - LATE + t], p_buf[t], a_buf[t])

    lv = lam_ref[...]
    lam = (jnp.exp(jnp.sum(lv[0:1] * lv[1:2], keepdims=True))
           - jnp.exp(jnp.sum(lv[2:3] * lv[3:4], keepdims=True)) + lambda_init)
    for hd in range(HEADS_PER_STEP):
        ot = (acc_ref[hd, 0, 0:d, :] * (1.0 / acc_ref[hd, 0, d:d + 1, :])
              - lam * (acc_ref[hd, 1, 0:d, :] * (1.0 / acc_ref[hd, 1, d:d + 1, :])))
        o = _rms(ot.T, sg_ref[...]) * (1.0 - lambda_init)
        o_ref[:, hd * d:(hd + 1) * d] = o.astype(BF16)


def _attention(qt, k, vt, bias, lam_vecs, subln_g, lambda_init):
    nq = SEQ // TQ
    wide = HEADS_PER_STEP * 2 * HEAD_DIM
    return pl.pallas_call(
        functools.partial(_attn_kernel, lambda_init=lambda_init),
        grid=(BATCH, N_HEADS // HEADS_PER_STEP, nq),
        in_specs=[pl.BlockSpec((wide, TQ), lambda b, h, i: (h, b * nq + i)),
                  pl.BlockSpec((SEQ, wide), lambda b, h, i: (b, h)),
                  pl.BlockSpec((wide, SEQ), lambda b, h, i: (h, b)),
                  pl.BlockSpec((HEADS_PER_STEP, 2, BIAS_TILES, TK, TQ),
                               lambda b, h, i: (h, 0, 0, 0, 0)),
                  pl.BlockSpec((4, HEAD_DIM), lambda b, h, i: (0, 0)),
                  pl.BlockSpec((1, 2 * HEAD_DIM), lambda b, h, i: (0, 0))],
        out_specs=pl.BlockSpec((TQ, wide), lambda b, h, i: (b * nq + i, h)),
        out_shape=jax.ShapeDtypeStruct((TOKENS, D_MODEL), BF16),
        scratch_shapes=[pltpu.VMEM((HEADS_PER_STEP, 2, 1, TQ), F32),
                        pltpu.VMEM((HEADS_PER_STEP, 2, 2 * HEAD_DIM + SUM_ROWS, TQ), F32),
                        pltpu.VMEM((EARLY, TK, TQ_HALF), BF16), pltpu.VMEM((LATE, TK, TQ_HALF), BF16),
                        pltpu.VMEM((LATE, 1, TQ_HALF), F32)],
        compiler_params=_params(3),
        name="diff_attn",
    )(qt, k, vt, bias, lam_vecs, subln_g)


def _ssm_operators(lam_re, lam_im, log_dt, b_re, b_im, c_re, c_im, d_skip):
    hp = lax.Precision.HIGHEST
    lam_re, lam_im, log_dt, b_re, b_im, c_re, c_im, d_skip = lax.optimization_barrier(
        (lam_re, lam_im, log_dt, b_re, b_im, c_re, c_im, d_skip))
    dt = jnp.exp(log_dt)[:, None]
    zr, zi = lam_re * dt, lam_im * dt
    ks = jnp.arange(SSM_L + 1, dtype=F32)[:, None, None]
    mag = jnp.exp(ks * zr)
    pr, pi = mag * jnp.cos(ks * zi), mag * jnp.sin(ks * zi)
    nr = jnp.expm1(zr) * jnp.cos(zi) - 2.0 * jnp.sin(0.5 * zi) ** 2
    ni = pi[1]
    den = lam_re * lam_re + lam_im * lam_im
    fr, fi = (nr * lam_re + ni * lam_im) / den, (ni * lam_re - nr * lam_im) / den
    bb_re = fr[..., None] * b_re - fi[..., None] * b_im
    bb_im = fr[..., None] * b_im + fi[..., None] * b_re
    ca_re = c_re[None] * pr[:SSM_L, :, None, :] - c_im[None] * pi[:SSM_L, :, None, :]
    ca_im = c_re[None] * pi[:SSM_L, :, None, :] + c_im[None] * pr[:SSM_L, :, None, :]

    kern = (jnp.einsum('tgop,gph->tgoh', ca_re, bb_re, precision=hp)
            - jnp.einsum('tgop,gph->tgoh', ca_im, bb_im, precision=hp))
    eye_h = jnp.eye(GROUP_CH, dtype=F32)
    kern = kern.at[0].add(d_skip.reshape(GROUPS, GROUP_CH)[:, :, None] * eye_h[None])

    same = jnp.eye(OCT_GROUPS, dtype=F32)
    kj = kern.reshape(SSM_L, OCTETS, OCT_GROUPS, GROUP_CH, GROUP_CH)
    kj = lax.optimization_barrier(kj.transpose(1, 2, 4, 0, 3))
    imp = kj[:, :, :, :, None, :] * same[None, :, None, None, :, None]
    imp = imp.reshape(OCTETS, LANES, OCT_W)

    def block_rows(t):
        t = lax.optimization_barrier(t)
        return (t[:, :, :, None, :] * same[None, :, None, :, None]).reshape(OCTETS, LANES, OCT_STATE)

    b_rows = lambda t: block_rows(t.reshape(OCTETS, OCT_GROUPS, SSM_STATE, GROUP_CH)
                                  .transpose(0, 1, 3, 2))
    c_rows = lambda t: block_rows(t.reshape(OCTETS, OCT_GROUPS, GROUP_CH, SSM_STATE))
    bmat = jnp.stack([b_rows(bb_re), b_rows(bb_im)], axis=1)
    cmat = jnp.stack([c_rows(c_re), c_rows(c_im)], axis=1)

    kr = jnp.arange(SSM_L - 1, -1, -1, dtype=F32)[:, None, None]
    mag_r = jnp.exp(kr * zr)
    per_oct = lambda t: t.reshape(t.shape[:-2] + (OCTETS, OCT_STATE))
    pw_b = per_oct(jnp.stack([mag_r * jnp.cos(kr * zi), mag_r * jnp.sin(kr * zi)]))
    pw_c = per_oct(jnp.stack([pr[1:], pi[1:]]))
    a_chunk = per_oct(jnp.stack([pr[SSM_L], pi[SSM_L]]))
    return (imp.astype(BF16), bmat, cmat, pw_b.transpose(2, 0, 1, 3), pw_c.transpose(2, 0, 1, 3),
            a_chunk.transpose(1, 0, 2))


def _scaled_blocks(mat_ref, pw_ref, out_ref, im_sign):
    def block(s, carry):
        m_re, m_im = mat_ref[0], mat_ref[1]
        p_re, p_im = pw_ref[0, pl.ds(s, 1), :], pw_ref[1, pl.ds(s, 1), :]
        rows = pl.ds(pl.multiple_of(s * LANES, LANES), LANES)
        out_ref[rows, 0:OCT_STATE] = (m_re * p_re - m_im * p_im).astype(BF16)
        out_ref[rows, OCT_STATE:2 * OCT_STATE] = (im_sign * (m_re * p_im + m_im * p_re)).astype(BF16)
        return carry

    lax.fori_loop(0, SSM_L, block, 0)


def _s5_state_kernel(u_ref, bmat_ref, pw_ref, a_ref, xs_ref, bpow_scr, s_scr, x_scr):
    _scaled_blocks(bmat_ref, pw_ref, bpow_scr, 1.0)
    for r in range(0, SSM_ROWS, SSM_RT):
        s_loc = _dot(u_ref[r:r + SSM_RT, :], bpow_scr[...])
        for k in range(STATE_COLS):
            s_scr[k, r:r + SSM_RT, :] = s_loc[:, k * LANES:(k + 1) * LANES]
    a = a_ref[...]
    col = lambda r, k: jnp.broadcast_to(a[r:r + 1, k * LANES:(k + 1) * LANES], (BATCH, LANES))
    ar = [col(0, k) for k in range(RE_COLS)]
    ai = [col(1, k) for k in range(RE_COLS)]

    def step(c, carry):
        rows = pl.ds(c, BATCH, stride=SSM_NC)
        nxt_re, nxt_im = [], []
        for k in range(RE_COLS):
            xr, xi = carry[k], carry[RE_COLS + k]
            x_scr[k, rows, :] = xr
            x_scr[RE_COLS + k, rows, :] = xi
            nxt_re.append(ar[k] * xr - ai[k] * xi + s_scr[k, rows, :])
            nxt_im.append(ar[k] * xi + ai[k] * xr + s_scr[RE_COLS + k, rows, :])
        return tuple(nxt_re + nxt_im)

    zeros = jnp.zeros((BATCH, LANES), F32)
    lax.fori_loop(0, SSM_NC, step, (zeros,) * STATE_COLS, unroll=4)
    for k in range(STATE_COLS):
        xs_ref[:, k * LANES:(k + 1) * LANES] = x_scr[k].astype(BF16)


def _s5_states(u_oct, bmat, pw_b, a_chunk):
    per_oct = lambda *tail: pl.BlockSpec((None,) + tail, lambda j: (j,) + (0,) * len(tail))
    return pl.pallas_call(
        _s5_state_kernel,
        grid=(OCTETS,),
        in_specs=[per_oct(SSM_ROWS, OCT_W), per_oct(2, LANES, OCT_STATE),
                  per_oct(2, SSM_L, OCT_STATE), per_oct(2, OCT_STATE)],
        out_specs=per_oct(SSM_ROWS, 2 * OCT_STATE),
        out_shape=jax.ShapeDtypeStruct((OCTETS, SSM_ROWS, 2 * OCT_STATE), BF16),
        scratch_shapes=[pltpu.VMEM((OCT_W, 2 * OCT_STATE), BF16),
                        pltpu.VMEM((STATE_COLS, SSM_ROWS, LANES), F32),
                        pltpu.VMEM((STATE_COLS, SSM_ROWS, LANES), F32)],
        compiler_params=_params(),
        name="s5_states",
    )(u_oct, bmat, pw_b, a_chunk)


def _s5_out_kernel(u_ref, xs_ref, imp_ref, cmat_ref, pw_ref, y_ref, toep_scr, cpow_scr):
    @pl.when(pl.program_id(1) == 0)
    def _():
        toep_scr[...] = jnp.zeros(toep_scr.shape, BF16)
        for s in range(SSM_L):
            toep_scr[s * LANES:(s + 1) * LANES, s * LANES:] = imp_ref[:, :OCT_W - s * LANES]
        _scaled_blocks(cmat_ref, pw_ref, cpow_scr, -1.0)

    y = _dot(u_ref[...], toep_scr[...]) + lax.dot_general(
        xs_ref[...], cpow_scr[...], _NT, preferred_element_type=F32)
    y_ref[...] = y.astype(BF16)


def _s5_output(u_oct, xs, imp, cmat, pw_c):
    rows = lambda w: pl.BlockSpec((None, SSM_RT, w), lambda j, r: (j, r, 0))
    per_oct = lambda *tail: pl.BlockSpec((None,) + tail, lambda j, r: (j,) + (0,) * len(tail))
    return pl.pallas_call(
        _s5_out_kernel,
        grid=(OCTETS, SSM_ROWS // SSM_RT),
        in_specs=[rows(OCT_W), rows(2 * OCT_STATE), per_oct(LANES, OCT_W),
                  per_oct(2, LANES, OCT_STATE), per_oct(2, SSM_L, OCT_STATE)],
        out_specs=rows(OCT_W),
        out_shape=jax.ShapeDtypeStruct((OCTETS, SSM_ROWS, OCT_W), BF16),
        scratch_shapes=[pltpu.VMEM((OCT_W, OCT_W), BF16), pltpu.VMEM((OCT_W, 2 * OCT_STATE), BF16)],
        compiler_params=_params(2),
        name="s5_output",
    )(u_oct, xs, imp, cmat, pw_c)


def _gelu_tanh(x):
    return 0.5 * x * (1.0 + jnp.tanh(math.sqrt(2.0 / math.pi) * (x + 0.044715 * (x * x * x))))


def _tail(x, mixed, g_ref, wi_ref, wo_ref, p_ref, wg_ref, wp_ref, o_ref):
    x = x + _rms(mixed, g_ref[3:4, :])
    x = _ffn_residual(x, g_ref[4:5, :], g_ref[5:6, :], wi_ref, wo_ref)
    gate = _sigmoid(_dot(_rms(x, g_ref[6:7, :]).astype(BF16), wg_ref[...]))
    emb = _dot(p_ref[...].astype(BF16), wp_ref[...])
    o_ref[...] = x + _rms(gate * emb, g_ref[7:8, :])


def _tail_attn_kernel(a_ref, x_ref, wm_ref, g_ref, wi_ref, wo_ref, p_ref, wg_ref, wp_ref, o_ref):
    mixed = _dot(a_ref[...], wm_ref[...])
    _tail(x_ref[...], mixed, g_ref, wi_ref, wo_ref, p_ref, wg_ref, wp_ref, o_ref)


def _tail_ssm_kernel(y_ref, x_ref, wm_ref, bm_ref, g_ref, wi_ref, wo_ref, p_ref, wg_ref, wp_ref,
                     o_ref, y_scr):
    y = _from_super_rows(y_ref, y_scr)
    z = _dot(_gelu_tanh(y).astype(BF16), wm_ref[...]) + bm_ref[...]
    mixed = z[:, :D_MODEL] * _sigmoid(z[:, D_MODEL:])
    _tail(x_ref[...], mixed, g_ref, wi_ref, wo_ref, p_ref, wg_ref, wp_ref, o_ref)


def _layer_tail(mix, x, p, layer, g, w_mix, b_mix, w_in, w_out, w_gate, w_proj):
    ple_specs = [pl.BlockSpec((None, TM, PLE_DIM), lambda b, c: (layer, b * TILES_PER_SEQ + c, 0)),
                 _const_spec((D_MODEL, D_MODEL)), _const_spec((PLE_DIM, D_MODEL))]
    common = dict(grid=_TOKEN_GRID, out_specs=_TOKEN_TILE,
                  out_shape=jax.ShapeDtypeStruct((TOKENS, D_MODEL), F32),
                  compiler_params=_params(2))
    if b_mix is None:
        return pl.pallas_call(
            _tail_attn_kernel,
            in_specs=[_TOKEN_TILE, _TOKEN_TILE, _const_spec((D_MODEL, D_MODEL))] + _FFN_SPECS
            + ple_specs,
            name="attn_out_ffn_ple", **common,
        )(mix, x, w_mix, g, w_in, w_out, p, w_gate, w_proj)
    return pl.pallas_call(
        _tail_ssm_kernel,
        in_specs=[_SUPER_TILE, _TOKEN_TILE, _const_spec((D_MODEL, 2 * D_MODEL)),
                  _const_spec((1, 2 * D_MODEL))] + _FFN_SPECS + ple_specs,
        scratch_shapes=[pltpu.VMEM((OCTETS, TM, LANES), F32)],
        name="s5_glu_ffn_ple", **common,
    )(mix, x, w_mix, b_mix, g, w_in, w_out, p, w_gate, w_proj)


def kernel(x, p, norm_g, ffn_w_in, ffn_w_out, attn_w_qkv, attn_w_o, attn_lam, attn_subln_g,
           rel_bias, ssm_lam_re, ssm_lam_im, ssm_log_dt, ssm_b_re, ssm_b_im, ssm_c_re, ssm_c_im,
           ssm_d, ssm_w_glu, ssm_b_glu, ple_w_proj, ple_w_gate):
    x = x.reshape(TOKENS, D_MODEL)
    p = p.reshape(DEPTH, TOKENS, PLE_DIM)
    bias = _bias_tiles(rel_bias)
    bf = lambda w: w.astype(BF16)
    for i in range(DEPTH):
        g = norm_g[i]
        j = i // N_MIXERS
        if i % N_MIXERS == 0:
            lambda_init = 0.8 - 0.6 * math.exp(-0.3 * i)
            w = bf(attn_w_qkv[j])
            x, qt, k, vt = _layer_head(x, g, bf(ffn_w_in[i, 0]), bf(ffn_w_out[i, 0]),
                                       (w[:, :D_MODEL].T, w[:, D_MODEL:2 * D_MODEL],
                                        w[:, 2 * D_MODEL:].T))
            mix = _attention(qt, k, vt, bias, attn_lam[j], attn_subln_g[j].reshape(1, -1),
                             lambda_init)
            w_mix, b_mix = bf(attn_w_o[j]), None
        else:
            imp, bmat, cmat, pw_b, pw_c, a_chunk = _ssm_operators(
                ssm_lam_re[j], ssm_lam_im[j], ssm_log_dt[j], ssm_b_re[j], ssm_b_im[j],
                ssm_c_re[j], ssm_c_im[j], ssm_d[j])
            x, u_oct = _layer_head(x, g, bf(ffn_w_in[i, 0]), bf(ffn_w_out[i, 0]))
            xs = _s5_states(u_oct, bmat, pw_b, a_chunk)
            mix = _s5_output(u_oct, xs, imp, cmat, pw_c)
            w_mix, b_mix = bf(ssm_w_glu[j]), ssm_b_glu[j].reshape(1, -1)
        x = _layer_tail(mix, x, p, i, g, w_mix, b_mix, bf(ffn_w_in[i, 1]), bf(ffn_w_out[i, 1]),
                        bf(ple_w_gate[i]), bf(ple_w_proj[i]))
    return x.reshape(BATCH, SEQ, D_MODEL)
```

```python
import functools
import math

import numpy as np
import jax
import jax.numpy as jnp
from jax import lax
from jax.experimental import pallas as pl
from jax.experimental.pallas import tpu as pltpu

D_MODEL = 1024
BATCH = 8
SEQ = 4096
DEPTH = 4
N_MIXERS = 2
HEAD_DIM = 64
N_HEADS = D_MODEL // (2 * HEAD_DIM)
REL_BUCKETS = 32
REL_MAX_DIST = 128
GROUP_CH = 16
GROUPS = D_MODEL // GROUP_CH
SSM_STATE = 64
D_FF = 2816
FFN_RESIDUAL = 0.5
PLE_DIM = 256
N_NORMS = 8
RMS_EPS = 1e-6
NEG_INF = -1e30

TOKENS = BATCH * SEQ
F32 = jnp.float32
BF16 = jnp.bfloat16
LANES = 128

V7X_VMEM_BYTES = 64 * 1024 * 1024
VMEM_LIMIT = V7X_VMEM_BYTES - 8 * 1024 * 1024

TM = 512
TILES_PER_SEQ = SEQ // TM
MXU_TILE = 256
FF_SPLITS = (0, 6 * MXU_TILE, D_FF)
assert D_FF % MXU_TILE == 0
TQ = 512
TK = 512
BIAS_TILES = 3
TQ_HALF = TQ // 2
SUM_ROWS = 16
HEADS_PER_STEP = 4
EARLY = 2
LATE = 2
LOG2E = math.log2(math.e)

SSM_L = 16
SSM_NC = SEQ // SSM_L
OCTETS = D_MODEL // LANES
OCT_GROUPS = LANES // GROUP_CH
OCT_W = SSM_L * LANES
OCT_STATE = OCT_GROUPS * SSM_STATE
SSM_ROWS = BATCH * SSM_NC
SSM_RT = 512
CH_PER_TILE = TM // SSM_L
RE_COLS = OCT_STATE // LANES
STATE_COLS = 2 * RE_COLS


def _t5_thresholds():
    n = np.arange(0, 4 * REL_MAX_DIST)
    max_exact = REL_BUCKETS // 2
    nf = np.maximum(n, 1).astype(np.float64)
    large = max_exact + (np.log(nf / max_exact) / math.log(REL_MAX_DIST / max_exact)
                         * (REL_BUCKETS - max_exact)).astype(np.int32)
    bucket = np.where(n < max_exact, n, np.minimum(large, REL_BUCKETS - 1))
    return [int(np.argmax(bucket >= j)) for j in range(1, REL_BUCKETS)]


T5_THRESHOLDS = _t5_thresholds()
assert TQ == TK and T5_THRESHOLDS[-1] <= TK, "key blocks before i-1 sit in the last bucket"

_NT = (((1,), (1,)), ((), ()))


def _const_spec(shape):
    nd = len(shape)
    return pl.BlockSpec(shape, lambda *_: (0,) * nd, pipeline_mode=pl.Buffered(1))


def _params(n_axes=1):
    return pltpu.CompilerParams(dimension_semantics=("arbitrary",) * n_axes,
                                vmem_limit_bytes=VMEM_LIMIT)


def _rms(x, g):
    return x * lax.rsqrt(jnp.mean(x * x, axis=-1, keepdims=True) + RMS_EPS) * g


def _sigmoid(x):
    return 1.0 / (1.0 + jnp.exp(-x))


def _dot(a, b):
    return jnp.dot(a, b, preferred_element_type=F32)


_TOKEN_TILE = pl.BlockSpec((TM, D_MODEL), lambda b, c: (b * TILES_PER_SEQ + c, 0))
_FEATURE_TILE = pl.BlockSpec((D_MODEL, TM), lambda b, c: (0, b * TILES_PER_SEQ + c))
_SUPER_TILE = pl.BlockSpec((OCTETS, CH_PER_TILE, OCT_W),
                           lambda b, c: (0, b * TILES_PER_SEQ + c, 0))
_TOKEN_GRID = (BATCH, TILES_PER_SEQ)
_FFN_SPECS = [_const_spec((N_NORMS, D_MODEL)), _const_spec((D_MODEL, 2 * D_FF)),
              _const_spec((D_FF, D_MODEL))]


def _ffn_residual(x, g_pre, g_post, wi_ref, wo_ref):
    h = _rms(x, g_pre).astype(BF16)
    y = jnp.zeros((TM, D_MODEL), F32)
    for lo, hi in zip(FF_SPLITS[:-1], FF_SPLITS[1:]):
        a = _dot(h, wi_ref[:, lo:hi])
        u = _dot(h, wi_ref[:, D_FF + lo:D_FF + hi])
        act = (a * _sigmoid(a) * u).astype(BF16)
        y = y + _dot(act, wo_ref[lo:hi, :])
    return x + FFN_RESIDUAL * _rms(y, g_post)


def _to_super_rows(h, o_ref, h_scr):
    for j in range(OCTETS):
        h_scr[j] = h[:, j * LANES:(j + 1) * LANES]
    for j in range(OCTETS):
        for s in range(SSM_L):
            rows = h_scr[j, pl.ds(s, CH_PER_TILE, stride=SSM_L), :]
            o_ref[j, :, s * LANES:(s + 1) * LANES] = rows.astype(BF16)


def _from_super_rows(y_ref, y_scr):
    for j in range(OCTETS):
        for s in range(SSM_L):
            y_scr[j, pl.ds(s, CH_PER_TILE, stride=SSM_L), :] = (
                y_ref[j, :, s * LANES:(s + 1) * LANES].astype(F32))
    return jnp.concatenate([y_scr[j] for j in range(OCTETS)], axis=1)


def _head_attn_kernel(x_ref, g_ref, wi_ref, wo_ref, wqt_ref, wk_ref, wvt_ref,
                      xo_ref, qt_ref, k_ref, vt_ref):
    x = _ffn_residual(x_ref[...], g_ref[0:1, :], g_ref[1:2, :], wi_ref, wo_ref)
    xo_ref[...] = x
    h = _rms(x, g_ref[2:3, :]).astype(BF16)
    qt = lax.dot_general(wqt_ref[...], h, _NT, preferred_element_type=F32)
    qt_ref[...] = (qt * (HEAD_DIM ** -0.5 * LOG2E)).astype(BF16)
    k_ref[...] = _dot(h, wk_ref[...]).astype(BF16)
    vt_ref[...] = lax.dot_general(wvt_ref[...], h, _NT, preferred_element_type=F32).astype(BF16)


def _head_ssm_kernel(x_ref, g_ref, wi_ref, wo_ref, xo_ref, u_ref, h_scr):
    x = _ffn_residual(x_ref[...], g_ref[0:1, :], g_ref[1:2, :], wi_ref, wo_ref)
    xo_ref[...] = x
    _to_super_rows(_rms(x, g_ref[2:3, :]), u_ref, h_scr)


def _layer_head(x, g, w_in, w_out, qkv=None):
    x_out = jax.ShapeDtypeStruct((TOKENS, D_MODEL), F32)
    if qkv is None:
        return pl.pallas_call(
            _head_ssm_kernel,
            grid=_TOKEN_GRID,
            in_specs=[_TOKEN_TILE] + _FFN_SPECS,
            out_specs=(_TOKEN_TILE, _SUPER_TILE),
            out_shape=(x_out, jax.ShapeDtypeStruct((OCTETS, SSM_ROWS, OCT_W), BF16)),
            scratch_shapes=[pltpu.VMEM((OCTETS, TM, LANES), F32)],
            compiler_params=_params(2),
            name="ffn_s5in",
        )(x, g, w_in, w_out)
    feat = jax.ShapeDtypeStruct((D_MODEL, TOKENS), BF16)
    return pl.pallas_call(
        _head_attn_kernel,
        grid=_TOKEN_GRID,
        in_specs=[_TOKEN_TILE] + _FFN_SPECS + [_const_spec((D_MODEL, D_MODEL))] * 3,
        out_specs=(_TOKEN_TILE, _FEATURE_TILE, _TOKEN_TILE, _FEATURE_TILE),
        out_shape=(x_out, feat, jax.ShapeDtypeStruct((TOKENS, D_MODEL), BF16), feat),
        compiler_params=_params(2),
        name="ffn_qkv",
    )(x, g, w_in, w_out, *qkv)


def _bias_kernel(rel_ref, o_ref):
    c = pl.program_id(0)
    r = pl.program_id(1)
    @pl.when(r == BIAS_TILES - 1)
    def _():
        o_ref[...] = jnp.zeros((TK, TQ), BF16)

    @pl.when(r < BIAS_TILES - 1)
    def _():
        ki = lax.broadcasted_iota(jnp.int32, (TK, TQ), 0)
        qi = lax.broadcasted_iota(jnp.int32, (TK, TQ), 1)
        d = qi - ki + r * TK
        val = jnp.full((TK, TQ), rel_ref[0, c], F32)
        for j, thr in enumerate(T5_THRESHOLDS, start=1):
            val = jnp.where(d >= thr, rel_ref[j, c], val)
        val = (val - rel_ref[REL_BUCKETS - 1, c]) * LOG2E
        o_ref[...] = jnp.where(d >= 0, val, NEG_INF).astype(BF16)


def _bias_tiles(rel_bias):
    tiles = pl.pallas_call(
        _bias_kernel,
        grid=(2 * N_HEADS, BIAS_TILES),
        in_specs=[pl.BlockSpec(memory_space=pltpu.SMEM)],
        out_specs=pl.BlockSpec((None, None, TK, TQ), lambda c, r: (c, r, 0, 0)),
        out_shape=jax.ShapeDtypeStruct((2 * N_HEADS, BIAS_TILES, TK, TQ), BF16),
        compiler_params=_params(2),
        name="t5_bias",
    )(rel_bias)
    return tiles.reshape(N_HEADS, 2, BIAS_TILES, TK, TQ)


def _attn_kernel(qt_ref, k_ref, vt_ref, bias_ref, lam_ref, sg_ref, o_ref, m_ref, acc_ref,
                 s_buf, p_buf, a_buf, *, lambda_init):
    i = pl.program_id(2)
    d = 2 * HEAD_DIM
    chains = [(hd, mi, slice(hf * TQ_HALF, (hf + 1) * TQ_HALF))
              for hd in range(HEADS_PER_STEP) for hf in range(2) for mi in range(2)]
    n = len(chains)

    qt = qt_ref[...]
    row = lax.broadcasted_iota(jnp.int32, (d, TQ), 0)
    zero = jnp.zeros((d, TQ), BF16)
    q_maps = [[jnp.where(row < HEAD_DIM, qt[hd * d:(hd + 1) * d], zero),
               jnp.where(row >= HEAD_DIM, qt[hd * d:(hd + 1) * d], zero)]
              for hd in range(HEADS_PER_STEP)]

    m_ref[...] = jnp.full(m_ref.shape, NEG_INF, F32)
    acc_ref[...] = jnp.zeros(acc_ref.shape, F32)
    sum_rows = jnp.where(lax.broadcasted_iota(jnp.int32, (SUM_ROWS, TK), 0) == 0, 1.0, 0.0).astype(BF16)

    def keys(j):
        return k_ref[pl.ds(pl.multiple_of(j * TK, TK), TK), :]

    def vals(j):
        vb = vt_ref[:, pl.ds(pl.multiple_of(j * TK, TK), TK)]
        return [jnp.concatenate([vb[hd * d:(hd + 1) * d], sum_rows], axis=0)
                for hd in range(HEADS_PER_STEP)]

    def scores(kb, j, chain):
        hd, mi, cols = chain
        tile = jnp.clip(i - j, 0, BIAS_TILES - 1)
        s = _dot(kb[:, hd * d:(hd + 1) * d], q_maps[hd][mi][:, cols])
        return s.astype(BF16) + bias_ref[hd, mi, tile, :, cols]

    def softmax(chain, s):
        hd, mi, cols = chain
        m_prev = m_ref[hd, mi, :, cols]
        m_new = jnp.maximum(m_prev, jnp.max(s, axis=0, keepdims=True).astype(F32))
        alpha = jnp.exp2(m_prev - m_new)
        p = jnp.exp2(s - m_new.astype(BF16))
        m_ref[hd, mi, :, cols] = m_new
        return p, alpha

    def values(vb, chain, p, alpha):
        hd, mi, cols = chain
        acc_ref[hd, mi, :, cols] = alpha * acc_ref[hd, mi, :, cols] + _dot(vb[hd], p)

    kb0 = keys(0)
    for e in range(EARLY):
        s_buf[e] = scores(kb0, 0, chains[e])
    for t in range(LATE):
        p_buf[t] = jnp.zeros((TK, TQ_HALF), BF16)
        a_buf[t] = jnp.ones((1, TQ_HALF), F32)

    def trip(j, carry):
        kb, vb = keys(j), vals(j)
        j_next = jnp.minimum(j + 1, i)
        kb_next, vb_prev = keys(j_next), vals(jnp.maximum(j - 1, 0))
        s_tiles, p_tiles = {}, {}
        for slot in range(n):
            ahead = slot + EARLY
            if ahead < n:
                s_tiles[ahead] = scores(kb, j, chains[ahead])
            else:
                s_next = scores(kb_next, j_next, chains[ahead - n])
            s = s_buf[slot] if slot < EARLY else s_tiles.pop(slot)
            p_tiles[slot] = softmax(chains[slot], s)
            if ahead >= n:
                s_buf[ahead - n] = s_next
            behind = slot - LATE
            if behind >= 0:
                values(vb, chains[behind], *p_tiles.pop(behind))
            else:
                values(vb_prev, chains[n + behind], p_buf[slot], a_buf[slot])
            if slot >= n - LATE:
                p_buf[slot - (n - LATE)], a_buf[slot - (n - LATE)] = p_tiles.pop(slot)
        return carry

    lax.fori_loop(0, i + 1, trip, 0)
    vb_last = vals(i)
    for t in range(LATE):
        values(vb_last, chains[n - LATE + t], p_buf[t], a_buf[t])

    lv = lam_ref[...]
    lam = (jnp.exp(jnp.sum(lv[0:1] * lv[1:2], keepdims=True))
           - jnp.exp(jnp.sum(lv[2:3] * lv[3:4], keepdims=True)) + lambda_init)
    for hd in range(HEADS_PER_STEP):
        ot = (acc_ref[hd, 0, 0:d, :] * (1.0 / acc_ref[hd, 0, d:d + 1, :])
              - lam * (acc_ref[hd, 1, 0:d, :] * (1.0 / acc_ref[hd, 1, d:d + 1, :])))
        o = _rms(ot.T, sg_ref[...]) * (1.0 - lambda_init)
        o_ref[:, hd * d:(hd + 1) * d] = o.astype(BF16)


def _attention(qt, k, vt, bias, lam_vecs, subln_g, lambda_init):
    nq = SEQ // TQ
    wide = HEADS_PER_STEP * 2 * HEAD_DIM
    return pl.pallas_call(
        functools.partial(_attn_kernel, lambda_init=lambda_init),
        grid=(BATCH, N_HEADS // HEADS_PER_STEP, nq),
        in_specs=[pl.BlockSpec((wide, TQ), lambda b, h, i: (h, b * nq + i)),
                  pl.BlockSpec((SEQ, wide), lambda b, h, i: (b, h)),
                  pl.BlockSpec((wide, SEQ), lambda b, h, i: (h, b)),
                  pl.BlockSpec((HEADS_PER_STEP, 2, BIAS_TILES, TK, TQ),
                               lambda b, h, i: (h, 0, 0, 0, 0)),
                  pl.BlockSpec((4, HEAD_DIM), lambda b, h, i: (0, 0)),
                  pl.BlockSpec((1, 2 * HEAD_DIM), lambda b, h, i: (0, 0))],
        out_specs=pl.BlockSpec((TQ, wide), lambda b, h, i: (b * nq + i, h)),
        out_shape=jax.ShapeDtypeStruct((TOKENS, D_MODEL), BF16),
        scratch_shapes=[pltpu.VMEM((HEADS_PER_STEP, 2, 1, TQ), F32),
                        pltpu.VMEM((HEADS_PER_STEP, 2, 2 * HEAD_DIM + SUM_ROWS, TQ), F32),
                        pltpu.VMEM((EARLY, TK, TQ_HALF), BF16), pltpu.VMEM((LATE, TK, TQ_HALF), BF16),
                        pltpu.VMEM((LATE, 1, TQ_HALF), F32)],
        compiler_params=_params(3),
        name="diff_attn",
    )(qt, k, vt, bias, lam_vecs, subln_g)


def _ssm_operators(lam_re, lam_im, log_dt, b_re, b_im, c_re, c_im, d_skip):
    lam_re, lam_im, log_dt, b_re, b_im, c_re, c_im, d_skip = lax.optimization_barrier(
        (lam_re, lam_im, log_dt, b_re, b_im, c_re, c_im, d_skip))
    dt = jnp.exp(log_dt)[:, None]
    zr, zi = lam_re * dt, lam_im * dt
    ks = jnp.arange(SSM_L + 1, dtype=F32)[:, None, None]
    mag = jnp.exp(ks * zr)
    pr, pi = mag * jnp.cos(ks * zi), mag * jnp.sin(ks * zi)
    nr = jnp.expm1(zr) * jnp.cos(zi) - 2.0 * jnp.sin(0.5 * zi) ** 2
    ni = pi[1]
    den = lam_re * lam_re + lam_im * lam_im
    fr, fi = (nr * lam_re + ni * lam_im) / den, (ni * lam_re - nr * lam_im) / den
    bb_re = fr[..., None] * b_re - fi[..., None] * b_im
    bb_im = fr[..., None] * b_im + fi[..., None] * b_re
    same = jnp.eye(OCT_GROUPS, dtype=F32)

    def block_rows(t):
        t = lax.optimization_barrier(t)
        return (t[:, :, :, None, :] * same[None, :, None, :, None]).reshape(OCTETS, LANES, OCT_STATE)

    b_rows = lambda t: block_rows(t.reshape(OCTETS, OCT_GROUPS, SSM_STATE, GROUP_CH)
                                  .transpose(0, 1, 3, 2))
    c_rows = lambda t: block_rows(t.reshape(OCTETS, OCT_GROUPS, GROUP_CH, SSM_STATE))
    bmat = jnp.stack([b_rows(bb_re), b_rows(bb_im)], axis=1)
    cmat = jnp.stack([c_rows(c_re), c_rows(c_im)], axis=1)

    kr = jnp.arange(SSM_L - 1, -1, -1, dtype=F32)[:, None, None]
    mag_r = jnp.exp(kr * zr)
    per_oct = lambda t: t.reshape(t.shape[:-2] + (OCTETS, OCT_STATE))
    pw_b = per_oct(jnp.stack([mag_r * jnp.cos(kr * zi), mag_r * jnp.sin(kr * zi)]))
    pw_c = per_oct(jnp.stack([pr[1:], pi[1:]]))
    a_chunk = per_oct(jnp.stack([pr[SSM_L], pi[SSM_L]]))
    return (bmat, cmat, pw_b.transpose(2, 0, 1, 3), pw_c.transpose(2, 0, 1, 3),
            a_chunk.transpose(1, 0, 2), d_skip.reshape(OCTETS, 1, LANES))


def _scaled_blocks(mat_ref, pw_ref, out_ref, im_sign):
    def block(s, carry):
        m_re, m_im = mat_ref[0], mat_ref[1]
        p_re, p_im = pw_ref[0, pl.ds(s, 1), :], pw_ref[1, pl.ds(s, 1), :]
        rows = pl.ds(pl.multiple_of(s * LANES, LANES), LANES)
        out_ref[rows, 0:OCT_STATE] = (m_re * p_re - m_im * p_im).astype(BF16)
        out_ref[rows, OCT_STATE:2 * OCT_STATE] = (im_sign * (m_re * p_im + m_im * p_re)).astype(BF16)
        return carry

    lax.fori_loop(0, SSM_L, block, 0)


def _s5_state_kernel(u_ref, bmat_ref, pw_ref, a_ref, xs_ref, bpow_scr, s_scr, x_scr):
    _scaled_blocks(bmat_ref, pw_ref, bpow_scr, 1.0)
    for r in range(0, SSM_ROWS, SSM_RT):
        s_loc = _dot(u_ref[r:r + SSM_RT, :], bpow_scr[...])
        for k in range(STATE_COLS):
            s_scr[k, r:r + SSM_RT, :] = s_loc[:, k * LANES:(k + 1) * LANES]
    a = a_ref[...]
    col = lambda r, k: jnp.broadcast_to(a[r:r + 1, k * LANES:(k + 1) * LANES], (BATCH, LANES))
    ar = [col(0, k) for k in range(RE_COLS)]
    ai = [col(1, k) for k in range(RE_COLS)]

    def step(c, carry):
        rows = pl.ds(c, BATCH, stride=SSM_NC)
        nxt_re, nxt_im = [], []
        for k in range(RE_COLS):
            xr, xi = carry[k], carry[RE_COLS + k]
            x_scr[k, rows, :] = xr
            x_scr[RE_COLS + k, rows, :] = xi
            nxt_re.append(ar[k] * xr - ai[k] * xi + s_scr[k, rows, :])
            nxt_im.append(ar[k] * xi + ai[k] * xr + s_scr[RE_COLS + k, rows, :])
        return tuple(nxt_re + nxt_im)

    zeros = jnp.zeros((BATCH, LANES), F32)
    lax.fori_loop(0, SSM_NC, step, (zeros,) * STATE_COLS, unroll=4)
    for k in range(STATE_COLS):
        xs_ref[:, k * LANES:(k + 1) * LANES] = x_scr[k].astype(BF16)


def _s5_states(u_oct, bmat, pw_b, a_chunk):
    per_oct = lambda *tail: pl.BlockSpec((None,) + tail, lambda j: (j,) + (0,) * len(tail))
    return pl.pallas_call(
        _s5_state_kernel,
        grid=(OCTETS,),
        in_specs=[per_oct(SSM_ROWS, OCT_W), per_oct(2, LANES, OCT_STATE),
                  per_oct(2, SSM_L, OCT_STATE), per_oct(2, OCT_STATE)],
        out_specs=per_oct(SSM_ROWS, 2 * OCT_STATE),
        out_shape=jax.ShapeDtypeStruct((OCTETS, SSM_ROWS, 2 * OCT_STATE), BF16),
        scratch_shapes=[pltpu.VMEM((OCT_W, 2 * OCT_STATE), BF16),
                        pltpu.VMEM((STATE_COLS, SSM_ROWS, LANES), F32),
                        pltpu.VMEM((STATE_COLS, SSM_ROWS, LANES), F32)],
        compiler_params=_params(),
        name="s5_states",
    )(u_oct, bmat, pw_b, a_chunk)


def _s5_out_kernel(u_ref, xs_ref, bmat_ref, cmat_ref, pw_ref, d_ref, y_ref, toep_scr, cpow_scr):
    @pl.when(pl.program_id(1) == 0)
    def _():
        _scaled_blocks(cmat_ref, pw_ref, cpow_scr, -1.0)
        b_cat = jnp.concatenate([bmat_ref[0], bmat_ref[1]], axis=1).astype(BF16)
        c_tau0 = jnp.concatenate([cmat_ref[0], -cmat_ref[1]], axis=1).astype(BF16)
        c_all = jnp.concatenate([c_tau0, cpow_scr[0:OCT_W - LANES, :]], axis=0)
        imp = lax.dot_general(b_cat, c_all, _NT, preferred_element_type=F32)
        on_diag = (lax.broadcasted_iota(jnp.int32, (LANES, LANES), 0)
                   == lax.broadcasted_iota(jnp.int32, (LANES, LANES), 1))
        skip = jnp.where(on_diag, d_ref[...], 0.0)
        imp = jnp.concatenate([imp[:, 0:LANES] + skip, imp[:, LANES:]], axis=1).astype(BF16)
        toep_scr[...] = jnp.zeros(toep_scr.shape, BF16)
        for s in range(SSM_L):
            toep_scr[s * LANES:(s + 1) * LANES, s * LANES:] = imp[:, :OCT_W - s * LANES]

    u, xs = u_ref[...], xs_ref[...]
    for lo in range(0, OCT_W, MXU_TILE):
        hi = lo + MXU_TILE
        y = _dot(u[:, :hi], toep_scr[0:hi, lo:hi]) + lax.dot_general(
            xs, cpow_scr[lo:hi, :], _NT, preferred_element_type=F32)
        y_ref[:, lo:hi] = y.astype(BF16)


def _s5_output(u_oct, xs, bmat, cmat, pw_c, d_oct):
    rows = lambda w: pl.BlockSpec((None, SSM_RT, w), lambda j, r: (j, r, 0))
    per_oct = lambda *tail: pl.BlockSpec((None,) + tail, lambda j, r: (j,) + (0,) * len(tail))
    return pl.pallas_call(
        _s5_out_kernel,
        grid=(OCTETS, SSM_ROWS // SSM_RT),
        in_specs=[rows(OCT_W), rows(2 * OCT_STATE), per_oct(2, LANES, OCT_STATE),
                  per_oct(2, LANES, OCT_STATE), per_oct(2, SSM_L, OCT_STATE), per_oct(1, LANES)],
        out_specs=rows(OCT_W),
        out_shape=jax.ShapeDtypeStruct((OCTETS, SSM_ROWS, OCT_W), BF16),
        scratch_shapes=[pltpu.VMEM((OCT_W, OCT_W), BF16), pltpu.VMEM((OCT_W, 2 * OCT_STATE), BF16)],
        compiler_params=_params(2),
        name="s5_output",
    )(u_oct, xs, bmat, cmat, pw_c, d_oct)


def _gelu_tanh(x):
    return 0.5 * x * (1.0 + jnp.tanh(math.sqrt(2.0 / math.pi) * (x + 0.044715 * (x * x * x))))


def _tail(x, mixed, g_ref, wi_ref, wo_ref, p_ref, wg_ref, wp_ref, o_ref):
    x = x + _rms(mixed, g_ref[3:4, :])
    x = _ffn_residual(x, g_ref[4:5, :], g_ref[5:6, :], wi_ref, wo_ref)
    gate = _sigmoid(_dot(_rms(x, g_ref[6:7, :]).astype(BF16), wg_ref[...]))
    emb = _dot(p_ref[...].astype(BF16), wp_ref[...])
    o_ref[...] = x + _rms(gate * emb, g_ref[7:8, :])


def _tail_attn_kernel(a_ref, x_ref, wm_ref, g_ref, wi_ref, wo_ref, p_ref, wg_ref, wp_ref, o_ref):
    mixed = _dot(a_ref[...], wm_ref[...])
    _tail(x_ref[...], mixed, g_ref, wi_ref, wo_ref, p_ref, wg_ref, wp_ref, o_ref)


def _tail_ssm_kernel(y_ref, x_ref, wm_ref, bm_ref, g_ref, wi_ref, wo_ref, p_ref, wg_ref, wp_ref,
                     o_ref, y_scr):
    y = _from_super_rows(y_ref, y_scr)
    z = _dot(_gelu_tanh(y).astype(BF16), wm_ref[...]) + bm_ref[...]
    mixed = z[:, :D_MODEL] * _sigmoid(z[:, D_MODEL:])
    _tail(x_ref[...], mixed, g_ref, wi_ref, wo_ref, p_ref, wg_ref, wp_ref, o_ref)


def _layer_tail(mix, x, p, layer, g, w_mix, b_mix, w_in, w_out, w_gate, w_proj):
    ple_specs = [pl.BlockSpec((None, TM, PLE_DIM), lambda b, c: (layer, b * TILES_PER_SEQ + c, 0)),
                 _const_spec((D_MODEL, D_MODEL)), _const_spec((PLE_DIM, D_MODEL))]
    common = dict(grid=_TOKEN_GRID, out_specs=_TOKEN_TILE,
                  out_shape=jax.ShapeDtypeStruct((TOKENS, D_MODEL), F32),
                  compiler_params=_params(2))
    if b_mix is None:
        return pl.pallas_call(
            _tail_attn_kernel,
            in_specs=[_TOKEN_TILE, _TOKEN_TILE, _const_spec((D_MODEL, D_MODEL))] + _FFN_SPECS
            + ple_specs,
            name="attn_out_ffn_ple", **common,
        )(mix, x, w_mix, g, w_in, w_out, p, w_gate, w_proj)
    return pl.pallas_call(
        _tail_ssm_kernel,
        in_specs=[_SUPER_TILE, _TOKEN_TILE, _const_spec((D_MODEL, 2 * D_MODEL)),
                  _const_spec((1, 2 * D_MODEL))] + _FFN_SPECS + ple_specs,
        scratch_shapes=[pltpu.VMEM((OCTETS, TM, LANES), F32)],
        name="s5_glu_ffn_ple", **common,
    )(mix, x, w_mix, b_mix, g, w_in, w_out, p, w_gate, w_proj)


def kernel(x, p, norm_g, ffn_w_in, ffn_w_out, attn_w_qkv, attn_w_o, attn_lam, attn_subln_g,
           rel_bias, ssm_lam_re, ssm_lam_im, ssm_log_dt, ssm_b_re, ssm_b_im, ssm_c_re, ssm_c_im,
           ssm_d, ssm_w_glu, ssm_b_glu, ple_w_proj, ple_w_gate):
    x = x.reshape(TOKENS, D_MODEL)
    p = p.reshape(DEPTH, TOKENS, PLE_DIM)
    bias = _bias_tiles(rel_bias)
    bf = lambda w: w.astype(BF16)
    for i in range(DEPTH):
        g = norm_g[i]
        j = i // N_MIXERS
        if i % N_MIXERS == 0:
            lambda_init = 0.8 - 0.6 * math.exp(-0.3 * i)
            w = bf(attn_w_qkv[j])
            x, qt, k, vt = _layer_head(x, g, bf(ffn_w_in[i, 0]), bf(ffn_w_out[i, 0]),
                                       (w[:, :D_MODEL].T, w[:, D_MODEL:2 * D_MODEL],
                                        w[:, 2 * D_MODEL:].T))
            mix = _attention(qt, k, vt, bias, attn_lam[j], attn_subln_g[j].reshape(1, -1),
                             lambda_init)
            w_mix, b_mix = bf(attn_w_o[j]), None
        else:
            bmat, cmat, pw_b, pw_c, a_chunk, d_oct = _ssm_operators(
                ssm_lam_re[j], ssm_lam_im[j], ssm_log_dt[j], ssm_b_re[j], ssm_b_im[j],
                ssm_c_re[j], ssm_c_im[j], ssm_d[j])
            x, u_oct = _layer_head(x, g, bf(ffn_w_in[i, 0]), bf(ffn_w_out[i, 0]))
            xs = _s5_states(u_oct, bmat, pw_b, a_chunk)
            mix = _s5_output(u_oct, xs, bmat, cmat, pw_c, d_oct)
            w_mix, b_mix = bf(ssm_w_glu[j]), ssm_b_glu[j].reshape(1, -1)
        x = _layer_tail(mix, x, p, i, g, w_mix, b_mix, bf(ffn_w_in[i, 1]), bf(ffn_w_out[i, 1]),
                        bf(ple_w_gate[i]), bf(ple_w_proj[i]))
    return x.reshape(BATCH, SEQ, D_MODEL)
```

```python
import functools
import math

import numpy as np
import jax
import jax.numpy as jnp
from jax import lax
from jax.experimental import pallas as pl
from jax.experimental.pallas import tpu as pltpu

D_MODEL = 1024
BATCH = 8
SEQ = 4096
DEPTH = 4
N_MIXERS = 2
HEAD_DIM = 64
N_HEADS = D_MODEL // (2 * HEAD_DIM)
REL_BUCKETS = 32
REL_MAX_DIST = 128
GROUP_CH = 16
GROUPS = D_MODEL // GROUP_CH
SSM_STATE = 64
D_FF = 2816
FFN_RESIDUAL = 0.5
PLE_DIM = 256
N_NORMS = 8
RMS_EPS = 1e-6
NEG_INF = -1e30

TOKENS = BATCH * SEQ
F32 = jnp.float32
BF16 = jnp.bfloat16
LANES = 128

V7X_VMEM_BYTES = 64 * 1024 * 1024
VMEM_LIMIT = V7X_VMEM_BYTES - 8 * 1024 * 1024

TM = 512
TILES_PER_SEQ = SEQ // TM
SUB_TILES = 2
SUB_TM = TM // SUB_TILES
MXU_TILE = 256
FF_SPLITS = (0, 6 * MXU_TILE, D_FF)
assert D_FF % MXU_TILE == 0
TQ = 512
TK = 512
BIAS_TILES = 3
TQ_HALF = TQ // 2
SUM_ROWS = 16
HEADS_PER_STEP = 4
EARLY = 2
LATE = 2
LOG2E = math.log2(math.e)

SSM_L = 16
SSM_NC = SEQ // SSM_L
OCTETS = D_MODEL // LANES
OCT_GROUPS = LANES // GROUP_CH
OCT_W = SSM_L * LANES
OCT_STATE = OCT_GROUPS * SSM_STATE
SSM_ROWS = BATCH * SSM_NC
SSM_RT = 512
CH_PER_TILE = TM // SSM_L
SUB_CH = SUB_TM // SSM_L
RE_COLS = OCT_STATE // LANES
STATE_COLS = 2 * RE_COLS


def _t5_thresholds():
    n = np.arange(0, 4 * REL_MAX_DIST)
    max_exact = REL_BUCKETS // 2
    nf = np.maximum(n, 1).astype(np.float64)
    large = max_exact + (np.log(nf / max_exact) / math.log(REL_MAX_DIST / max_exact)
                         * (REL_BUCKETS - max_exact)).astype(np.int32)
    bucket = np.where(n < max_exact, n, np.minimum(large, REL_BUCKETS - 1))
    return [int(np.argmax(bucket >= j)) for j in range(1, REL_BUCKETS)]


T5_THRESHOLDS = _t5_thresholds()
assert TQ == TK and T5_THRESHOLDS[-1] <= TK, "key blocks before i-1 sit in the last bucket"

_NT = (((1,), (1,)), ((), ()))


def _const_spec(shape):
    nd = len(shape)
    return pl.BlockSpec(shape, lambda *_: (0,) * nd, pipeline_mode=pl.Buffered(1))


def _params(n_axes=1):
    return pltpu.CompilerParams(dimension_semantics=("arbitrary",) * n_axes,
                                vmem_limit_bytes=VMEM_LIMIT)


def _rms(x, g):
    return x * lax.rsqrt(jnp.mean(x * x, axis=-1, keepdims=True) + RMS_EPS) * g


def _sigmoid(x):
    return 1.0 / (1.0 + jnp.exp(-x))


def _dot(a, b):
    return jnp.dot(a, b, preferred_element_type=F32)


_TOKEN_TILE = pl.BlockSpec((TM, D_MODEL), lambda b, c: (b * TILES_PER_SEQ + c, 0))
_FEATURE_TILE = pl.BlockSpec((D_MODEL, TM), lambda b, c: (0, b * TILES_PER_SEQ + c))
_SUPER_TILE = pl.BlockSpec((OCTETS, CH_PER_TILE, OCT_W),
                           lambda b, c: (0, b * TILES_PER_SEQ + c, 0))
_TOKEN_GRID = (BATCH, TILES_PER_SEQ)
_FFN_SPECS = [_const_spec((N_NORMS, D_MODEL)), _const_spec((D_MODEL, 2 * D_FF)),
              _const_spec((D_FF, D_MODEL))]


def _skewed(stage_gens):
    waiting, running = list(stage_gens), []
    while waiting or running:
        if waiting:
            running.append(waiting.pop(0))
        for gen in list(running):
            if next(gen, _DONE) is _DONE:
                running.remove(gen)


_DONE = object()


def _sub_rows(r):
    return pl.ds(r * SUB_TM, SUB_TM)


def _ffn_stages(x, g_pre, g_post, wi_ref, wo_ref, out):
    h = _rms(x, g_pre).astype(BF16)
    yield
    y = None
    for lo, hi in zip(FF_SPLITS[:-1], FF_SPLITS[1:]):
        a = _dot(h, wi_ref[:, lo:hi])
        u = _dot(h, wi_ref[:, D_FF + lo:D_FF + hi])
        yield
        act = (a * _sigmoid(a) * u).astype(BF16)
        yield
        down = _dot(act, wo_ref[lo:hi, :])
        y = down if y is None else y + down
        yield
    out.append(x + FFN_RESIDUAL * _rms(y, g_post))


def _to_super_rows(h, r, o_ref, h_scr):
    chunks = slice(r * SUB_CH, (r + 1) * SUB_CH)
    for j in range(OCTETS):
        h_scr[j, _sub_rows(r), :] = h[:, j * LANES:(j + 1) * LANES]
    for j in range(OCTETS):
        for s in range(SSM_L):
            rows = h_scr[j, pl.ds(r * SUB_TM + s, SUB_CH, stride=SSM_L), :]
            o_ref[j, chunks, s * LANES:(s + 1) * LANES] = rows.astype(BF16)


def _from_super_rows(y_ref, r, y_scr):
    chunks = slice(r * SUB_CH, (r + 1) * SUB_CH)
    for j in range(OCTETS):
        for s in range(SSM_L):
            y_scr[j, pl.ds(r * SUB_TM + s, SUB_CH, stride=SSM_L), :] = (
                y_ref[j, chunks, s * LANES:(s + 1) * LANES].astype(F32))
    return jnp.concatenate([y_scr[j, _sub_rows(r), :] for j in range(OCTETS)], axis=1)


def _head_stages(r, x_ref, g_ref, wi_ref, wo_ref, xo_ref, out):
    res = []
    yield from _ffn_stages(x_ref[_sub_rows(r), :], g_ref[0:1, :], g_ref[1:2, :], wi_ref, wo_ref,
                           res)
    xo_ref[_sub_rows(r), :] = res[0]
    out.append(_rms(res[0], g_ref[2:3, :]))
    yield


def _head_attn_kernel(x_ref, g_ref, wi_ref, wo_ref, wqt_ref, wk_ref, wvt_ref,
                      xo_ref, qt_ref, k_ref, vt_ref):
    def stages(r):
        res = []
        yield from _head_stages(r, x_ref, g_ref, wi_ref, wo_ref, xo_ref, res)
        h = res[0].astype(BF16)
        cols = _sub_rows(r)
        qt = lax.dot_general(wqt_ref[...], h, _NT, preferred_element_type=F32)
        qt_ref[:, cols] = (qt * (HEAD_DIM ** -0.5 * LOG2E)).astype(BF16)
        k_ref[cols, :] = _dot(h, wk_ref[...]).astype(BF16)
        vt = lax.dot_general(wvt_ref[...], h, _NT, preferred_element_type=F32)
        vt_ref[:, cols] = vt.astype(BF16)

    _skewed(stages(r) for r in range(SUB_TILES))


def _head_ssm_kernel(x_ref, g_ref, wi_ref, wo_ref, xo_ref, u_ref, h_scr):
    def stages(r):
        res = []
        yield from _head_stages(r, x_ref, g_ref, wi_ref, wo_ref, xo_ref, res)
        _to_super_rows(res[0], r, u_ref, h_scr)

    _skewed(stages(r) for r in range(SUB_TILES))


def _layer_head(x, g, w_in, w_out, qkv=None):
    x_out = jax.ShapeDtypeStruct((TOKENS, D_MODEL), F32)
    if qkv is None:
        return pl.pallas_call(
            _head_ssm_kernel,
            grid=_TOKEN_GRID,
            in_specs=[_TOKEN_TILE] + _FFN_SPECS,
            out_specs=(_TOKEN_TILE, _SUPER_TILE),
            out_shape=(x_out, jax.ShapeDtypeStruct((OCTETS, SSM_ROWS, OCT_W), BF16)),
            scratch_shapes=[pltpu.VMEM((OCTETS, TM, LANES), F32)],
            compiler_params=_params(2),
            name="ffn_s5in",
        )(x, g, w_in, w_out)
    feat = jax.ShapeDtypeStruct((D_MODEL, TOKENS), BF16)
    return pl.pallas_call(
        _head_attn_kernel,
        grid=_TOKEN_GRID,
        in_specs=[_TOKEN_TILE] + _FFN_SPECS + [_const_spec((D_MODEL, D_MODEL))] * 3,
        out_specs=(_TOKEN_TILE, _FEATURE_TILE, _TOKEN_TILE, _FEATURE_TILE),
        out_shape=(x_out, feat, jax.ShapeDtypeStruct((TOKENS, D_MODEL), BF16), feat),
        compiler_params=_params(2),
        name="ffn_qkv",
    )(x, g, w_in, w_out, *qkv)


def _bias_kernel(rel_ref, o_ref):
    c = pl.program_id(0)
    r = pl.program_id(1)
    @pl.when(r == BIAS_TILES - 1)
    def _():
        o_ref[...] = jnp.zeros((TK, TQ), BF16)

    @pl.when(r < BIAS_TILES - 1)
    def _():
        ki = lax.broadcasted_iota(jnp.int32, (TK, TQ), 0)
        qi = lax.broadcasted_iota(jnp.int32, (TK, TQ), 1)
        d = qi - ki + r * TK
        val = jnp.full((TK, TQ), rel_ref[0, c], F32)
        for j, thr in enumerate(T5_THRESHOLDS, start=1):
            val = jnp.where(d >= thr, rel_ref[j, c], val)
        val = (val - rel_ref[REL_BUCKETS - 1, c]) * LOG2E
        o_ref[...] = jnp.where(d >= 0, val, NEG_INF).astype(BF16)


def _bias_tiles(rel_bias):
    tiles = pl.pallas_call(
        _bias_kernel,
        grid=(2 * N_HEADS, BIAS_TILES),
        in_specs=[pl.BlockSpec(memory_space=pltpu.SMEM)],
        out_specs=pl.BlockSpec((None, None, TK, TQ), lambda c, r: (c, r, 0, 0)),
        out_shape=jax.ShapeDtypeStruct((2 * N_HEADS, BIAS_TILES, TK, TQ), BF16),
        compiler_params=_params(2),
        name="t5_bias",
    )(rel_bias)
    return tiles.reshape(N_HEADS, 2, BIAS_TILES, TK, TQ)


def _attn_kernel(qt_ref, k_ref, vt_ref, bias_ref, lam_ref, sg_ref, o_ref, m_ref, acc_ref,
                 s_buf, p_buf, a_buf, *, lambda_init):
    i = pl.program_id(2)
    d = 2 * HEAD_DIM
    chains = [(hd, mi, slice(hf * TQ_HALF, (hf + 1) * TQ_HALF))
              for hd in range(HEADS_PER_STEP) for hf in range(2) for mi in range(2)]
    n = len(chains)

    qt = qt_ref[...]
    row = lax.broadcasted_iota(jnp.int32, (d, TQ), 0)
    zero = jnp.zeros((d, TQ), BF16)
    q_maps = [[jnp.where(row < HEAD_DIM, qt[hd * d:(hd + 1) * d], zero),
               jnp.where(row >= HEAD_DIM, qt[hd * d:(hd + 1) * d], zero)]
              for hd in range(HEADS_PER_STEP)]

    m_ref[...] = jnp.full(m_ref.shape, NEG_INF, F32)
    acc_ref[...] = jnp.zeros(acc_ref.shape, F32)
    sum_rows = jnp.where(lax.broadcasted_iota(jnp.int32, (SUM_ROWS, TK), 0) == 0, 1.0, 0.0).astype(BF16)

    def keys(j):
        return k_ref[pl.ds(pl.multiple_of(j * TK, TK), TK), :]

    def vals(j):
        vb = vt_ref[:, pl.ds(pl.multiple_of(j * TK, TK), TK)]
        return [jnp.concatenate([vb[hd * d:(hd + 1) * d], sum_rows], axis=0)
                for hd in range(HEADS_PER_STEP)]

    def scores(kb, j, chain):
        hd, mi, cols = chain
        tile = jnp.clip(i - j, 0, BIAS_TILES - 1)
        s = _dot(kb[:, hd * d:(hd + 1) * d], q_maps[hd][mi][:, cols])
        return s.astype(BF16) + bias_ref[hd, mi, tile, :, cols]

    def softmax(chain, s):
        hd, mi, cols = chain
        m_prev = m_ref[hd, mi, :, cols]
        m_new = jnp.maximum(m_prev, jnp.max(s, axis=0, keepdims=True).astype(F32))
        alpha = jnp.exp2(m_prev - m_new)
        p = jnp.exp2(s - m_new.astype(BF16))
        m_ref[hd, mi, :, cols] = m_new
        return p, alpha

    def values(vb, chain, p, alpha):
        hd, mi, cols = chain
        acc_ref[hd, mi, :, cols] = alpha * acc_ref[hd, mi, :, cols] + _dot(vb[hd], p)

    kb0 = keys(0)
    for e in range(EARLY):
        s_buf[e] = scores(kb0, 0, chains[e])
    for t in range(LATE):
        p_buf[t] = jnp.zeros((TK, TQ_HALF), BF16)
        a_buf[t] = jnp.ones((1, TQ_HALF), F32)

    def trip(j, carry):
        kb, vb = keys(j), vals(j)
        j_next = jnp.minimum(j + 1, i)
        kb_next, vb_prev = keys(j_next), vals(jnp.maximum(j - 1, 0))
        s_tiles, p_tiles = {}, {}
        for slot in range(n):
            ahead = slot + EARLY
            if ahead < n:
                s_tiles[ahead] = scores(kb, j, chains[ahead])
            else:
                s_next = scores(kb_next, j_next, chains[ahead - n])
            s = s_buf[slot] if slot < EARLY else s_tiles.pop(slot)
            p_tiles[slot] = softmax(chains[slot], s)
            if ahead >= n:
                s_buf[ahead - n] = s_next
            behind = slot - LATE
            if behind >= 0:
                values(vb, chains[behind], *p_tiles.pop(behind))
            else:
                values(vb_prev, chains[n + behind], p_buf[slot], a_buf[slot])
            if slot >= n - LATE:
                p_buf[slot - (n - LATE)], a_buf[slot - (n - LATE)] = p_tiles.pop(slot)
        return carry

    lax.fori_loop(0, i + 1, trip, 0)
    vb_last = vals(i)
    for t in range(LATE):
        values(vb_last, chains[n - LATE + t], p_buf[t], a_buf[t])

    lv = lam_ref[...]
    lam = (jnp.exp(jnp.sum(lv[0:1] * lv[1:2], keepdims=True))
           - jnp.exp(jnp.sum(lv[2:3] * lv[3:4], keepdims=True)) + lambda_init)
    for hd in range(HEADS_PER_STEP):
        ot = (acc_ref[hd, 0, 0:d, :] * (1.0 / acc_ref[hd, 0, d:d + 1, :])
              - lam * (acc_ref[hd, 1, 0:d, :] * (1.0 / acc_ref[hd, 1, d:d + 1, :])))
        o = _rms(ot.T, sg_ref[...]) * (1.0 - lambda_init)
        o_ref[:, hd * d:(hd + 1) * d] = o.astype(BF16)


def _attention(qt, k, vt, bias, lam_vecs, subln_g, lambda_init):
    nq = SEQ // TQ
    wide = HEADS_PER_STEP * 2 * HEAD_DIM
    return pl.pallas_call(
        functools.partial(_attn_kernel, lambda_init=lambda_init),
        grid=(BATCH, N_HEADS // HEADS_PER_STEP, nq),
        in_specs=[pl.BlockSpec((wide, TQ), lambda b, h, i: (h, b * nq + i)),
                  pl.BlockSpec((SEQ, wide), lambda b, h, i: (b, h)),
                  pl.BlockSpec((wide, SEQ), lambda b, h, i: (h, b)),
                  pl.BlockSpec((HEADS_PER_STEP, 2, BIAS_TILES, TK, TQ),
                               lambda b, h, i: (h, 0, 0, 0, 0)),
                  pl.BlockSpec((4, HEAD_DIM), lambda b, h, i: (0, 0)),
                  pl.BlockSpec((1, 2 * HEAD_DIM), lambda b, h, i: (0, 0))],
        out_specs=pl.BlockSpec((TQ, wide), lambda b, h, i: (b * nq + i, h)),
        out_shape=jax.ShapeDtypeStruct((TOKENS, D_MODEL), BF16),
        scratch_shapes=[pltpu.VMEM((HEADS_PER_STEP, 2, 1, TQ), F32),
                        pltpu.VMEM((HEADS_PER_STEP, 2, 2 * HEAD_DIM + SUM_ROWS, TQ), F32),
                        pltpu.VMEM((EARLY, TK, TQ_HALF), BF16), pltpu.VMEM((LATE, TK, TQ_HALF), BF16),
                        pltpu.VMEM((LATE, 1, TQ_HALF), F32)],
        compiler_params=_params(3),
        name="diff_attn",
    )(qt, k, vt, bias, lam_vecs, subln_g)


def _ssm_operators(lam_re, lam_im, log_dt, b_re, b_im, c_re, c_im, d_skip):
    lam_re, lam_im, log_dt, b_re, b_im, c_re, c_im, d_skip = lax.optimization_barrier(
        (lam_re, lam_im, log_dt, b_re, b_im, c_re, c_im, d_skip))
    dt = jnp.exp(log_dt)[:, None]
    zr, zi = lam_re * dt, lam_im * dt
    ks = jnp.arange(SSM_L + 1, dtype=F32)[:, None, None]
    mag = jnp.exp(ks * zr)
    pr, pi = mag * jnp.cos(ks * zi), mag * jnp.sin(ks * zi)
    nr = jnp.expm1(zr) * jnp.cos(zi) - 2.0 * jnp.sin(0.5 * zi) ** 2
    ni = pi[1]
    den = lam_re * lam_re + lam_im * lam_im
    fr, fi = (nr * lam_re + ni * lam_im) / den, (ni * lam_re - nr * lam_im) / den
    bb_re = fr[..., None] * b_re - fi[..., None] * b_im
    bb_im = fr[..., None] * b_im + fi[..., None] * b_re
    same = jnp.eye(OCT_GROUPS, dtype=F32)

    def block_rows(t):
        t = lax.optimization_barrier(t)
        return (t[:, :, :, None, :] * same[None, :, None, :, None]).reshape(OCTETS, LANES, OCT_STATE)

    b_rows = lambda t: block_rows(t.reshape(OCTETS, OCT_GROUPS, SSM_STATE, GROUP_CH)
                                  .transpose(0, 1, 3, 2))
    c_rows = lambda t: block_rows(t.reshape(OCTETS, OCT_GROUPS, GROUP_CH, SSM_STATE))
    bmat = jnp.stack([b_rows(bb_re), b_rows(bb_im)], axis=1)
    cmat = jnp.stack([c_rows(c_re), c_rows(c_im)], axis=1)

    kr = jnp.arange(SSM_L - 1, -1, -1, dtype=F32)[:, None, None]
    mag_r = jnp.exp(kr * zr)
    per_oct = lambda t: t.reshape(t.shape[:-2] + (OCTETS, OCT_STATE))
    pw_b = per_oct(jnp.stack([mag_r * jnp.cos(kr * zi), mag_r * jnp.sin(kr * zi)]))
    pw_c = per_oct(jnp.stack([pr[1:], pi[1:]]))
    a_chunk = per_oct(jnp.stack([pr[SSM_L], pi[SSM_L]]))
    return (bmat, cmat, pw_b.transpose(2, 0, 1, 3), pw_c.transpose(2, 0, 1, 3),
            a_chunk.transpose(1, 0, 2), d_skip.reshape(OCTETS, 1, LANES))


def _scaled_blocks(mat_ref, pw_ref, out_ref, im_sign):
    def block(s, carry):
        m_re, m_im = mat_ref[0], mat_ref[1]
        p_re, p_im = pw_ref[0, pl.ds(s, 1), :], pw_ref[1, pl.ds(s, 1), :]
        rows = pl.ds(pl.multiple_of(s * LANES, LANES), LANES)
        out_ref[rows, 0:OCT_STATE] = (m_re * p_re - m_im * p_im).astype(BF16)
        out_ref[rows, OCT_STATE:2 * OCT_STATE] = (im_sign * (m_re * p_im + m_im * p_re)).astype(BF16)
        return carry

    lax.fori_loop(0, SSM_L, block, 0)


def _s5_state_kernel(u_ref, bmat_ref, pw_ref, a_ref, xs_ref, bpow_scr, s_scr, x_scr):
    _scaled_blocks(bmat_ref, pw_ref, bpow_scr, 1.0)
    for r in range(0, SSM_ROWS, SSM_RT):
        s_loc = _dot(u_ref[r:r + SSM_RT, :], bpow_scr[...])
        for k in range(STATE_COLS):
            s_scr[k, r:r + SSM_RT, :] = s_loc[:, k * LANES:(k + 1) * LANES]
    a = a_ref[...]
    col = lambda r, k: jnp.broadcast_to(a[r:r + 1, k * LANES:(k + 1) * LANES], (BATCH, LANES))
    ar = [col(0, k) for k in range(RE_COLS)]
    ai = [col(1, k) for k in range(RE_COLS)]

    def step(c, carry):
        rows = pl.ds(c, BATCH, stride=SSM_NC)
        nxt_re, nxt_im = [], []
        for k in range(RE_COLS):
            xr, xi = carry[k], carry[RE_COLS + k]
            x_scr[k, rows, :] = xr
            x_scr[RE_COLS + k, rows, :] = xi
            nxt_re.append(ar[k] * xr - ai[k] * xi + s_scr[k, rows, :])
            nxt_im.append(ar[k] * xi + ai[k] * xr + s_scr[RE_COLS + k, rows, :])
        return tuple(nxt_re + nxt_im)

    zeros = jnp.zeros((BATCH, LANES), F32)
    lax.fori_loop(0, SSM_NC, step, (zeros,) * STATE_COLS, unroll=4)
    for k in range(STATE_COLS):
        xs_ref[:, k * LANES:(k + 1) * LANES] = x_scr[k].astype(BF16)


def _s5_states(u_oct, bmat, pw_b, a_chunk):
    per_oct = lambda *tail: pl.BlockSpec((None,) + tail, lambda j: (j,) + (0,) * len(tail))
    return pl.pallas_call(
        _s5_state_kernel,
        grid=(OCTETS,),
        in_specs=[per_oct(SSM_ROWS, OCT_W), per_oct(2, LANES, OCT_STATE),
                  per_oct(2, SSM_L, OCT_STATE), per_oct(2, OCT_STATE)],
        out_specs=per_oct(SSM_ROWS, 2 * OCT_STATE),
        out_shape=jax.ShapeDtypeStruct((OCTETS, SSM_ROWS, 2 * OCT_STATE), BF16),
        scratch_shapes=[pltpu.VMEM((OCT_W, 2 * OCT_STATE), BF16),
                        pltpu.VMEM((STATE_COLS, SSM_ROWS, LANES), F32),
                        pltpu.VMEM((STATE_COLS, SSM_ROWS, LANES), F32)],
        compiler_params=_params(),
        name="s5_states",
    )(u_oct, bmat, pw_b, a_chunk)


def _s5_out_kernel(u_ref, xs_ref, bmat_ref, cmat_ref, pw_ref, d_ref, y_ref, toep_scr, cpow_scr):
    @pl.when(pl.program_id(1) == 0)
    def _():
        _scaled_blocks(cmat_ref, pw_ref, cpow_scr, -1.0)
        b_cat = jnp.concatenate([bmat_ref[0], bmat_ref[1]], axis=1).astype(BF16)
        c_tau0 = jnp.concatenate([cmat_ref[0], -cmat_ref[1]], axis=1).astype(BF16)
        c_all = jnp.concatenate([c_tau0, cpow_scr[0:OCT_W - LANES, :]], axis=0)
        imp = lax.dot_general(b_cat, c_all, _NT, preferred_element_type=F32)
        on_diag = (lax.broadcasted_iota(jnp.int32, (LANES, LANES), 0)
                   == lax.broadcasted_iota(jnp.int32, (LANES, LANES), 1))
        skip = jnp.where(on_diag, d_ref[...], 0.0)
        imp = jnp.concatenate([imp[:, 0:LANES] + skip, imp[:, LANES:]], axis=1).astype(BF16)
        toep_scr[...] = jnp.zeros(toep_scr.shape, BF16)
        for s in range(SSM_L):
            toep_scr[s * LANES:(s + 1) * LANES, s * LANES:] = imp[:, :OCT_W - s * LANES]

    u, xs = u_ref[...], xs_ref[...]
    for lo in range(0, OCT_W, MXU_TILE):
        hi = lo + MXU_TILE
        y = _dot(u[:, :hi], toep_scr[0:hi, lo:hi]) + lax.dot_general(
            xs, cpow_scr[lo:hi, :], _NT, preferred_element_type=F32)
        y_ref[:, lo:hi] = y.astype(BF16)


def _s5_output(u_oct, xs, bmat, cmat, pw_c, d_oct):
    rows = lambda w: pl.BlockSpec((None, SSM_RT, w), lambda j, r: (j, r, 0))
    per_oct = lambda *tail: pl.BlockSpec((None,) + tail, lambda j, r: (j,) + (0,) * len(tail))
    return pl.pallas_call(
        _s5_out_kernel,
        grid=(OCTETS, SSM_ROWS // SSM_RT),
        in_specs=[rows(OCT_W), rows(2 * OCT_STATE), per_oct(2, LANES, OCT_STATE),
                  per_oct(2, LANES, OCT_STATE), per_oct(2, SSM_L, OCT_STATE), per_oct(1, LANES)],
        out_specs=rows(OCT_W),
        out_shape=jax.ShapeDtypeStruct((OCTETS, SSM_ROWS, OCT_W), BF16),
        scratch_shapes=[pltpu.VMEM((OCT_W, OCT_W), BF16), pltpu.VMEM((OCT_W, 2 * OCT_STATE), BF16)],
        compiler_params=_params(2),
        name="s5_output",
    )(u_oct, xs, bmat, cmat, pw_c, d_oct)


def _gelu_tanh(x):
    return 0.5 * x * (1.0 + jnp.tanh(math.sqrt(2.0 / math.pi) * (x + 0.044715 * (x * x * x))))


def _tail_stages(r, x, mixed, g_ref, wi_ref, wo_ref, p_ref, wg_ref, wp_ref, o_ref):
    x = x + _rms(mixed, g_ref[3:4, :])
    res = []
    yield from _ffn_stages(x, g_ref[4:5, :], g_ref[5:6, :], wi_ref, wo_ref, res)
    x = res[0]
    h = _rms(x, g_ref[6:7, :]).astype(BF16)
    yield
    gate = _dot(h, wg_ref[...])
    emb = _dot(p_ref[_sub_rows(r), :].astype(BF16), wp_ref[...])
    yield
    o_ref[_sub_rows(r), :] = x + _rms(_sigmoid(gate) * emb, g_ref[7:8, :])


def _tail_attn_kernel(a_ref, x_ref, wm_ref, g_ref, wi_ref, wo_ref, p_ref, wg_ref, wp_ref, o_ref):
    def stages(r):
        mixed = _dot(a_ref[_sub_rows(r), :], wm_ref[...])
        yield
        yield from _tail_stages(r, x_ref[_sub_rows(r), :], mixed, g_ref, wi_ref, wo_ref, p_ref,
                                wg_ref, wp_ref, o_ref)

    _skewed(stages(r) for r in range(SUB_TILES))


def _tail_ssm_kernel(y_ref, x_ref, wm_ref, bm_ref, g_ref, wi_ref, wo_ref, p_ref, wg_ref, wp_ref,
                     o_ref, y_scr):
    def stages(r):
        act = _gelu_tanh(_from_super_rows(y_ref, r, y_scr)).astype(BF16)
        yield
        z = _dot(act, wm_ref[...]) + bm_ref[...]
        yield
        mixed = z[:, :D_MODEL] * _sigmoid(z[:, D_MODEL:])
        yield from _tail_stages(r, x_ref[_sub_rows(r), :], mixed, g_ref, wi_ref, wo_ref, p_ref,
                                wg_ref, wp_ref, o_ref)

    _skewed(stages(r) for r in range(SUB_TILES))


def _layer_tail(mix, x, p, layer, g, w_mix, b_mix, w_in, w_out, w_gate, w_proj):
    ple_specs = [pl.BlockSpec((None, TM, PLE_DIM), lambda b, c: (layer, b * TILES_PER_SEQ + c, 0)),
                 _const_spec((D_MODEL, D_MODEL)), _const_spec((PLE_DIM, D_MODEL))]
    common = dict(grid=_TOKEN_GRID, out_specs=_TOKEN_TILE,
                  out_shape=jax.ShapeDtypeStruct((TOKENS, D_MODEL), F32),
                  compiler_params=_params(2))
    if b_mix is None:
        return pl.pallas_call(
            _tail_attn_kernel,
            in_specs=[_TOKEN_TILE, _TOKEN_TILE, _const_spec((D_MODEL, D_MODEL))] + _FFN_SPECS
            + ple_specs,
            name="attn_out_ffn_ple", **common,
        )(mix, x, w_mix, g, w_in, w_out, p, w_gate, w_proj)
    return pl.pallas_call(
        _tail_ssm_kernel,
        in_specs=[_SUPER_TILE, _TOKEN_TILE, _const_spec((D_MODEL, 2 * D_MODEL)),
                  _const_spec((1, 2 * D_MODEL))] + _FFN_SPECS + ple_specs,
        scratch_shapes=[pltpu.VMEM((OCTETS, TM, LANES), F32)],
        name="s5_glu_ffn_ple", **common,
    )(mix, x, w_mix, b_mix, g, w_in, w_out, p, w_gate, w_proj)


def kernel(x, p, norm_g, ffn_w_in, ffn_w_out, attn_w_qkv, attn_w_o, attn_lam, attn_subln_g,
           rel_bias, ssm_lam_re, ssm_lam_im, ssm_log_dt, ssm_b_re, ssm_b_im, ssm_c_re, ssm_c_im,
           ssm_d, ssm_w_glu, ssm_b_glu, ple_w_proj, ple_w_gate):
    x = x.reshape(TOKENS, D_MODEL)
    p = p.reshape(DEPTH, TOKENS, PLE_DIM)
    bias = _bias_tiles(rel_bias)
    bf = lambda w: w.astype(BF16)
    for i in range(DEPTH):
        g = norm_g[i]
        j = i // N_MIXERS
        if i % N_MIXERS == 0:
            lambda_init = 0.8 - 0.6 * math.exp(-0.3 * i)
            w = bf(attn_w_qkv[j])
            x, qt, k, vt = _layer_head(x, g, bf(ffn_w_in[i, 0]), bf(ffn_w_out[i, 0]),
                                       (w[:, :D_MODEL].T, w[:, D_MODEL:2 * D_MODEL],
                                        w[:, 2 * D_MODEL:].T))
            mix = _attention(qt, k, vt, bias, attn_lam[j], attn_subln_g[j].reshape(1, -1),
                             lambda_init)
            w_mix, b_mix = bf(attn_w_o[j]), None
        else:
            bmat, cmat, pw_b, pw_c, a_chunk, d_oct = _ssm_operators(
                ssm_lam_re[j], ssm_lam_im[j], ssm_log_dt[j], ssm_b_re[j], ssm_b_im[j],
                ssm_c_re[j], ssm_c_im[j], ssm_d[j])
            x, u_oct = _layer_head(x, g, bf(ffn_w_in[i, 0]), bf(ffn_w_out[i, 0]))
            xs = _s5_states(u_oct, bmat, pw_b, a_chunk)
            mix = _s5_output(u_oct, xs, bmat, cmat, pw_c, d_oct)
            w_mix, b_mix = bf(ssm_w_glu[j]), ssm_b_glu[j].reshape(1, -1)
        x = _layer_tail(mix, x, p, i, g, w_mix, b_mix, bf(ffn_w_in[i, 1]), bf(ffn_w_out[i, 1]),
                        bf(ple_w_gate[i]), bf(ple_w_proj[i]))
    return x.reshape(BATCH, SEQ, D_MODEL)
```

```python
import functools
import math

import numpy as np
import jax
import jax.numpy as jnp
from jax import lax
from jax.experimental import pallas as pl
from jax.experimental.pallas import tpu as pltpu

D_MODEL = 1024
BATCH = 8
SEQ = 4096
DEPTH = 4
N_MIXERS = 2
HEAD_DIM = 64
N_HEADS = D_MODEL // (2 * HEAD_DIM)
REL_BUCKETS = 32
REL_MAX_DIST = 128
GROUP_CH = 16
GROUPS = D_MODEL // GROUP_CH
SSM_STATE = 64
D_FF = 2816
FFN_RESIDUAL = 0.5
PLE_DIM = 256
N_NORMS = 8
RMS_EPS = 1e-6
NEG_INF = -1e30

TOKENS = BATCH * SEQ
F32 = jnp.float32
BF16 = jnp.bfloat16
LANES = 128

V7X_VMEM_BYTES = 64 * 1024 * 1024
VMEM_LIMIT = V7X_VMEM_BYTES - 8 * 1024 * 1024

TM = 512
TILES_PER_SEQ = SEQ // TM
SUB_TILES = 2
SUB_TM = TM // SUB_TILES
MXU_TILE = 256
FF_SPLITS = (0, 6 * MXU_TILE, D_FF)
assert D_FF % MXU_TILE == 0
TQ = 512
TK = 512
BIAS_TILES = 3
TQ_HALF = TQ // 2
SUM_ROWS = 16
HEADS_PER_STEP = 8
EARLY = 2
LATE = 2
LOG2E = math.log2(math.e)

SSM_L = 16
SSM_NC = SEQ // SSM_L
OCTETS = D_MODEL // LANES
OCT_GROUPS = LANES // GROUP_CH
OCT_W = SSM_L * LANES
OCT_STATE = OCT_GROUPS * SSM_STATE
SSM_ROWS = BATCH * SSM_NC
SSM_RT = 512
CH_PER_TILE = TM // SSM_L
SUB_CH = SUB_TM // SSM_L
RE_COLS = OCT_STATE // LANES
STATE_COLS = 2 * RE_COLS


def _t5_thresholds():
    n = np.arange(0, 4 * REL_MAX_DIST)
    max_exact = REL_BUCKETS // 2
    nf = np.maximum(n, 1).astype(np.float64)
    large = max_exact + (np.log(nf / max_exact) / math.log(REL_MAX_DIST / max_exact)
                         * (REL_BUCKETS - max_exact)).astype(np.int32)
    bucket = np.where(n < max_exact, n, np.minimum(large, REL_BUCKETS - 1))
    return [int(np.argmax(bucket >= j)) for j in range(1, REL_BUCKETS)]


T5_THRESHOLDS = _t5_thresholds()
assert TQ == TK and T5_THRESHOLDS[-1] <= TK, "key blocks before i-1 sit in the last bucket"

_NT = (((1,), (1,)), ((), ()))


def _const_spec(shape):
    nd = len(shape)
    return pl.BlockSpec(shape, lambda *_: (0,) * nd, pipeline_mode=pl.Buffered(1))


def _params(n_axes=1):
    return pltpu.CompilerParams(dimension_semantics=("arbitrary",) * n_axes,
                                vmem_limit_bytes=VMEM_LIMIT)


def _rms(x, g):
    return x * lax.rsqrt(jnp.mean(x * x, axis=-1, keepdims=True) + RMS_EPS) * g


def _sigmoid(x):
    return 1.0 / (1.0 + jnp.exp(-x))


def _dot(a, b):
    return jnp.dot(a, b, preferred_element_type=F32)


_TOKEN_TILE = pl.BlockSpec((TM, D_MODEL), lambda b, c: (b * TILES_PER_SEQ + c, 0))
_FEATURE_TILE = pl.BlockSpec((D_MODEL, TM), lambda b, c: (0, b * TILES_PER_SEQ + c))
_SUPER_TILE = pl.BlockSpec((OCTETS, CH_PER_TILE, OCT_W),
                           lambda b, c: (0, b * TILES_PER_SEQ + c, 0))
_TOKEN_GRID = (BATCH, TILES_PER_SEQ)
_FFN_SPECS = [_const_spec((N_NORMS, D_MODEL)), _const_spec((D_MODEL, 2 * D_FF)),
              _const_spec((D_FF, D_MODEL))]


def _skewed(stage_gens):
    waiting, running = list(stage_gens), []
    while waiting or running:
        if waiting:
            running.append(waiting.pop(0))
        for gen in list(running):
            if next(gen, _DONE) is _DONE:
                running.remove(gen)


_DONE = object()


def _sub_rows(r):
    return pl.ds(r * SUB_TM, SUB_TM)


def _ffn_stages(x, g_pre, g_post, wi_ref, wo_ref, out):
    h = _rms(x, g_pre).astype(BF16)
    yield
    y = None
    for lo, hi in zip(FF_SPLITS[:-1], FF_SPLITS[1:]):
        a = _dot(h, wi_ref[:, lo:hi])
        u = _dot(h, wi_ref[:, D_FF + lo:D_FF + hi])
        yield
        act = (a * _sigmoid(a) * u).astype(BF16)
        yield
        down = _dot(act, wo_ref[lo:hi, :])
        y = down if y is None else y + down
        yield
    out.append(x + FFN_RESIDUAL * _rms(y, g_post))


def _to_super_rows(h, r, o_ref, h_scr):
    chunks = slice(r * SUB_CH, (r + 1) * SUB_CH)
    for j in range(OCTETS):
        h_scr[j, _sub_rows(r), :] = h[:, j * LANES:(j + 1) * LANES]
    for j in range(OCTETS):
        for s in range(SSM_L):
            rows = h_scr[j, pl.ds(r * SUB_TM + s, SUB_CH, stride=SSM_L), :]
            o_ref[j, chunks, s * LANES:(s + 1) * LANES] = rows.astype(BF16)


def _from_super_rows(y_ref, r, y_scr):
    chunks = slice(r * SUB_CH, (r + 1) * SUB_CH)
    for j in range(OCTETS):
        for s in range(SSM_L):
            y_scr[j, pl.ds(r * SUB_TM + s, SUB_CH, stride=SSM_L), :] = (
                y_ref[j, chunks, s * LANES:(s + 1) * LANES].astype(F32))
    return jnp.concatenate([y_scr[j, _sub_rows(r), :] for j in range(OCTETS)], axis=1)


def _head_stages(r, x_ref, g_ref, wi_ref, wo_ref, xo_ref, out):
    res = []
    yield from _ffn_stages(x_ref[_sub_rows(r), :], g_ref[0:1, :], g_ref[1:2, :], wi_ref, wo_ref,
                           res)
    xo_ref[_sub_rows(r), :] = res[0]
    out.append(_rms(res[0], g_ref[2:3, :]))
    yield


def _head_attn_kernel(x_ref, g_ref, wi_ref, wo_ref, wqt_ref, wk_ref, wvt_ref,
                      xo_ref, qt_ref, k_ref, vt_ref):
    def stages(r):
        res = []
        yield from _head_stages(r, x_ref, g_ref, wi_ref, wo_ref, xo_ref, res)
        h = res[0].astype(BF16)
        cols = _sub_rows(r)
        qt = lax.dot_general(wqt_ref[...], h, _NT, preferred_element_type=F32)
        qt_ref[:, cols] = (qt * (HEAD_DIM ** -0.5 * LOG2E)).astype(BF16)
        k_ref[cols, :] = _dot(h, wk_ref[...]).astype(BF16)
        vt = lax.dot_general(wvt_ref[...], h, _NT, preferred_element_type=F32)
        vt_ref[:, cols] = vt.astype(BF16)

    _skewed(stages(r) for r in range(SUB_TILES))


def _head_ssm_kernel(x_ref, g_ref, wi_ref, wo_ref, xo_ref, u_ref, h_scr):
    def stages(r):
        res = []
        yield from _head_stages(r, x_ref, g_ref, wi_ref, wo_ref, xo_ref, res)
        _to_super_rows(res[0], r, u_ref, h_scr)

    _skewed(stages(r) for r in range(SUB_TILES))


def _layer_head(x, g, w_in, w_out, qkv=None):
    x_out = jax.ShapeDtypeStruct((TOKENS, D_MODEL), F32)
    if qkv is None:
        return pl.pallas_call(
            _head_ssm_kernel,
            grid=_TOKEN_GRID,
            in_specs=[_TOKEN_TILE] + _FFN_SPECS,
            out_specs=(_TOKEN_TILE, _SUPER_TILE),
            out_shape=(x_out, jax.ShapeDtypeStruct((OCTETS, SSM_ROWS, OCT_W), BF16)),
            scratch_shapes=[pltpu.VMEM((OCTETS, TM, LANES), F32)],
            compiler_params=_params(2),
            name="ffn_s5in",
        )(x, g, w_in, w_out)
    feat = jax.ShapeDtypeStruct((D_MODEL, TOKENS), BF16)
    return pl.pallas_call(
        _head_attn_kernel,
        grid=_TOKEN_GRID,
        in_specs=[_TOKEN_TILE] + _FFN_SPECS + [_const_spec((D_MODEL, D_MODEL))] * 3,
        out_specs=(_TOKEN_TILE, _FEATURE_TILE, _TOKEN_TILE, _FEATURE_TILE),
        out_shape=(x_out, feat, jax.ShapeDtypeStruct((TOKENS, D_MODEL), BF16), feat),
        compiler_params=_params(2),
        name="ffn_qkv",
    )(x, g, w_in, w_out, *qkv)


def _bias_kernel(rel_ref, o_ref):
    c = pl.program_id(0)
    r = pl.program_id(1)
    @pl.when(r == BIAS_TILES - 1)
    def _():
        o_ref[...] = jnp.zeros((TK, TQ), BF16)

    @pl.when(r < BIAS_TILES - 1)
    def _():
        ki = lax.broadcasted_iota(jnp.int32, (TK, TQ), 0)
        qi = lax.broadcasted_iota(jnp.int32, (TK, TQ), 1)
        d = qi - ki + r * TK
        val = jnp.full((TK, TQ), rel_ref[0, c], F32)
        for j, thr in enumerate(T5_THRESHOLDS, start=1):
            val = jnp.where(d >= thr, rel_ref[j, c], val)
        val = (val - rel_ref[REL_BUCKETS - 1, c]) * LOG2E
        o_ref[...] = jnp.where(d >= 0, val, NEG_INF).astype(BF16)


def _bias_tiles(rel_bias):
    tiles = pl.pallas_call(
        _bias_kernel,
        grid=(2 * N_HEADS, BIAS_TILES),
        in_specs=[pl.BlockSpec(memory_space=pltpu.SMEM)],
        out_specs=pl.BlockSpec((None, None, TK, TQ), lambda c, r: (c, r, 0, 0)),
        out_shape=jax.ShapeDtypeStruct((2 * N_HEADS, BIAS_TILES, TK, TQ), BF16),
        compiler_params=_params(2),
        name="t5_bias",
    )(rel_bias)
    return tiles.reshape(N_HEADS, 2, BIAS_TILES, TK, TQ)


def _attn_kernel(qt_ref, k_ref, vt_ref, bias_ref, lam_ref, sg_ref, o_ref, m_ref, acc_ref,
                 s_buf, p_buf, a_buf, *, lambda_init):
    i = pl.program_id(2)
    d = 2 * HEAD_DIM
    chains = [(hd, mi, slice(hf * TQ_HALF, (hf + 1) * TQ_HALF))
              for hd in range(HEADS_PER_STEP) for hf in range(2) for mi in range(2)]
    n = len(chains)

    qt = qt_ref[...]
    row = lax.broadcasted_iota(jnp.int32, (d, TQ), 0)
    zero = jnp.zeros((d, TQ), BF16)
    q_maps = [[jnp.where(row < HEAD_DIM, qt[hd * d:(hd + 1) * d], zero),
               jnp.where(row >= HEAD_DIM, qt[hd * d:(hd + 1) * d], zero)]
              for hd in range(HEADS_PER_STEP)]

    m_ref[...] = jnp.full(m_ref.shape, NEG_INF, F32)
    acc_ref[...] = jnp.zeros(acc_ref.shape, F32)
    sum_rows = jnp.where(lax.broadcasted_iota(jnp.int32, (SUM_ROWS, TK), 0) == 0, 1.0, 0.0).astype(BF16)

    def keys(j):
        return k_ref[pl.ds(pl.multiple_of(j * TK, TK), TK), :]

    def vals(j):
        vb = vt_ref[:, pl.ds(pl.multiple_of(j * TK, TK), TK)]
        return [jnp.concatenate([vb[hd * d:(hd + 1) * d], sum_rows], axis=0)
                for hd in range(HEADS_PER_STEP)]

    def scores(kb, j, chain):
        hd, mi, cols = chain
        tile = jnp.clip(i - j, 0, BIAS_TILES - 1)
        s = _dot(kb[:, hd * d:(hd + 1) * d], q_maps[hd][mi][:, cols])
        return s.astype(BF16) + bias_ref[hd, mi, tile, :, cols]

    def softmax(chain, s):
        hd, mi, cols = chain
        m_prev = m_ref[hd, mi, :, cols]
        m_new = jnp.maximum(m_prev, jnp.max(s, axis=0, keepdims=True).astype(F32))
        alpha = jnp.exp2(m_prev - m_new)
        p = jnp.exp2(s - m_new.astype(BF16))
        m_ref[hd, mi, :, cols] = m_new
        return p, alpha

    def values(vb, chain, p, alpha):
        hd, mi, cols = chain
        acc_ref[hd, mi, :, cols] = alpha * acc_ref[hd, mi, :, cols] + _dot(vb[hd], p)

    kb0 = keys(0)
    for e in range(EARLY):
        s_buf[e] = scores(kb0, 0, chains[e])
    for t in range(LATE):
        p_buf[t] = jnp.zeros((TK, TQ_HALF), BF16)
        a_buf[t] = jnp.ones((1, TQ_HALF), F32)

    def trip(j, carry):
        kb, vb = keys(j), vals(j)
        j_next = jnp.minimum(j + 1, i)
        kb_next, vb_prev = keys(j_next), vals(jnp.maximum(j - 1, 0))
        s_tiles, p_tiles = {}, {}
        for slot in range(n):
            ahead = slot + EARLY
            if ahead < n:
                s_tiles[ahead] = scores(kb, j, chains[ahead])
            else:
                s_next = scores(kb_next, j_next, chains[ahead - n])
            s = s_buf[slot] if slot < EARLY else s_tiles.pop(slot)
            p_tiles[slot] = softmax(chains[slot], s)
            if ahead >= n:
                s_buf[ahead - n] = s_next
            behind = slot - LATE
            if behind >= 0:
                values(vb, chains[behind], *p_tiles.pop(behind))
            else:
                values(vb_prev, chains[n + behind], p_buf[slot], a_buf[slot])
            if slot >= n - LATE:
                p_buf[slot - (n - LATE)], a_buf[slot - (n - LATE)] = p_tiles.pop(slot)
        return carry

    lax.fori_loop(0, i + 1, trip, 0)
    vb_last = vals(i)
    for t in range(LATE):
        values(vb_last, chains[n - LATE + t], p_buf[t], a_buf[t])

    lv = lam_ref[...]
    lam = (jnp.exp(jnp.sum(lv[0:1] * lv[1:2], keepdims=True))
           - jnp.exp(jnp.sum(lv[2:3] * lv[3:4], keepdims=True)) + lambda_init)
    for hd in range(HEADS_PER_STEP):
        ot = (acc_ref[hd, 0, 0:d, :] * (1.0 / acc_ref[hd, 0, d:d + 1, :])
              - lam * (acc_ref[hd, 1, 0:d, :] * (1.0 / acc_ref[hd, 1, d:d + 1, :])))
        o = _rms(ot.T, sg_ref[...]) * (1.0 - lambda_init)
        o_ref[:, hd * d:(hd + 1) * d] = o.astype(BF16)


def _attention(qt, k, vt, bias, lam_vecs, subln_g, lambda_init):
    nq = SEQ // TQ
    wide = HEADS_PER_STEP * 2 * HEAD_DIM
    return pl.pallas_call(
        functools.partial(_attn_kernel, lambda_init=lambda_init),
        grid=(BATCH, N_HEADS // HEADS_PER_STEP, nq),
        in_specs=[pl.BlockSpec((wide, TQ), lambda b, h, i: (h, b * nq + i)),
                  pl.BlockSpec((SEQ, wide), lambda b, h, i: (b, h), pipeline_mode=pl.Buffered(1)),
                  pl.BlockSpec((wide, SEQ), lambda b, h, i: (h, b), pipeline_mode=pl.Buffered(1)),
                  pl.BlockSpec((HEADS_PER_STEP, 2, BIAS_TILES, TK, TQ),
                               lambda b, h, i: (h, 0, 0, 0, 0), pipeline_mode=pl.Buffered(1)),
                  pl.BlockSpec((4, HEAD_DIM), lambda b, h, i: (0, 0)),
                  pl.BlockSpec((1, 2 * HEAD_DIM), lambda b, h, i: (0, 0))],
        out_specs=pl.BlockSpec((TQ, wide), lambda b, h, i: (b * nq + i, h)),
        out_shape=jax.ShapeDtypeStruct((TOKENS, D_MODEL), BF16),
        scratch_shapes=[pltpu.VMEM((HEADS_PER_STEP, 2, 1, TQ), F32),
                        pltpu.VMEM((HEADS_PER_STEP, 2, 2 * HEAD_DIM + SUM_ROWS, TQ), F32),
                        pltpu.VMEM((EARLY, TK, TQ_HALF), BF16), pltpu.VMEM((LATE, TK, TQ_HALF), BF16),
                        pltpu.VMEM((LATE, 1, TQ_HALF), F32)],
        compiler_params=_params(3),
        name="diff_attn",
    )(qt, k, vt, bias, lam_vecs, subln_g)


def _ssm_operators(lam_re, lam_im, log_dt, b_re, b_im, c_re, c_im, d_skip):
    lam_re, lam_im, log_dt, b_re, b_im, c_re, c_im, d_skip = lax.optimization_barrier(
        (lam_re, lam_im, log_dt, b_re, b_im, c_re, c_im, d_skip))
    dt = jnp.exp(log_dt)[:, None]
    zr, zi = lam_re * dt, lam_im * dt
    ks = jnp.arange(SSM_L + 1, dtype=F32)[:, None, None]
    mag = jnp.exp(ks * zr)
    pr, pi = mag * jnp.cos(ks * zi), mag * jnp.sin(ks * zi)
    nr = jnp.expm1(zr) * jnp.cos(zi) - 2.0 * jnp.sin(0.5 * zi) ** 2
    ni = pi[1]
    den = lam_re * lam_re + lam_im * lam_im
    fr, fi = (nr * lam_re + ni * lam_im) / den, (ni * lam_re - nr * lam_im) / den
    bb_re = fr[..., None] * b_re - fi[..., None] * b_im
    bb_im = fr[..., None] * b_im + fi[..., None] * b_re
    same = jnp.eye(OCT_GROUPS, dtype=F32)

    def block_rows(t):
        t = lax.optimization_barrier(t)
        return (t[:, :, :, None, :] * same[None, :, None, :, None]).reshape(OCTETS, LANES, OCT_STATE)

    b_rows = lambda t: block_rows(t.reshape(OCTETS, OCT_GROUPS, SSM_STATE, GROUP_CH)
                                  .transpose(0, 1, 3, 2))
    c_rows = lambda t: block_rows(t.reshape(OCTETS, OCT_GROUPS, GROUP_CH, SSM_STATE))
    bmat = jnp.stack([b_rows(bb_re), b_rows(bb_im)], axis=1)
    cmat = jnp.stack([c_rows(c_re), c_rows(c_im)], axis=1)

    kr = jnp.arange(SSM_L - 1, -1, -1, dtype=F32)[:, None, None]
    mag_r = jnp.exp(kr * zr)
    per_oct = lambda t: t.reshape(t.shape[:-2] + (OCTETS, OCT_STATE))
    pw_b = per_oct(jnp.stack([mag_r * jnp.cos(kr * zi), mag_r * jnp.sin(kr * zi)]))
    pw_c = per_oct(jnp.stack([pr[1:], pi[1:]]))
    a_chunk = per_oct(jnp.stack([pr[SSM_L], pi[SSM_L]]))
    return (bmat, cmat, pw_b.transpose(2, 0, 1, 3), pw_c.transpose(2, 0, 1, 3),
            a_chunk.transpose(1, 0, 2), d_skip.reshape(OCTETS, 1, LANES))


def _scaled_blocks(mat_ref, pw_ref, out_ref, im_sign):
    def block(s, carry):
        m_re, m_im = mat_ref[0], mat_ref[1]
        p_re, p_im = pw_ref[0, pl.ds(s, 1), :], pw_ref[1, pl.ds(s, 1), :]
        rows = pl.ds(pl.multiple_of(s * LANES, LANES), LANES)
        out_ref[rows, 0:OCT_STATE] = (m_re * p_re - m_im * p_im).astype(BF16)
        out_ref[rows, OCT_STATE:2 * OCT_STATE] = (im_sign * (m_re * p_im + m_im * p_re)).astype(BF16)
        return carry

    lax.fori_loop(0, SSM_L, block, 0)


def _s5_state_kernel(u_ref, bmat_ref, pw_ref, a_ref, xs_ref, bpow_scr, s_scr, x_scr):
    _scaled_blocks(bmat_ref, pw_ref, bpow_scr, 1.0)
    for r in range(0, SSM_ROWS, SSM_RT):
        s_loc = _dot(u_ref[r:r + SSM_RT, :], bpow_scr[...])
        for k in range(STATE_COLS):
            s_scr[k, r:r + SSM_RT, :] = s_loc[:, k * LANES:(k + 1) * LANES]
    a = a_ref[...]
    col = lambda r, k: jnp.broadcast_to(a[r:r + 1, k * LANES:(k + 1) * LANES], (BATCH, LANES))
    ar = [col(0, k) for k in range(RE_COLS)]
    ai = [col(1, k) for k in range(RE_COLS)]

    def step(c, carry):
        rows = pl.ds(c, BATCH, stride=SSM_NC)
        nxt_re, nxt_im = [], []
        for k in range(RE_COLS):
            xr, xi = carry[k], carry[RE_COLS + k]
            x_scr[k, rows, :] = xr
            x_scr[RE_COLS + k, rows, :] = xi
            nxt_re.append(ar[k] * xr - ai[k] * xi + s_scr[k, rows, :])
            nxt_im.append(ar[k] * xi + ai[k] * xr + s_scr[RE_COLS + k, rows, :])
        return tuple(nxt_re + nxt_im)

    zeros = jnp.zeros((BATCH, LANES), F32)
    lax.fori_loop(0, SSM_NC, step, (zeros,) * STATE_COLS, unroll=4)
    for k in range(STATE_COLS):
        xs_ref[:, k * LANES:(k + 1) * LANES] = x_scr[k].astype(BF16)


def _s5_states(u_oct, bmat, pw_b, a_chunk):
    per_oct = lambda *tail: pl.BlockSpec((None,) + tail, lambda j: (j,) + (0,) * len(tail))
    return pl.pallas_call(
        _s5_state_kernel,
        grid=(OCTETS,),
        in_specs=[per_oct(SSM_ROWS, OCT_W), per_oct(2, LANES, OCT_STATE),
                  per_oct(2, SSM_L, OCT_STATE), per_oct(2, OCT_STATE)],
        out_specs=per_oct(SSM_ROWS, 2 * OCT_STATE),
        out_shape=jax.ShapeDtypeStruct((OCTETS, SSM_ROWS, 2 * OCT_STATE), BF16),
        scratch_shapes=[pltpu.VMEM((OCT_W, 2 * OCT_STATE), BF16),
                        pltpu.VMEM((STATE_COLS, SSM_ROWS, LANES), F32),
                        pltpu.VMEM((STATE_COLS, SSM_ROWS, LANES), F32)],
        compiler_params=_params(),
        name="s5_states",
    )(u_oct, bmat, pw_b, a_chunk)


def _s5_out_kernel(u_ref, xs_ref, bmat_ref, cmat_ref, pw_ref, d_ref, y_ref, toep_scr, cpow_scr):
    @pl.when(pl.program_id(1) == 0)
    def _():
        _scaled_blocks(cmat_ref, pw_ref, cpow_scr, -1.0)
        b_cat = jnp.concatenate([bmat_ref[0], bmat_ref[1]], axis=1).astype(BF16)
        c_tau0 = jnp.concatenate([cmat_ref[0], -cmat_ref[1]], axis=1).astype(BF16)
        c_all = jnp.concatenate([c_tau0, cpow_scr[0:OCT_W - LANES, :]], axis=0)
        imp = lax.dot_general(b_cat, c_all, _NT, preferred_element_type=F32)
        on_diag = (lax.broadcasted_iota(jnp.int32, (LANES, LANES), 0)
                   == lax.broadcasted_iota(jnp.int32, (LANES, LANES), 1))
        skip = jnp.where(on_diag, d_ref[...], 0.0)
        imp = jnp.concatenate([imp[:, 0:LANES] + skip, imp[:, LANES:]], axis=1).astype(BF16)
        toep_scr[...] = jnp.zeros(toep_scr.shape, BF16)
        for s in range(SSM_L):
            toep_scr[s * LANES:(s + 1) * LANES, s * LANES:] = imp[:, :OCT_W - s * LANES]

    u, xs = u_ref[...], xs_ref[...]
    for lo in range(0, OCT_W, MXU_TILE):
        hi = lo + MXU_TILE
        y = _dot(u[:, :hi], toep_scr[0:hi, lo:hi]) + lax.dot_general(
            xs, cpow_scr[lo:hi, :], _NT, preferred_element_type=F32)
        y_ref[:, lo:hi] = y.astype(BF16)


def _s5_output(u_oct, xs, bmat, cmat, pw_c, d_oct):
    rows = lambda w: pl.BlockSpec((None, SSM_RT, w), lambda j, r: (j, r, 0))
    per_oct = lambda *tail: pl.BlockSpec((None,) + tail, lambda j, r: (j,) + (0,) * len(tail))
    return pl.pallas_call(
        _s5_out_kernel,
        grid=(OCTETS, SSM_ROWS // SSM_RT),
        in_specs=[rows(OCT_W), rows(2 * OCT_STATE), per_oct(2, LANES, OCT_STATE),
                  per_oct(2, LANES, OCT_STATE), per_oct(2, SSM_L, OCT_STATE), per_oct(1, LANES)],
        out_specs=rows(OCT_W),
        out_shape=jax.ShapeDtypeStruct((OCTETS, SSM_ROWS, OCT_W), BF16),
        scratch_shapes=[pltpu.VMEM((OCT_W, OCT_W), BF16), pltpu.VMEM((OCT_W, 2 * OCT_STATE), BF16)],
        compiler_params=_params(2),
        name="s5_output",
    )(u_oct, xs, bmat, cmat, pw_c, d_oct)


def _gelu_tanh(x):
    return 0.5 * x * (1.0 + jnp.tanh(math.sqrt(2.0 / math.pi) * (x + 0.044715 * (x * x * x))))


def _tail_stages(r, x, mixed, g_ref, wi_ref, wo_ref, p_ref, wg_ref, wp_ref, o_ref):
    x = x + _rms(mixed, g_ref[3:4, :])
    res = []
    yield from _ffn_stages(x, g_ref[4:5, :], g_ref[5:6, :], wi_ref, wo_ref, res)
    x = res[0]
    h = _rms(x, g_ref[6:7, :]).astype(BF16)
    yield
    gate = _dot(h, wg_ref[...])
    emb = _dot(p_ref[_sub_rows(r), :].astype(BF16), wp_ref[...])
    yield
    o_ref[_sub_rows(r), :] = x + _rms(_sigmoid(gate) * emb, g_ref[7:8, :])


def _tail_attn_kernel(a_ref, x_ref, wm_ref, g_ref, wi_ref, wo_ref, p_ref, wg_ref, wp_ref, o_ref):
    def stages(r):
        mixed = _dot(a_ref[_sub_rows(r), :], wm_ref[...])
        yield
        yield from _tail_stages(r, x_ref[_sub_rows(r), :], mixed, g_ref, wi_ref, wo_ref, p_ref,
                                wg_ref, wp_ref, o_ref)

    _skewed(stages(r) for r in range(SUB_TILES))


def _tail_ssm_kernel(y_ref, x_ref, wm_ref, bm_ref, g_ref, wi_ref, wo_ref, p_ref, wg_ref, wp_ref,
                     o_ref, y_scr):
    def stages(r):
        act = _gelu_tanh(_from_super_rows(y_ref, r, y_scr)).astype(BF16)
        yield
        z = _dot(act, wm_ref[...]) + bm_ref[...]
        yield
        mixed = z[:, :D_MODEL] * _sigmoid(z[:, D_MODEL:])
        yield from _tail_stages(r, x_ref[_sub_rows(r), :], mixed, g_ref, wi_ref, wo_ref, p_ref,
                                wg_ref, wp_ref, o_ref)

    _skewed(stages(r) for r in range(SUB_TILES))


def _layer_tail(mix, x, p, layer, g, w_mix, b_mix, w_in, w_out, w_gate, w_proj):
    ple_specs = [pl.BlockSpec((None, TM, PLE_DIM), lambda b, c: (layer, b * TILES_PER_SEQ + c, 0)),
                 _const_spec((D_MODEL, D_MODEL)), _const_spec((PLE_DIM, D_MODEL))]
    common = dict(grid=_TOKEN_GRID, out_specs=_TOKEN_TILE,
                  out_shape=jax.ShapeDtypeStruct((TOKENS, D_MODEL), F32),
                  compiler_params=_params(2))
    if b_mix is None:
        return pl.pallas_call(
            _tail_attn_kernel,
            in_specs=[_TOKEN_TILE, _TOKEN_TILE, _const_spec((D_MODEL, D_MODEL))] + _FFN_SPECS
            + ple_specs,
            name="attn_out_ffn_ple", **common,
        )(mix, x, w_mix, g, w_in, w_out, p, w_gate, w_proj)
    return pl.pallas_call(
        _tail_ssm_kernel,
        in_specs=[_SUPER_TILE, _TOKEN_TILE, _const_spec((D_MODEL, 2 * D_MODEL)),
                  _const_spec((1, 2 * D_MODEL))] + _FFN_SPECS + ple_specs,
        scratch_shapes=[pltpu.VMEM((OCTETS, TM, LANES), F32)],
        name="s5_glu_ffn_ple", **common,
    )(mix, x, w_mix, b_mix, g, w_in, w_out, p, w_gate, w_proj)


def kernel(x, p, norm_g, ffn_w_in, ffn_w_out, attn_w_qkv, attn_w_o, attn_lam, attn_subln_g,
           rel_bias, ssm_lam_re, ssm_lam_im, ssm_log_dt, ssm_b_re, ssm_b_im, ssm_c_re, ssm_c_im,
           ssm_d, ssm_w_glu, ssm_b_glu, ple_w_proj, ple_w_gate):
    x = x.reshape(TOKENS, D_MODEL)
    p = p.reshape(DEPTH, TOKENS, PLE_DIM)
    bias = _bias_tiles(rel_bias)
    bf = lambda w: w.astype(BF16)
    for i in range(DEPTH):
        g = norm_g[i]
        j = i // N_MIXERS
        if i % N_MIXERS == 0:
            lambda_init = 0.8 - 0.6 * math.exp(-0.3 * i)
            w = bf(attn_w_qkv[j])
            x, qt, k, vt = _layer_head(x, g, bf(ffn_w_in[i, 0]), bf(ffn_w_out[i, 0]),
                                       (w[:, :D_MODEL].T, w[:, D_MODEL:2 * D_MODEL],
                                        w[:, 2 * D_MODEL:].T))
            mix = _attention(qt, k, vt, bias, attn_lam[j], attn_subln_g[j].reshape(1, -1),
                             lambda_init)
            w_mix, b_mix = bf(attn_w_o[j]), None
        else:
            bmat, cmat, pw_b, pw_c, a_chunk, d_oct = _ssm_operators(
                ssm_lam_re[j], ssm_lam_im[j], ssm_log_dt[j], ssm_b_re[j], ssm_b_im[j],
                ssm_c_re[j], ssm_c_im[j], ssm_d[j])
            x, u_oct = _layer_head(x, g, bf(ffn_w_in[i, 0]), bf(ffn_w_out[i, 0]))
            xs = _s5_states(u_oct, bmat, pw_b, a_chunk)
            mix = _s5_output(u_oct, xs, bmat, cmat, pw_c, d_oct)
            w_mix, b_mix = bf(ssm_w_glu[j]), ssm_b_glu[j].reshape(1, -1)
        x = _layer_tail(mix, x, p, i, g, w_mix, b_mix, bf(ffn_w_in[i, 1]), bf(ffn_w_out[i, 1]),
                        bf(ple_w_gate[i]), bf(ple_w_proj[i]))
    return x.reshape(BATCH, SEQ, D_MODEL)
```

```python
import functools
import math

import numpy as np
import jax
import jax.numpy as jnp
from jax import lax
from jax.experimental import pallas as pl
from jax.experimental.pallas import tpu as pltpu

D_MODEL = 1024
BATCH = 8
SEQ = 4096
DEPTH = 4
N_MIXERS = 2
HEAD_DIM = 64
N_HEADS = D_MODEL // (2 * HEAD_DIM)
REL_BUCKETS = 32
REL_MAX_DIST = 128
GROUP_CH = 16
GROUPS = D_MODEL // GROUP_CH
SSM_STATE = 64
D_FF = 2816
FFN_RESIDUAL = 0.5
PLE_DIM = 256
N_NORMS = 8
RMS_EPS = 1e-6
NEG_INF = -1e30

TOKENS = BATCH * SEQ
F32 = jnp.float32
BF16 = jnp.bfloat16
LANES = 128

V7X_VMEM_BYTES = 64 * 1024 * 1024
VMEM_LIMIT = V7X_VMEM_BYTES - 8 * 1024 * 1024

TM = 512
TILES_PER_SEQ = SEQ // TM
SUB_TILES = 2
SUB_TM = TM // SUB_TILES
MXU_TILE = 256
FF_SPLITS = (0, 6 * MXU_TILE, D_FF)
assert D_FF % MXU_TILE == 0
TQ = 512
TK = 512
BIAS_TILES = 3
TQ_HALF = TQ // 2
SUM_ROWS = 16
HEADS_PER_STEP = 8
EARLY = 2
LATE = 2
LOG2E = math.log2(math.e)

SSM_L = 16
SSM_NC = SEQ // SSM_L
OCTETS = D_MODEL // LANES
OCT_GROUPS = LANES // GROUP_CH
OCT_W = SSM_L * LANES
OCT_STATE = OCT_GROUPS * SSM_STATE
SSM_ROWS = BATCH * SSM_NC
SSM_RT = 512
CH_PER_TILE = TM // SSM_L
SUB_CH = SUB_TM // SSM_L
RE_COLS = OCT_STATE // LANES
STATE_COLS = 2 * RE_COLS


def _t5_thresholds():
    n = np.arange(0, 4 * REL_MAX_DIST)
    max_exact = REL_BUCKETS // 2
    nf = np.maximum(n, 1).astype(np.float64)
    large = max_exact + (np.log(nf / max_exact) / math.log(REL_MAX_DIST / max_exact)
                         * (REL_BUCKETS - max_exact)).astype(np.int32)
    bucket = np.where(n < max_exact, n, np.minimum(large, REL_BUCKETS - 1))
    return [int(np.argmax(bucket >= j)) for j in range(1, REL_BUCKETS)]


T5_THRESHOLDS = _t5_thresholds()
assert TQ == TK and T5_THRESHOLDS[-1] <= TK, "key blocks before i-1 sit in the last bucket"

_NT = (((1,), (1,)), ((), ()))


def _const_spec(shape):
    nd = len(shape)
    return pl.BlockSpec(shape, lambda *_: (0,) * nd, pipeline_mode=pl.Buffered(1))


def _params(n_axes=1):
    return pltpu.CompilerParams(dimension_semantics=("arbitrary",) * n_axes,
                                vmem_limit_bytes=VMEM_LIMIT)


def _rms(x, g):
    return x * lax.rsqrt(jnp.mean(x * x, axis=-1, keepdims=True) + RMS_EPS) * g


def _sigmoid(x):
    return 1.0 / (1.0 + jnp.exp(-x))


def _dot(a, b):
    return jnp.dot(a, b, preferred_element_type=F32)


_TOKEN_TILE = pl.BlockSpec((TM, D_MODEL), lambda b, c: (b * TILES_PER_SEQ + c, 0))
_FEATURE_TILE = pl.BlockSpec((D_MODEL, TM), lambda b, c: (0, b * TILES_PER_SEQ + c))
_SUPER_TILE = pl.BlockSpec((OCTETS, CH_PER_TILE, OCT_W),
                           lambda b, c: (0, b * TILES_PER_SEQ + c, 0))
_TOKEN_GRID = (BATCH, TILES_PER_SEQ)
_FFN_SPECS = [_const_spec((N_NORMS, D_MODEL)), _const_spec((D_MODEL, 2 * D_FF)),
              _const_spec((D_FF, D_MODEL))]


def _skewed(stage_gens):
    waiting, running = list(stage_gens), []
    while waiting or running:
        if waiting:
            running.append(waiting.pop(0))
        for gen in list(running):
            if next(gen, _DONE) is _DONE:
                running.remove(gen)


_DONE = object()


def _sub_rows(r):
    return pl.ds(r * SUB_TM, SUB_TM)


def _ffn_stages(x, g_pre, g_post, wi_ref, wo_ref, out):
    h = _rms(x, g_pre).astype(BF16)
    yield
    y = None
    for lo, hi in zip(FF_SPLITS[:-1], FF_SPLITS[1:]):
        a = _dot(h, wi_ref[:, lo:hi])
        u = _dot(h, wi_ref[:, D_FF + lo:D_FF + hi])
        yield
        act = (a * _sigmoid(a) * u).astype(BF16)
        yield
        down = _dot(act, wo_ref[lo:hi, :])
        y = down if y is None else y + down
        yield
    out.append(x + FFN_RESIDUAL * _rms(y, g_post))


def _to_super_rows(h, r, o_ref, h_scr):
    chunks = slice(r * SUB_CH, (r + 1) * SUB_CH)
    for j in range(OCTETS):
        h_scr[j, _sub_rows(r), :] = h[:, j * LANES:(j + 1) * LANES]
    for j in range(OCTETS):
        for s in range(SSM_L):
            rows = h_scr[j, pl.ds(r * SUB_TM + s, SUB_CH, stride=SSM_L), :]
            o_ref[j, chunks, s * LANES:(s + 1) * LANES] = rows.astype(BF16)


def _from_super_rows(y_ref, r, y_scr):
    chunks = slice(r * SUB_CH, (r + 1) * SUB_CH)
    for j in range(OCTETS):
        for s in range(SSM_L):
            y_scr[j, pl.ds(r * SUB_TM + s, SUB_CH, stride=SSM_L), :] = (
                y_ref[j, chunks, s * LANES:(s + 1) * LANES].astype(F32))
    return jnp.concatenate([y_scr[j, _sub_rows(r), :] for j in range(OCTETS)], axis=1)


def _head_stages(r, x_ref, g_ref, wi_ref, wo_ref, xo_ref, out):
    res = []
    yield from _ffn_stages(x_ref[_sub_rows(r), :], g_ref[0:1, :], g_ref[1:2, :], wi_ref, wo_ref,
                           res)
    xo_ref[_sub_rows(r), :] = res[0]
    out.append(_rms(res[0], g_ref[2:3, :]))
    yield


def _head_attn_kernel(x_ref, g_ref, wi_ref, wo_ref, wqt_ref, wk_ref, wvt_ref,
                      xo_ref, qt_ref, k_ref, vt_ref):
    def stages(r):
        res = []
        yield from _head_stages(r, x_ref, g_ref, wi_ref, wo_ref, xo_ref, res)
        h = res[0].astype(BF16)
        cols = _sub_rows(r)
        qt = lax.dot_general(wqt_ref[...], h, _NT, preferred_element_type=F32)
        qt_ref[:, cols] = (qt * (HEAD_DIM ** -0.5 * LOG2E)).astype(BF16)
        k_ref[cols, :] = _dot(h, wk_ref[...]).astype(BF16)
        vt = lax.dot_general(wvt_ref[...], h, _NT, preferred_element_type=F32)
        vt_ref[:, cols] = vt.astype(BF16)

    _skewed(stages(r) for r in range(SUB_TILES))


def _head_ssm_kernel(x_ref, g_ref, wi_ref, wo_ref, xo_ref, u_ref, h_scr):
    def stages(r):
        res = []
        yield from _head_stages(r, x_ref, g_ref, wi_ref, wo_ref, xo_ref, res)
        _to_super_rows(res[0], r, u_ref, h_scr)

    _skewed(stages(r) for r in range(SUB_TILES))


def _layer_head(x, g, w_in, w_out, qkv=None):
    x_out = jax.ShapeDtypeStruct((TOKENS, D_MODEL), F32)
    if qkv is None:
        return pl.pallas_call(
            _head_ssm_kernel,
            grid=_TOKEN_GRID,
            in_specs=[_TOKEN_TILE] + _FFN_SPECS,
            out_specs=(_TOKEN_TILE, _SUPER_TILE),
            out_shape=(x_out, jax.ShapeDtypeStruct((OCTETS, SSM_ROWS, OCT_W), BF16)),
            scratch_shapes=[pltpu.VMEM((OCTETS, TM, LANES), F32)],
            compiler_params=_params(2),
            name="ffn_s5in",
        )(x, g, w_in, w_out)
    feat = jax.ShapeDtypeStruct((D_MODEL, TOKENS), BF16)
    return pl.pallas_call(
        _head_attn_kernel,
        grid=_TOKEN_GRID,
        in_specs=[_TOKEN_TILE] + _FFN_SPECS + [_const_spec((D_MODEL, D_MODEL))] * 3,
        out_specs=(_TOKEN_TILE, _FEATURE_TILE, _TOKEN_TILE, _FEATURE_TILE),
        out_shape=(x_out, feat, jax.ShapeDtypeStruct((TOKENS, D_MODEL), BF16), feat),
        compiler_params=_params(2),
        name="ffn_qkv",
    )(x, g, w_in, w_out, *qkv)


def _bias_kernel(rel_ref, o_ref):
    c = pl.program_id(0)
    r = pl.program_id(1)
    @pl.when(r == BIAS_TILES - 1)
    def _():
        o_ref[...] = jnp.zeros((TK, TQ), BF16)

    @pl.when(r < BIAS_TILES - 1)
    def _():
        ki = lax.broadcasted_iota(jnp.int32, (TK, TQ), 0)
        qi = lax.broadcasted_iota(jnp.int32, (TK, TQ), 1)
        d = qi - ki + r * TK
        val = jnp.full((TK, TQ), rel_ref[0, c], F32)
        for j, thr in enumerate(T5_THRESHOLDS, start=1):
            val = jnp.where(d >= thr, rel_ref[j, c], val)
        val = (val - rel_ref[REL_BUCKETS - 1, c]) * LOG2E
        o_ref[...] = jnp.where(d >= 0, val, NEG_INF).astype(BF16)


def _bias_tiles(rel_bias):
    tiles = pl.pallas_call(
        _bias_kernel,
        grid=(2 * N_HEADS, BIAS_TILES),
        in_specs=[pl.BlockSpec(memory_space=pltpu.SMEM)],
        out_specs=pl.BlockSpec((None, None, TK, TQ), lambda c, r: (c, r, 0, 0)),
        out_shape=jax.ShapeDtypeStruct((2 * N_HEADS, BIAS_TILES, TK, TQ), BF16),
        compiler_params=_params(2),
        name="t5_bias",
    )(rel_bias)
    return tiles.reshape(N_HEADS, 2, BIAS_TILES, TK, TQ)


def _attn_kernel(qt_ref, k_ref, vt_ref, bias_ref, lam_ref, sg_ref, o_ref, m_ref, acc_ref,
                 s_buf, p_buf, a_buf, *, lambda_init):
    i = pl.program_id(2)
    d = 2 * HEAD_DIM
    chains = [(hd, mi, slice(hf * TQ_HALF, (hf + 1) * TQ_HALF))
              for hd in range(HEADS_PER_STEP) for hf in range(2) for mi in range(2)]
    n = len(chains)

    qt = qt_ref[...]
    row = lax.broadcasted_iota(jnp.int32, (d, TQ), 0)
    zero = jnp.zeros((d, TQ), BF16)
    q_maps = [[jnp.where(row < HEAD_DIM, qt[hd * d:(hd + 1) * d], zero),
               jnp.where(row >= HEAD_DIM, qt[hd * d:(hd + 1) * d], zero)]
              for hd in range(HEADS_PER_STEP)]

    m_ref[...] = jnp.full(m_ref.shape, NEG_INF, F32)
    acc_ref[...] = jnp.zeros(acc_ref.shape, F32)
    sum_rows = jnp.where(lax.broadcasted_iota(jnp.int32, (SUM_ROWS, TK), 0) == 0, 1.0, 0.0).astype(BF16)

    def keys(j):
        return k_ref[pl.ds(pl.multiple_of(j * TK, TK), TK), :]

    def vals(j):
        vb = vt_ref[:, pl.ds(pl.multiple_of(j * TK, TK), TK)]
        return [jnp.concatenate([vb[hd * d:(hd + 1) * d], sum_rows], axis=0)
                for hd in range(HEADS_PER_STEP)]

    def scores(kb, j, chain):
        hd, mi, cols = chain
        tile = jnp.clip(i - j, 0, BIAS_TILES - 1)
        s = _dot(kb[:, hd * d:(hd + 1) * d], q_maps[hd][mi][:, cols])
        return s.astype(BF16) + bias_ref[hd, mi, tile, :, cols]

    def softmax(chain, s):
        hd, mi, cols = chain
        m_prev = m_ref[hd, mi, :, cols]
        m_new = jnp.maximum(m_prev, jnp.max(s, axis=0, keepdims=True).astype(F32))
        alpha = jnp.exp2(m_prev - m_new)
        p = jnp.exp2(s - m_new.astype(BF16))
        m_ref[hd, mi, :, cols] = m_new
        return p, alpha

    def values(vb, chain, p, alpha):
        hd, mi, cols = chain
        acc_ref[hd, mi, :, cols] = alpha * acc_ref[hd, mi, :, cols] + _dot(vb[hd], p)

    kb0 = keys(0)
    for e in range(EARLY):
        s_buf[e] = scores(kb0, 0, chains[e])
    for t in range(LATE):
        p_buf[t] = jnp.zeros((TK, TQ_HALF), BF16)
        a_buf[t] = jnp.ones((1, TQ_HALF), F32)

    def trip(j, carry):
        kb, vb = keys(j), vals(j)
        j_next = jnp.minimum(j + 1, i)
        kb_next, vb_prev = keys(j_next), vals(jnp.maximum(j - 1, 0))
        s_tiles, p_tiles = {}, {}
        for slot in range(n):
            ahead = slot + EARLY
            if ahead < n:
                s_tiles[ahead] = scores(kb, j, chains[ahead])
            else:
                s_next = scores(kb_next, j_next, chains[ahead - n])
            s = s_buf[slot] if slot < EARLY else s_tiles.pop(slot)
            p_tiles[slot] = softmax(chains[slot], s)
            if ahead >= n:
                s_buf[ahead - n] = s_next
            behind = slot - LATE
            if behind >= 0:
                values(vb, chains[behind], *p_tiles.pop(behind))
            else:
                values(vb_prev, chains[n + behind], p_buf[slot], a_buf[slot])
            if slot >= n - LATE:
                p_buf[slot - (n - LATE)], a_buf[slot - (n - LATE)] = p_tiles.pop(slot)
        return carry

    lax.fori_loop(0, i + 1, trip, 0)
    vb_last = vals(i)
    for t in range(LATE):
        values(vb_last, chains[n - LATE + t], p_buf[t], a_buf[t])

    lv = lam_ref[...]
    lam = (jnp.exp(jnp.sum(lv[0:1] * lv[1:2], keepdims=True))
           - jnp.exp(jnp.sum(lv[2:3] * lv[3:4], keepdims=True)) + lambda_init)
    for hd in range(HEADS_PER_STEP):
        ot = (acc_ref[hd, 0, 0:d, :] * (1.0 / acc_ref[hd, 0, d:d + 1, :])
              - lam * (acc_ref[hd, 1, 0:d, :] * (1.0 / acc_ref[hd, 1, d:d + 1, :])))
        inv = lax.rsqrt(jnp.mean(ot * ot, axis=0, keepdims=True) + RMS_EPS) * (1.0 - lambda_init)
        o_ref[hd * d:(hd + 1) * d, :] = (ot * inv * sg_ref[...]).astype(BF16)


def _attention(qt, k, vt, bias, lam_vecs, subln_g, lambda_init):
    nq = SEQ // TQ
    wide = HEADS_PER_STEP * 2 * HEAD_DIM
    return pl.pallas_call(
        functools.partial(_attn_kernel, lambda_init=lambda_init),
        grid=(BATCH, N_HEADS // HEADS_PER_STEP, nq),
        in_specs=[pl.BlockSpec((wide, TQ), lambda b, h, i: (h, b * nq + i)),
                  pl.BlockSpec((SEQ, wide), lambda b, h, i: (b, h), pipeline_mode=pl.Buffered(1)),
                  pl.BlockSpec((wide, SEQ), lambda b, h, i: (h, b), pipeline_mode=pl.Buffered(1)),
                  pl.BlockSpec((HEADS_PER_STEP, 2, BIAS_TILES, TK, TQ),
                               lambda b, h, i: (h, 0, 0, 0, 0), pipeline_mode=pl.Buffered(1)),
                  pl.BlockSpec((4, HEAD_DIM), lambda b, h, i: (0, 0)),
                  pl.BlockSpec((2 * HEAD_DIM, 1), lambda b, h, i: (0, 0))],
        out_specs=pl.BlockSpec((wide, TQ), lambda b, h, i: (h, b * nq + i)),
        out_shape=jax.ShapeDtypeStruct((D_MODEL, TOKENS), BF16),
        scratch_shapes=[pltpu.VMEM((HEADS_PER_STEP, 2, 1, TQ), F32),
                        pltpu.VMEM((HEADS_PER_STEP, 2, 2 * HEAD_DIM + SUM_ROWS, TQ), F32),
                        pltpu.VMEM((EARLY, TK, TQ_HALF), BF16), pltpu.VMEM((LATE, TK, TQ_HALF), BF16),
                        pltpu.VMEM((LATE, 1, TQ_HALF), F32)],
        compiler_params=_params(3),
        name="diff_attn",
    )(qt, k, vt, bias, lam_vecs, subln_g)


def _ssm_operators(lam_re, lam_im, log_dt, b_re, b_im, c_re, c_im, d_skip):
    lam_re, lam_im, log_dt, b_re, b_im, c_re, c_im, d_skip = lax.optimization_barrier(
        (lam_re, lam_im, log_dt, b_re, b_im, c_re, c_im, d_skip))
    dt = jnp.exp(log_dt)[:, None]
    zr, zi = lam_re * dt, lam_im * dt
    ks = jnp.arange(SSM_L + 1, dtype=F32)[:, None, None]
    mag = jnp.exp(ks * zr)
    pr, pi = mag * jnp.cos(ks * zi), mag * jnp.sin(ks * zi)
    nr = jnp.expm1(zr) * jnp.cos(zi) - 2.0 * jnp.sin(0.5 * zi) ** 2
    ni = pi[1]
    den = lam_re * lam_re + lam_im * lam_im
    fr, fi = (nr * lam_re + ni * lam_im) / den, (ni * lam_re - nr * lam_im) / den
    bb_re = fr[..., None] * b_re - fi[..., None] * b_im
    bb_im = fr[..., None] * b_im + fi[..., None] * b_re
    same = jnp.eye(OCT_GROUPS, dtype=F32)

    def block_rows(t):
        t = lax.optimization_barrier(t)
        return (t[:, :, :, None, :] * same[None, :, None, :, None]).reshape(OCTETS, LANES, OCT_STATE)

    b_rows = lambda t: block_rows(t.reshape(OCTETS, OCT_GROUPS, SSM_STATE, GROUP_CH)
                                  .transpose(0, 1, 3, 2))
    c_rows = lambda t: block_rows(t.reshape(OCTETS, OCT_GROUPS, GROUP_CH, SSM_STATE))
    bmat = jnp.stack([b_rows(bb_re), b_rows(bb_im)], axis=1)
    cmat = jnp.stack([c_rows(c_re), c_rows(c_im)], axis=1)

    kr = jnp.arange(SSM_L - 1, -1, -1, dtype=F32)[:, None, None]
    mag_r = jnp.exp(kr * zr)
    per_oct = lambda t: t.reshape(t.shape[:-2] + (OCTETS, OCT_STATE))
    pw_b = per_oct(jnp.stack([mag_r * jnp.cos(kr * zi), mag_r * jnp.sin(kr * zi)]))
    pw_c = per_oct(jnp.stack([pr[1:], pi[1:]]))
    a_chunk = per_oct(jnp.stack([pr[SSM_L], pi[SSM_L]]))
    return (bmat, cmat, pw_b.transpose(2, 0, 1, 3), pw_c.transpose(2, 0, 1, 3),
            a_chunk.transpose(1, 0, 2), d_skip.reshape(OCTETS, 1, LANES))


def _scaled_blocks(mat_ref, pw_ref, out_ref, im_sign):
    def block(s, carry):
        m_re, m_im = mat_ref[0], mat_ref[1]
        p_re, p_im = pw_ref[0, pl.ds(s, 1), :], pw_ref[1, pl.ds(s, 1), :]
        rows = pl.ds(pl.multiple_of(s * LANES, LANES), LANES)
        out_ref[rows, 0:OCT_STATE] = (m_re * p_re - m_im * p_im).astype(BF16)
        out_ref[rows, OCT_STATE:2 * OCT_STATE] = (im_sign * (m_re * p_im + m_im * p_re)).astype(BF16)
        return carry

    lax.fori_loop(0, SSM_L, block, 0)


def _s5_state_kernel(u_ref, bmat_ref, pw_ref, a_ref, xs_ref, bpow_scr, s_scr, x_scr):
    _scaled_blocks(bmat_ref, pw_ref, bpow_scr, 1.0)
    for r in range(0, SSM_ROWS, SSM_RT):
        s_loc = _dot(u_ref[r:r + SSM_RT, :], bpow_scr[...])
        for k in range(STATE_COLS):
            s_scr[k, r:r + SSM_RT, :] = s_loc[:, k * LANES:(k + 1) * LANES]
    a = a_ref[...]
    col = lambda r, k: jnp.broadcast_to(a[r:r + 1, k * LANES:(k + 1) * LANES], (BATCH, LANES))
    ar = [col(0, k) for k in range(RE_COLS)]
    ai = [col(1, k) for k in range(RE_COLS)]

    def step(c, carry):
        rows = pl.ds(c, BATCH, stride=SSM_NC)
        nxt_re, nxt_im = [], []
        for k in range(RE_COLS):
            xr, xi = carry[k], carry[RE_COLS + k]
            x_scr[k, rows, :] = xr
            x_scr[RE_COLS + k, rows, :] = xi
            nxt_re.append(ar[k] * xr - ai[k] * xi + s_scr[k, rows, :])
            nxt_im.append(ar[k] * xi + ai[k] * xr + s_scr[RE_COLS + k, rows, :])
        return tuple(nxt_re + nxt_im)

    zeros = jnp.zeros((BATCH, LANES), F32)
    lax.fori_loop(0, SSM_NC, step, (zeros,) * STATE_COLS, unroll=4)
    for k in range(STATE_COLS):
        xs_ref[:, k * LANES:(k + 1) * LANES] = x_scr[k].astype(BF16)


def _s5_states(u_oct, bmat, pw_b, a_chunk):
    per_oct = lambda *tail: pl.BlockSpec((None,) + tail, lambda j: (j,) + (0,) * len(tail))
    return pl.pallas_call(
        _s5_state_kernel,
        grid=(OCTETS,),
        in_specs=[per_oct(SSM_ROWS, OCT_W), per_oct(2, LANES, OCT_STATE),
                  per_oct(2, SSM_L, OCT_STATE), per_oct(2, OCT_STATE)],
        out_specs=per_oct(SSM_ROWS, 2 * OCT_STATE),
        out_shape=jax.ShapeDtypeStruct((OCTETS, SSM_ROWS, 2 * OCT_STATE), BF16),
        scratch_shapes=[pltpu.VMEM((OCT_W, 2 * OCT_STATE), BF16),
                        pltpu.VMEM((STATE_COLS, SSM_ROWS, LANES), F32),
                        pltpu.VMEM((STATE_COLS, SSM_ROWS, LANES), F32)],
        compiler_params=_params(),
        name="s5_states",
    )(u_oct, bmat, pw_b, a_chunk)


def _s5_out_kernel(u_ref, xs_ref, bmat_ref, cmat_ref, pw_ref, d_ref, y_ref, toep_scr, cpow_scr):
    @pl.when(pl.program_id(1) == 0)
    def _():
        _scaled_blocks(cmat_ref, pw_ref, cpow_scr, -1.0)
        b_cat = jnp.concatenate([bmat_ref[0], bmat_ref[1]], axis=1).astype(BF16)
        c_tau0 = jnp.concatenate([cmat_ref[0], -cmat_ref[1]], axis=1).astype(BF16)
        c_all = jnp.concatenate([c_tau0, cpow_scr[0:OCT_W - LANES, :]], axis=0)
        imp = lax.dot_general(b_cat, c_all, _NT, preferred_element_type=F32)
        on_diag = (lax.broadcasted_iota(jnp.int32, (LANES, LANES), 0)
                   == lax.broadcasted_iota(jnp.int32, (LANES, LANES), 1))
        skip = jnp.where(on_diag, d_ref[...], 0.0)
        imp = jnp.concatenate([imp[:, 0:LANES] + skip, imp[:, LANES:]], axis=1).astype(BF16)
        toep_scr[...] = jnp.zeros(toep_scr.shape, BF16)
        for s in range(SSM_L):
            toep_scr[s * LANES:(s + 1) * LANES, s * LANES:] = imp[:, :OCT_W - s * LANES]

    u, xs = u_ref[...], xs_ref[...]
    for lo in range(0, OCT_W, MXU_TILE):
        hi = lo + MXU_TILE
        y = _dot(u[:, :hi], toep_scr[0:hi, lo:hi]) + lax.dot_general(
            xs, cpow_scr[lo:hi, :], _NT, preferred_element_type=F32)
        y_ref[:, lo:hi] = y.astype(BF16)


def _s5_output(u_oct, xs, bmat, cmat, pw_c, d_oct):
    rows = lambda w: pl.BlockSpec((None, SSM_RT, w), lambda j, r: (j, r, 0))
    per_oct = lambda *tail: pl.BlockSpec((None,) + tail, lambda j, r: (j,) + (0,) * len(tail))
    return pl.pallas_call(
        _s5_out_kernel,
        grid=(OCTETS, SSM_ROWS // SSM_RT),
        in_specs=[rows(OCT_W), rows(2 * OCT_STATE), per_oct(2, LANES, OCT_STATE),
                  per_oct(2, LANES, OCT_STATE), per_oct(2, SSM_L, OCT_STATE), per_oct(1, LANES)],
        out_specs=rows(OCT_W),
        out_shape=jax.ShapeDtypeStruct((OCTETS, SSM_ROWS, OCT_W), BF16),
        scratch_shapes=[pltpu.VMEM((OCT_W, OCT_W), BF16), pltpu.VMEM((OCT_W, 2 * OCT_STATE), BF16)],
        compiler_params=_params(2),
        name="s5_output",
    )(u_oct, xs, bmat, cmat, pw_c, d_oct)


def _gelu_tanh(x):
    return 0.5 * x * (1.0 + jnp.tanh(math.sqrt(2.0 / math.pi) * (x + 0.044715 * (x * x * x))))


def _tail_stages(r, x, mixed, g_ref, wi_ref, wo_ref, p_ref, wg_ref, wp_ref, o_ref):
    x = x + _rms(mixed, g_ref[3:4, :])
    res = []
    yield from _ffn_stages(x, g_ref[4:5, :], g_ref[5:6, :], wi_ref, wo_ref, res)
    x = res[0]
    h = _rms(x, g_ref[6:7, :]).astype(BF16)
    yield
    gate = _dot(h, wg_ref[...])
    emb = _dot(p_ref[_sub_rows(r), :].astype(BF16), wp_ref[...])
    yield
    o_ref[_sub_rows(r), :] = x + _rms(_sigmoid(gate) * emb, g_ref[7:8, :])


def _tail_attn_kernel(a_ref, x_ref, wm_ref, g_ref, wi_ref, wo_ref, p_ref, wg_ref, wp_ref, o_ref):
    def stages(r):
        mixed = lax.dot_general(a_ref[:, _sub_rows(r)], wm_ref[...], (((0,), (0,)), ((), ())),
                                preferred_element_type=F32)
        yield
        yield from _tail_stages(r, x_ref[_sub_rows(r), :], mixed, g_ref, wi_ref, wo_ref, p_ref,
                                wg_ref, wp_ref, o_ref)

    _skewed(stages(r) for r in range(SUB_TILES))


def _tail_ssm_kernel(y_ref, x_ref, wm_ref, bm_ref, g_ref, wi_ref, wo_ref, p_ref, wg_ref, wp_ref,
                     o_ref, y_scr):
    def stages(r):
        act = _gelu_tanh(_from_super_rows(y_ref, r, y_scr)).astype(BF16)
        yield
        z = _dot(act, wm_ref[...]) + bm_ref[...]
        yield
        mixed = z[:, :D_MODEL] * _sigmoid(z[:, D_MODEL:])
        yield from _tail_stages(r, x_ref[_sub_rows(r), :], mixed, g_ref, wi_ref, wo_ref, p_ref,
                                wg_ref, wp_ref, o_ref)

    _skewed(stages(r) for r in range(SUB_TILES))


def _layer_tail(mix, x, p, layer, g, w_mix, b_mix, w_in, w_out, w_gate, w_proj):
    ple_specs = [pl.BlockSpec((None, TM, PLE_DIM), lambda b, c: (layer, b * TILES_PER_SEQ + c, 0)),
                 _const_spec((D_MODEL, D_MODEL)), _const_spec((PLE_DIM, D_MODEL))]
    common = dict(grid=_TOKEN_GRID, out_specs=_TOKEN_TILE,
                  out_shape=jax.ShapeDtypeStruct((TOKENS, D_MODEL), F32),
                  compiler_params=_params(2))
    if b_mix is None:
        return pl.pallas_call(
            _tail_attn_kernel,
            in_specs=[_FEATURE_TILE, _TOKEN_TILE, _const_spec((D_MODEL, D_MODEL))] + _FFN_SPECS
            + ple_specs,
            name="attn_out_ffn_ple", **common,
        )(mix, x, w_mix, g, w_in, w_out, p, w_gate, w_proj)
    return pl.pallas_call(
        _tail_ssm_kernel,
        in_specs=[_SUPER_TILE, _TOKEN_TILE, _const_spec((D_MODEL, 2 * D_MODEL)),
                  _const_spec((1, 2 * D_MODEL))] + _FFN_SPECS + ple_specs,
        scratch_shapes=[pltpu.VMEM((OCTETS, TM, LANES), F32)],
        name="s5_glu_ffn_ple", **common,
    )(mix, x, w_mix, b_mix, g, w_in, w_out, p, w_gate, w_proj)


def kernel(x, p, norm_g, ffn_w_in, ffn_w_out, attn_w_qkv, attn_w_o, attn_lam, attn_subln_g,
           rel_bias, ssm_lam_re, ssm_lam_im, ssm_log_dt, ssm_b_re, ssm_b_im, ssm_c_re, ssm_c_im,
           ssm_d, ssm_w_glu, ssm_b_glu, ple_w_proj, ple_w_gate):
    x = x.reshape(TOKENS, D_MODEL)
    p = p.reshape(DEPTH, TOKENS, PLE_DIM)
    bias = _bias_tiles(rel_bias)
    bf = lambda w: w.astype(BF16)
    for i in range(DEPTH):
        g = norm_g[i]
        j = i // N_MIXERS
        if i % N_MIXERS == 0:
            lambda_init = 0.8 - 0.6 * math.exp(-0.3 * i)
            w = bf(attn_w_qkv[j])
            x, qt, k, vt = _layer_head(x, g, bf(ffn_w_in[i, 0]), bf(ffn_w_out[i, 0]),
                                       (w[:, :D_MODEL].T, w[:, D_MODEL:2 * D_MODEL],
                                        w[:, 2 * D_MODEL:].T))
            mix = _attention(qt, k, vt, bias, attn_lam[j], attn_subln_g[j].reshape(-1, 1),
                             lambda_init)
            w_mix, b_mix = bf(attn_w_o[j]), None
        else:
            bmat, cmat, pw_b, pw_c, a_chunk, d_oct = _ssm_operators(
                ssm_lam_re[j], ssm_lam_im[j], ssm_log_dt[j], ssm_b_re[j], ssm_b_im[j],
                ssm_c_re[j], ssm_c_im[j], ssm_d[j])
            x, u_oct = _layer_head(x, g, bf(ffn_w_in[i, 0]), bf(ffn_w_out[i, 0]))
            xs = _s5_states(u_oct, bmat, pw_b, a_chunk)
            mix = _s5_output(u_oct, xs, bmat, cmat, pw_c, d_oct)
            w_mix, b_mix = bf(ssm_w_glu[j]), ssm_b_glu[j].reshape(1, -1)
        x = _layer_tail(mix, x, p, i, g, w_mix, b_mix, bf(ffn_w_in[i, 1]), bf(ffn_w_out[i, 1]),
                        bf(ple_w_gate[i]), bf(ple_w_proj[i]))
    return x.reshape(BATCH, SEQ, D_MODEL)
```

```python
import functools
import math

import numpy as np
import jax
import jax.numpy as jnp
from jax import lax
from jax.experimental import pallas as pl
from jax.experimental.pallas import tpu as pltpu

D_MODEL = 1024
BATCH = 8
SEQ = 4096
DEPTH = 4
N_MIXERS = 2
HEAD_DIM = 64
N_HEADS = D_MODEL // (2 * HEAD_DIM)
REL_BUCKETS = 32
REL_MAX_DIST = 128
GROUP_CH = 16
GROUPS = D_MODEL // GROUP_CH
SSM_STATE = 64
D_FF = 2816
FFN_RESIDUAL = 0.5
PLE_DIM = 256
N_NORMS = 8
RMS_EPS = 1e-6
NEG_INF = -1e30

TOKENS = BATCH * SEQ
F32 = jnp.float32
BF16 = jnp.bfloat16
LANES = 128

V7X_VMEM_BYTES = 64 * 1024 * 1024
VMEM_LIMIT = V7X_VMEM_BYTES - 8 * 1024 * 1024

TM = 512
TILES_PER_SEQ = SEQ // TM
SUB_TILES = 2
SUB_TM = TM // SUB_TILES
MXU_TILE = 256
FF_SPLITS = (0, 6 * MXU_TILE, D_FF)
assert D_FF % MXU_TILE == 0
TQ = 512
TK = 512
BIAS_TILES = 3
TQ_HALF = TQ // 2
SUM_ROWS = 16
HEADS_PER_STEP = 8
EARLY = 4
LATE = 4
LOG2E = math.log2(math.e)

SSM_L = 16
SSM_NC = SEQ // SSM_L
OCTETS = D_MODEL // LANES
OCT_GROUPS = LANES // GROUP_CH
OCT_W = SSM_L * LANES
OCT_STATE = OCT_GROUPS * SSM_STATE
SSM_ROWS = BATCH * SSM_NC
SSM_RT = 512
CH_PER_TILE = TM // SSM_L
SUB_CH = SUB_TM // SSM_L
RE_COLS = OCT_STATE // LANES
STATE_COLS = 2 * RE_COLS


def _t5_thresholds():
    n = np.arange(0, 4 * REL_MAX_DIST)
    max_exact = REL_BUCKETS // 2
    nf = np.maximum(n, 1).astype(np.float64)
    large = max_exact + (np.log(nf / max_exact) / math.log(REL_MAX_DIST / max_exact)
                         * (REL_BUCKETS - max_exact)).astype(np.int32)
    bucket = np.where(n < max_exact, n, np.minimum(large, REL_BUCKETS - 1))
    return [int(np.argmax(bucket >= j)) for j in range(1, REL_BUCKETS)]


T5_THRESHOLDS = _t5_thresholds()
assert TQ == TK and T5_THRESHOLDS[-1] <= TK, "key blocks before i-1 sit in the last bucket"

_NT = (((1,), (1,)), ((), ()))


def _const_spec(shape):
    nd = len(shape)
    return pl.BlockSpec(shape, lambda *_: (0,) * nd, pipeline_mode=pl.Buffered(1))


def _params(n_axes=1):
    return pltpu.CompilerParams(dimension_semantics=("arbitrary",) * n_axes,
                                vmem_limit_bytes=VMEM_LIMIT)


def _rms(x, g):
    return x * lax.rsqrt(jnp.mean(x * x, axis=-1, keepdims=True) + RMS_EPS) * g


def _sigmoid(x):
    return 1.0 / (1.0 + jnp.exp(-x))


def _dot(a, b):
    return jnp.dot(a, b, preferred_element_type=F32)


_TOKEN_TILE = pl.BlockSpec((TM, D_MODEL), lambda b, c: (b * TILES_PER_SEQ + c, 0))
_FEATURE_TILE = pl.BlockSpec((D_MODEL, TM), lambda b, c: (0, b * TILES_PER_SEQ + c))
_SUPER_TILE = pl.BlockSpec((OCTETS, CH_PER_TILE, OCT_W),
                           lambda b, c: (0, b * TILES_PER_SEQ + c, 0))
_TOKEN_GRID = (BATCH, TILES_PER_SEQ)
_FFN_SPECS = [_const_spec((N_NORMS, D_MODEL)), _const_spec((D_MODEL, 2 * D_FF)),
              _const_spec((D_FF, D_MODEL))]


def _skewed(stage_gens):
    waiting, running = list(stage_gens), []
    while waiting or running:
        if waiting:
            running.append(waiting.pop(0))
        for gen in list(running):
            if next(gen, _DONE) is _DONE:
                running.remove(gen)


_DONE = object()


def _sub_rows(r):
    return pl.ds(r * SUB_TM, SUB_TM)


def _ffn_stages(x, g_pre, g_post, wi_ref, wo_ref, out):
    h = _rms(x, g_pre).astype(BF16)
    yield
    y = None
    for lo, hi in zip(FF_SPLITS[:-1], FF_SPLITS[1:]):
        a = _dot(h, wi_ref[:, lo:hi])
        u = _dot(h, wi_ref[:, D_FF + lo:D_FF + hi])
        yield
        act = (a * _sigmoid(a) * u).astype(BF16)
        yield
        down = _dot(act, wo_ref[lo:hi, :])
        y = down if y is None else y + down
        yield
    out.append(x + FFN_RESIDUAL * _rms(y, g_post))


def _to_super_rows(h, r, o_ref, h_scr):
    chunks = slice(r * SUB_CH, (r + 1) * SUB_CH)
    for j in range(OCTETS):
        h_scr[j, _sub_rows(r), :] = h[:, j * LANES:(j + 1) * LANES]
    for j in range(OCTETS):
        for s in range(SSM_L):
            rows = h_scr[j, pl.ds(r * SUB_TM + s, SUB_CH, stride=SSM_L), :]
            o_ref[j, chunks, s * LANES:(s + 1) * LANES] = rows.astype(BF16)


def _from_super_rows(y_ref, r, y_scr):
    chunks = slice(r * SUB_CH, (r + 1) * SUB_CH)
    for j in range(OCTETS):
        for s in range(SSM_L):
            y_scr[j, pl.ds(r * SUB_TM + s, SUB_CH, stride=SSM_L), :] = (
                y_ref[j, chunks, s * LANES:(s + 1) * LANES].astype(F32))
    return jnp.concatenate([y_scr[j, _sub_rows(r), :] for j in range(OCTETS)], axis=1)


def _head_stages(r, x_ref, g_ref, wi_ref, wo_ref, xo_ref, out):
    res = []
    yield from _ffn_stages(x_ref[_sub_rows(r), :], g_ref[0:1, :], g_ref[1:2, :], wi_ref, wo_ref,
                           res)
    xo_ref[_sub_rows(r), :] = res[0]
    out.append(_rms(res[0], g_ref[2:3, :]))
    yield


def _head_attn_kernel(x_ref, g_ref, wi_ref, wo_ref, wqt_ref, wk_ref, wvt_ref,
                      xo_ref, qt_ref, k_ref, vt_ref):
    def stages(r):
        res = []
        yield from _head_stages(r, x_ref, g_ref, wi_ref, wo_ref, xo_ref, res)
        h = res[0].astype(BF16)
        cols = _sub_rows(r)
        qt = lax.dot_general(wqt_ref[...], h, _NT, preferred_element_type=F32)
        qt_ref[:, cols] = (qt * (HEAD_DIM ** -0.5 * LOG2E)).astype(BF16)
        k_ref[cols, :] = _dot(h, wk_ref[...]).astype(BF16)
        vt = lax.dot_general(wvt_ref[...], h, _NT, preferred_element_type=F32)
        vt_ref[:, cols] = vt.astype(BF16)

    _skewed(stages(r) for r in range(SUB_TILES))


def _head_ssm_kernel(x_ref, g_ref, wi_ref, wo_ref, xo_ref, u_ref, h_scr):
    def stages(r):
        res = []
        yield from _head_stages(r, x_ref, g_ref, wi_ref, wo_ref, xo_ref, res)
        _to_super_rows(res[0], r, u_ref, h_scr)

    _skewed(stages(r) for r in range(SUB_TILES))


def _layer_head(x, g, w_in, w_out, qkv=None):
    x_out = jax.ShapeDtypeStruct((TOKENS, D_MODEL), F32)
    if qkv is None:
        return pl.pallas_call(
            _head_ssm_kernel,
            grid=_TOKEN_GRID,
            in_specs=[_TOKEN_TILE] + _FFN_SPECS,
            out_specs=(_TOKEN_TILE, _SUPER_TILE),
            out_shape=(x_out, jax.ShapeDtypeStruct((OCTETS, SSM_ROWS, OCT_W), BF16)),
            scratch_shapes=[pltpu.VMEM((OCTETS, TM, LANES), F32)],
            compiler_params=_params(2),
            name="ffn_s5in",
        )(x, g, w_in, w_out)
    feat = jax.ShapeDtypeStruct((D_MODEL, TOKENS), BF16)
    return pl.pallas_call(
        _head_attn_kernel,
        grid=_TOKEN_GRID,
        in_specs=[_TOKEN_TILE] + _FFN_SPECS + [_const_spec((D_MODEL, D_MODEL))] * 3,
        out_specs=(_TOKEN_TILE, _FEATURE_TILE, _TOKEN_TILE, _FEATURE_TILE),
        out_shape=(x_out, feat, jax.ShapeDtypeStruct((TOKENS, D_MODEL), BF16), feat),
        compiler_params=_params(2),
        name="ffn_qkv",
    )(x, g, w_in, w_out, *qkv)


def _bias_kernel(rel_ref, o_ref):
    c = pl.program_id(0)
    r = pl.program_id(1)
    @pl.when(r == BIAS_TILES - 1)
    def _():
        o_ref[...] = jnp.zeros((TK, TQ), BF16)

    @pl.when(r < BIAS_TILES - 1)
    def _():
        ki = lax.broadcasted_iota(jnp.int32, (TK, TQ), 0)
        qi = lax.broadcasted_iota(jnp.int32, (TK, TQ), 1)
        d = qi - ki + r * TK
        val = jnp.full((TK, TQ), rel_ref[0, c], F32)
        for j, thr in enumerate(T5_THRESHOLDS, start=1):
            val = jnp.where(d >= thr, rel_ref[j, c], val)
        val = (val - rel_ref[REL_BUCKETS - 1, c]) * LOG2E
        o_ref[...] = jnp.where(d >= 0, val, NEG_INF).astype(BF16)


def _bias_tiles(rel_bias):
    tiles = pl.pallas_call(
        _bias_kernel,
        grid=(2 * N_HEADS, BIAS_TILES),
        in_specs=[pl.BlockSpec(memory_space=pltpu.SMEM)],
        out_specs=pl.BlockSpec((None, None, TK, TQ), lambda c, r: (c, r, 0, 0)),
        out_shape=jax.ShapeDtypeStruct((2 * N_HEADS, BIAS_TILES, TK, TQ), BF16),
        compiler_params=_params(2),
        name="t5_bias",
    )(rel_bias)
    return tiles.reshape(N_HEADS, 2, BIAS_TILES, TK, TQ)


def _attn_kernel(qt_ref, k_ref, vt_ref, bias_ref, lam_ref, sg_ref, o_ref, m_ref, acc_ref,
                 s_buf, p_buf, a_buf, *, lambda_init):
    i = pl.program_id(2)
    d = 2 * HEAD_DIM
    chains = [(hd, mi, slice(hf * TQ_HALF, (hf + 1) * TQ_HALF))
              for hd in range(HEADS_PER_STEP) for hf in range(2) for mi in range(2)]
    n = len(chains)

    qt = qt_ref[...]
    row = lax.broadcasted_iota(jnp.int32, (d, TQ), 0)
    zero = jnp.zeros((d, TQ), BF16)
    q_maps = [[jnp.where(row < HEAD_DIM, qt[hd * d:(hd + 1) * d], zero),
               jnp.where(row >= HEAD_DIM, qt[hd * d:(hd + 1) * d], zero)]
              for hd in range(HEADS_PER_STEP)]

    m_ref[...] = jnp.full(m_ref.shape, NEG_INF, F32)
    acc_ref[...] = jnp.zeros(acc_ref.shape, F32)
    sum_rows = jnp.where(lax.broadcasted_iota(jnp.int32, (SUM_ROWS, TK), 0) == 0, 1.0, 0.0).astype(BF16)

    def keys(j):
        return k_ref[pl.ds(pl.multiple_of(j * TK, TK), TK), :]

    def vals(j):
        vb = vt_ref[:, pl.ds(pl.multiple_of(j * TK, TK), TK)]
        return [jnp.concatenate([vb[hd * d:(hd + 1) * d], sum_rows], axis=0)
                for hd in range(HEADS_PER_STEP)]

    def scores(kb, j, chain):
        hd, mi, cols = chain
        tile = jnp.clip(i - j, 0, BIAS_TILES - 1)
        s = _dot(kb[:, hd * d:(hd + 1) * d], q_maps[hd][mi][:, cols])
        return s.astype(BF16) + bias_ref[hd, mi, tile, :, cols]

    def softmax(chain, s):
        hd, mi, cols = chain
        m_prev = m_ref[hd, mi, :, cols]
        m_new = jnp.maximum(m_prev, jnp.max(s, axis=0, keepdims=True).astype(F32))
        alpha = jnp.exp2(m_prev - m_new)
        p = jnp.exp2(s - m_new.astype(BF16))
        m_ref[hd, mi, :, cols] = m_new
        return p, alpha

    def values(vb, chain, p, alpha):
        hd, mi, cols = chain
        acc_ref[hd, mi, :, cols] = alpha * acc_ref[hd, mi, :, cols] + _dot(vb[hd], p)

    kb0 = keys(0)
    for e in range(EARLY):
        s_buf[e] = scores(kb0, 0, chains[e])
    for t in range(LATE):
        p_buf[t] = jnp.zeros((TK, TQ_HALF), BF16)
        a_buf[t] = jnp.ones((1, TQ_HALF), F32)

    def trip(j, carry):
        kb, vb = keys(j), vals(j)
        j_next = jnp.minimum(j + 1, i)
        kb_next, vb_prev = keys(j_next), vals(jnp.maximum(j - 1, 0))
        s_tiles, p_tiles = {}, {}
        for slot in range(n):
            ahead = slot + EARLY
            if ahead < n:
                s_tiles[ahead] = scores(kb, j, chains[ahead])
            else:
                s_next = scores(kb_next, j_next, chains[ahead - n])
            s = s_buf[slot] if slot < EARLY else s_tiles.pop(slot)
            p_tiles[slot] = softmax(chains[slot], s)
            if ahead >= n:
                s_buf[ahead - n] = s_next
            behind = slot - LATE
            if behind >= 0:
                values(vb, chains[behind], *p_tiles.pop(behind))
            else:
                values(vb_prev, chains[n + behind], p_buf[slot], a_buf[slot])
            if slot >= n - LATE:
                p_buf[slot - (n - LATE)], a_buf[slot - (n - LATE)] = p_tiles.pop(slot)
        return carry

    lax.fori_loop(0, i + 1, trip, 0)
    vb_last = vals(i)
    for t in range(LATE):
        values(vb_last, chains[n - LATE + t], p_buf[t], a_buf[t])

    lv = lam_ref[...]
    lam = (jnp.exp(jnp.sum(lv[0:1] * lv[1:2], keepdims=True))
           - jnp.exp(jnp.sum(lv[2:3] * lv[3:4], keepdims=True)) + lambda_init)
    for hd in range(HEADS_PER_STEP):
        ot = (acc_ref[hd, 0, 0:d, :] * (1.0 / acc_ref[hd, 0, d:d + 1, :])
              - lam * (acc_ref[hd, 1, 0:d, :] * (1.0 / acc_ref[hd, 1, d:d + 1, :])))
        inv = lax.rsqrt(jnp.mean(ot * ot, axis=0, keepdims=True) + RMS_EPS) * (1.0 - lambda_init)
        o_ref[hd * d:(hd + 1) * d, :] = (ot * inv * sg_ref[...]).astype(BF16)


def _attention(qt, k, vt, bias, lam_vecs, subln_g, lambda_init):
    nq = SEQ // TQ
    wide = HEADS_PER_STEP * 2 * HEAD_DIM
    return pl.pallas_call(
        functools.partial(_attn_kernel, lambda_init=lambda_init),
        grid=(BATCH, N_HEADS // HEADS_PER_STEP, nq),
        in_specs=[pl.BlockSpec((wide, TQ), lambda b, h, i: (h, b * nq + i)),
                  pl.BlockSpec((SEQ, wide), lambda b, h, i: (b, h), pipeline_mode=pl.Buffered(1)),
                  pl.BlockSpec((wide, SEQ), lambda b, h, i: (h, b), pipeline_mode=pl.Buffered(1)),
                  pl.BlockSpec((HEADS_PER_STEP, 2, BIAS_TILES, TK, TQ),
                               lambda b, h, i: (h, 0, 0, 0, 0), pipeline_mode=pl.Buffered(1)),
                  pl.BlockSpec((4, HEAD_DIM), lambda b, h, i: (0, 0)),
                  pl.BlockSpec((2 * HEAD_DIM, 1), lambda b, h, i: (0, 0))],
        out_specs=pl.BlockSpec((wide, TQ), lambda b, h, i: (h, b * nq + i)),
        out_shape=jax.ShapeDtypeStruct((D_MODEL, TOKENS), BF16),
        scratch_shapes=[pltpu.VMEM((HEADS_PER_STEP, 2, 1, TQ), F32),
                        pltpu.VMEM((HEADS_PER_STEP, 2, 2 * HEAD_DIM + SUM_ROWS, TQ), F32),
                        pltpu.VMEM((EARLY, TK, TQ_HALF), BF16), pltpu.VMEM((LATE, TK, TQ_HALF), BF16),
                        pltpu.VMEM((LATE, 1, TQ_HALF), F32)],
        compiler_params=_params(3),
        name="diff_attn",
    )(qt, k, vt, bias, lam_vecs, subln_g)


def _ssm_operators(lam_re, lam_im, log_dt, b_re, b_im, c_re, c_im, d_skip):
    lam_re, lam_im, log_dt, b_re, b_im, c_re, c_im, d_skip = lax.optimization_barrier(
        (lam_re, lam_im, log_dt, b_re, b_im, c_re, c_im, d_skip))
    dt = jnp.exp(log_dt)[:, None]
    zr, zi = lam_re * dt, lam_im * dt
    ks = jnp.arange(SSM_L + 1, dtype=F32)[:, None, None]
    mag = jnp.exp(ks * zr)
    pr, pi = mag * jnp.cos(ks * zi), mag * jnp.sin(ks * zi)
    nr = jnp.expm1(zr) * jnp.cos(zi) - 2.0 * jnp.sin(0.5 * zi) ** 2
    ni = pi[1]
    den = lam_re * lam_re + lam_im * lam_im
    fr, fi = (nr * lam_re + ni * lam_im) / den, (ni * lam_re - nr * lam_im) / den
    bb_re = fr[..., None] * b_re - fi[..., None] * b_im
    bb_im = fr[..., None] * b_im + fi[..., None] * b_re
    same = jnp.eye(OCT_GROUPS, dtype=F32)

    def block_rows(t):
        t = lax.optimization_barrier(t)
        return (t[:, :, :, None, :] * same[None, :, None, :, None]).reshape(OCTETS, LANES, OCT_STATE)

    b_rows = lambda t: block_rows(t.reshape(OCTETS, OCT_GROUPS, SSM_STATE, GROUP_CH)
                                  .transpose(0, 1, 3, 2))
    c_rows = lambda t: block_rows(t.reshape(OCTETS, OCT_GROUPS, GROUP_CH, SSM_STATE))
    bmat = jnp.stack([b_rows(bb_re), b_rows(bb_im)], axis=1)
    cmat = jnp.stack([c_rows(c_re), c_rows(c_im)], axis=1)

    kr = jnp.arange(SSM_L - 1, -1, -1, dtype=F32)[:, None, None]
    mag_r = jnp.exp(kr * zr)
    per_oct = lambda t: t.reshape(t.shape[:-2] + (OCTETS, OCT_STATE))
    pw_b = per_oct(jnp.stack([mag_r * jnp.cos(kr * zi), mag_r * jnp.sin(kr * zi)]))
    pw_c = per_oct(jnp.stack([pr[1:], pi[1:]]))
    a_chunk = per_oct(jnp.stack([pr[SSM_L], pi[SSM_L]]))
    return (bmat, cmat, pw_b.transpose(2, 0, 1, 3), pw_c.transpose(2, 0, 1, 3),
            a_chunk.transpose(1, 0, 2), d_skip.reshape(OCTETS, 1, LANES))


def _scaled_blocks(mat_ref, pw_ref, out_ref, im_sign):
    def block(s, carry):
        m_re, m_im = mat_ref[0], mat_ref[1]
        p_re, p_im = pw_ref[0, pl.ds(s, 1), :], pw_ref[1, pl.ds(s, 1), :]
        rows = pl.ds(pl.multiple_of(s * LANES, LANES), LANES)
        out_ref[rows, 0:OCT_STATE] = (m_re * p_re - m_im * p_im).astype(BF16)
        out_ref[rows, OCT_STATE:2 * OCT_STATE] = (im_sign * (m_re * p_im + m_im * p_re)).astype(BF16)
        return carry

    lax.fori_loop(0, SSM_L, block, 0)


def _s5_state_kernel(u_ref, bmat_ref, pw_ref, a_ref, xs_ref, bpow_scr, s_scr, x_scr):
    _scaled_blocks(bmat_ref, pw_ref, bpow_scr, 1.0)
    for r in range(0, SSM_ROWS, SSM_RT):
        s_loc = _dot(u_ref[r:r + SSM_RT, :], bpow_scr[...])
        for k in range(STATE_COLS):
            s_scr[k, r:r + SSM_RT, :] = s_loc[:, k * LANES:(k + 1) * LANES]
    a = a_ref[...]
    col = lambda r, k: jnp.broadcast_to(a[r:r + 1, k * LANES:(k + 1) * LANES], (BATCH, LANES))
    ar = [col(0, k) for k in range(RE_COLS)]
    ai = [col(1, k) for k in range(RE_COLS)]

    def step(c, carry):
        rows = pl.ds(c, BATCH, stride=SSM_NC)
        nxt_re, nxt_im = [], []
        for k in range(RE_COLS):
            xr, xi = carry[k], carry[RE_COLS + k]
            x_scr[k, rows, :] = xr
            x_scr[RE_COLS + k, rows, :] = xi
            nxt_re.append(ar[k] * xr - ai[k] * xi + s_scr[k, rows, :])
            nxt_im.append(ar[k] * xi + ai[k] * xr + s_scr[RE_COLS + k, rows, :])
        return tuple(nxt_re + nxt_im)

    zeros = jnp.zeros((BATCH, LANES), F32)
    lax.fori_loop(0, SSM_NC, step, (zeros,) * STATE_COLS, unroll=8)
    for k in range(STATE_COLS):
        xs_ref[:, k * LANES:(k + 1) * LANES] = x_scr[k].astype(BF16)


def _s5_states(u_oct, bmat, pw_b, a_chunk):
    per_oct = lambda *tail: pl.BlockSpec((None,) + tail, lambda j: (j,) + (0,) * len(tail))
    return pl.pallas_call(
        _s5_state_kernel,
        grid=(OCTETS,),
        in_specs=[per_oct(SSM_ROWS, OCT_W), per_oct(2, LANES, OCT_STATE),
                  per_oct(2, SSM_L, OCT_STATE), per_oct(2, OCT_STATE)],
        out_specs=per_oct(SSM_ROWS, 2 * OCT_STATE),
        out_shape=jax.ShapeDtypeStruct((OCTETS, SSM_ROWS, 2 * OCT_STATE), BF16),
        scratch_shapes=[pltpu.VMEM((OCT_W, 2 * OCT_STATE), BF16),
                        pltpu.VMEM((STATE_COLS, SSM_ROWS, LANES), F32),
                        pltpu.VMEM((STATE_COLS, SSM_ROWS, LANES), F32)],
        compiler_params=_params(),
        name="s5_states",
    )(u_oct, bmat, pw_b, a_chunk)


def _s5_out_kernel(u_ref, xs_ref, bmat_ref, cmat_ref, pw_ref, d_ref, y_ref, toep_scr, cpow_scr):
    @pl.when(pl.program_id(1) == 0)
    def _():
        _scaled_blocks(cmat_ref, pw_ref, cpow_scr, -1.0)
        b_cat = jnp.concatenate([bmat_ref[0], bmat_ref[1]], axis=1).astype(BF16)
        c_tau0 = jnp.concatenate([cmat_ref[0], -cmat_ref[1]], axis=1).astype(BF16)
        c_all = jnp.concatenate([c_tau0, cpow_scr[0:OCT_W - LANES, :]], axis=0)
        imp = lax.dot_general(b_cat, c_all, _NT, preferred_element_type=F32)
        on_diag = (lax.broadcasted_iota(jnp.int32, (LANES, LANES), 0)
                   == lax.broadcasted_iota(jnp.int32, (LANES, LANES), 1))
        skip = jnp.where(on_diag, d_ref[...], 0.0)
        imp = jnp.concatenate([imp[:, 0:LANES] + skip, imp[:, LANES:]], axis=1).astype(BF16)
        toep_scr[...] = jnp.zeros(toep_scr.shape, BF16)
        for s in range(SSM_L):
            toep_scr[s * LANES:(s + 1) * LANES, s * LANES:] = imp[:, :OCT_W - s * LANES]

    u, xs = u_ref[...], xs_ref[...]
    for lo in range(0, OCT_W, MXU_TILE):
        hi = lo + MXU_TILE
        y = _dot(u[:, :hi], toep_scr[0:hi, lo:hi]) + lax.dot_general(
            xs, cpow_scr[lo:hi, :], _NT, preferred_element_type=F32)
        y_ref[:, lo:hi] = y.astype(BF16)


def _s5_output(u_oct, xs, bmat, cmat, pw_c, d_oct):
    rows = lambda w: pl.BlockSpec((None, SSM_RT, w), lambda j, r: (j, r, 0))
    per_oct = lambda *tail: pl.BlockSpec((None,) + tail, lambda j, r: (j,) + (0,) * len(tail))
    return pl.pallas_call(
        _s5_out_kernel,
        grid=(OCTETS, SSM_ROWS // SSM_RT),
        in_specs=[rows(OCT_W), rows(2 * OCT_STATE), per_oct(2, LANES, OCT_STATE),
                  per_oct(2, LANES, OCT_STATE), per_oct(2, SSM_L, OCT_STATE), per_oct(1, LANES)],
        out_specs=rows(OCT_W),
        out_shape=jax.ShapeDtypeStruct((OCTETS, SSM_ROWS, OCT_W), BF16),
        scratch_shapes=[pltpu.VMEM((OCT_W, OCT_W), BF16), pltpu.VMEM((OCT_W, 2 * OCT_STATE), BF16)],
        compiler_params=_params(2),
        name="s5_output",
    )(u_oct, xs, bmat, cmat, pw_c, d_oct)


def _gelu_tanh(x):
    return 0.5 * x * (1.0 + jnp.tanh(math.sqrt(2.0 / math.pi) * (x + 0.044715 * (x * x * x))))


def _tail_stages(r, x, mixed, g_ref, wi_ref, wo_ref, p_ref, wg_ref, wp_ref, o_ref):
    x = x + _rms(mixed, g_ref[3:4, :])
    res = []
    yield from _ffn_stages(x, g_ref[4:5, :], g_ref[5:6, :], wi_ref, wo_ref, res)
    x = res[0]
    h = _rms(x, g_ref[6:7, :]).astype(BF16)
    yield
    gate = _dot(h, wg_ref[...])
    emb = _dot(p_ref[_sub_rows(r), :].astype(BF16), wp_ref[...])
    yield
    o_ref[_sub_rows(r), :] = x + _rms(_sigmoid(gate) * emb, g_ref[7:8, :])


def _tail_attn_kernel(a_ref, x_ref, wm_ref, g_ref, wi_ref, wo_ref, p_ref, wg_ref, wp_ref, o_ref):
    def stages(r):
        mixed = lax.dot_general(a_ref[:, _sub_rows(r)], wm_ref[...], (((0,), (0,)), ((), ())),
                                preferred_element_type=F32)
        yield
        yield from _tail_stages(r, x_ref[_sub_rows(r), :], mixed, g_ref, wi_ref, wo_ref, p_ref,
                                wg_ref, wp_ref, o_ref)

    _skewed(stages(r) for r in range(SUB_TILES))


def _tail_ssm_kernel(y_ref, x_ref, wm_ref, bm_ref, g_ref, wi_ref, wo_ref, p_ref, wg_ref, wp_ref,
                     o_ref, y_scr):
    def stages(r):
        act = _gelu_tanh(_from_super_rows(y_ref, r, y_scr)).astype(BF16)
        yield
        z = _dot(act, wm_ref[...]) + bm_ref[...]
        yield
        mixed = z[:, :D_MODEL] * _sigmoid(z[:, D_MODEL:])
        yield from _tail_stages(r, x_ref[_sub_rows(r), :], mixed, g_ref, wi_ref, wo_ref, p_ref,
                                wg_ref, wp_ref, o_ref)

    _skewed(stages(r) for r in range(SUB_TILES))


def _layer_tail(mix, x, p, layer, g, w_mix, b_mix, w_in, w_out, w_gate, w_proj):
    ple_specs = [pl.BlockSpec((None, TM, PLE_DIM), lambda b, c: (layer, b * TILES_PER_SEQ + c, 0)),
                 _const_spec((D_MODEL, D_MODEL)), _const_spec((PLE_DIM, D_MODEL))]
    common = dict(grid=_TOKEN_GRID, out_specs=_TOKEN_TILE,
                  out_shape=jax.ShapeDtypeStruct((TOKENS, D_MODEL), F32),
                  compiler_params=_params(2))
    if b_mix is None:
        return pl.pallas_call(
            _tail_attn_kernel,
            in_specs=[_FEATURE_TILE, _TOKEN_TILE, _const_spec((D_MODEL, D_MODEL))] + _FFN_SPECS
            + ple_specs,
            name="attn_out_ffn_ple", **common,
        )(mix, x, w_mix, g, w_in, w_out, p, w_gate, w_proj)
    return pl.pallas_call(
        _tail_ssm_kernel,
        in_specs=[_SUPER_TILE, _TOKEN_TILE, _const_spec((D_MODEL, 2 * D_MODEL)),
                  _const_spec((1, 2 * D_MODEL))] + _FFN_SPECS + ple_specs,
        scratch_shapes=[pltpu.VMEM((OCTETS, TM, LANES), F32)],
        name="s5_glu_ffn_ple", **common,
    )(mix, x, w_mix, b_mix, g, w_in, w_out, p, w_gate, w_proj)


def kernel(x, p, norm_g, ffn_w_in, ffn_w_out, attn_w_qkv, attn_w_o, attn_lam, attn_subln_g,
           rel_bias, ssm_lam_re, ssm_lam_im, ssm_log_dt, ssm_b_re, ssm_b_im, ssm_c_re, ssm_c_im,
           ssm_d, ssm_w_glu, ssm_b_glu, ple_w_proj, ple_w_gate):
    x = x.reshape(TOKENS, D_MODEL)
    p = p.reshape(DEPTH, TOKENS, PLE_DIM)
    bias = _bias_tiles(rel_bias)
    bf = lambda w: w.astype(BF16)
    for i in range(DEPTH):
        g = norm_g[i]
        j = i // N_MIXERS
        if i % N_MIXERS == 0:
            lambda_init = 0.8 - 0.6 * math.exp(-0.3 * i)
            w = bf(attn_w_qkv[j])
            x, qt, k, vt = _layer_head(x, g, bf(ffn_w_in[i, 0]), bf(ffn_w_out[i, 0]),
                                       (w[:, :D_MODEL].T, w[:, D_MODEL:2 * D_MODEL],
                                        w[:, 2 * D_MODEL:].T))
            mix = _attention(qt, k, vt, bias, attn_lam[j], attn_subln_g[j].reshape(-1, 1),
                             lambda_init)
            w_mix, b_mix = bf(attn_w_o[j]), None
        else:
            bmat, cmat, pw_b, pw_c, a_chunk, d_oct = _ssm_operators(
                ssm_lam_re[j], ssm_lam_im[j], ssm_log_dt[j], ssm_b_re[j], ssm_b_im[j],
                ssm_c_re[j], ssm_c_im[j], ssm_d[j])
            x, u_oct = _layer_head(x, g, bf(ffn_w_in[i, 0]), bf(ffn_w_out[i, 0]))
            xs = _s5_states(u_oct, bmat, pw_b, a_chunk)
            mix = _s5_output(u_oct, xs, bmat, cmat, pw_c, d_oct)
            w_mix, b_mix = bf(ssm_w_glu[j]), ssm_b_glu[j].reshape(1, -1)
        x = _layer_tail(mix, x, p, i, g, w_mix, b_mix, bf(ffn_w_in[i, 1]), bf(ffn_w_out[i, 1]),
                        bf(ple_w_gate[i]), bf(ple_w_proj[i]))
    return x.reshape(BATCH, SEQ, D_MODEL)
```

```python
import functools
import math

import numpy as np
import jax
import jax.numpy as jnp
from jax import lax
from jax.experimental import pallas as pl
from jax.experimental.pallas import tpu as pltpu

D_MODEL = 1024
BATCH = 8
SEQ = 4096
DEPTH = 4
N_MIXERS = 2
HEAD_DIM = 64
N_HEADS = D_MODEL // (2 * HEAD_DIM)
REL_BUCKETS = 32
REL_MAX_DIST = 128
GROUP_CH = 16
GROUPS = D_MODEL // GROUP_CH
SSM_STATE = 64
D_FF = 2816
FFN_RESIDUAL = 0.5
PLE_DIM = 256
N_NORMS = 8
RMS_EPS = 1e-6
NEG_INF = -1e30

TOKENS = BATCH * SEQ
F32 = jnp.float32
BF16 = jnp.bfloat16
LANES = 128

V7X_VMEM_BYTES = 64 * 1024 * 1024
VMEM_LIMIT = V7X_VMEM_BYTES - 8 * 1024 * 1024

TM = 512
TILES_PER_SEQ = SEQ // TM
SUB_TILES = 2
SUB_TM = TM // SUB_TILES
MXU_TILE = 256
FF_SPLITS = (0, 6 * MXU_TILE, D_FF)
assert D_FF % MXU_TILE == 0
TQ = 512
TK = 512
BIAS_TILES = 3
TQ_HALF = TQ // 2
SUM_ROWS = 16
HEADS_PER_STEP = 8
EARLY = 1
LATE = 1
LOG2E = math.log2(math.e)

SSM_L = 16
SSM_NC = SEQ // SSM_L
OCTETS = D_MODEL // LANES
OCT_GROUPS = LANES // GROUP_CH
OCT_W = SSM_L * LANES
OCT_STATE = OCT_GROUPS * SSM_STATE
SSM_ROWS = BATCH * SSM_NC
SSM_RT = 512
CH_PER_TILE = TM // SSM_L
SUB_CH = SUB_TM // SSM_L
RE_COLS = OCT_STATE // LANES
STATE_COLS = 2 * RE_COLS


def _t5_thresholds():
    n = np.arange(0, 4 * REL_MAX_DIST)
    max_exact = REL_BUCKETS // 2
    nf = np.maximum(n, 1).astype(np.float64)
    large = max_exact + (np.log(nf / max_exact) / math.log(REL_MAX_DIST / max_exact)
                         * (REL_BUCKETS - max_exact)).astype(np.int32)
    bucket = np.where(n < max_exact, n, np.minimum(large, REL_BUCKETS - 1))
    return [int(np.argmax(bucket >= j)) for j in range(1, REL_BUCKETS)]


T5_THRESHOLDS = _t5_thresholds()
assert TQ == TK and T5_THRESHOLDS[-1] <= TK, "key blocks before i-1 sit in the last bucket"

_NT = (((1,), (1,)), ((), ()))


def _const_spec(shape):
    nd = len(shape)
    return pl.BlockSpec(shape, lambda *_: (0,) * nd, pipeline_mode=pl.Buffered(1))


def _params(n_axes=1):
    return pltpu.CompilerParams(dimension_semantics=("arbitrary",) * n_axes,
                                vmem_limit_bytes=VMEM_LIMIT)


def _rms(x, g):
    return x * lax.rsqrt(jnp.mean(x * x, axis=-1, keepdims=True) + RMS_EPS) * g


def _sigmoid(x):
    return 1.0 / (1.0 + jnp.exp(-x))


def _dot(a, b):
    return jnp.dot(a, b, preferred_element_type=F32)


_TOKEN_TILE = pl.BlockSpec((TM, D_MODEL), lambda b, c: (b * TILES_PER_SEQ + c, 0))
_FEATURE_TILE = pl.BlockSpec((D_MODEL, TM), lambda b, c: (0, b * TILES_PER_SEQ + c))
_SUPER_TILE = pl.BlockSpec((OCTETS, CH_PER_TILE, OCT_W),
                           lambda b, c: (0, b * TILES_PER_SEQ + c, 0))
_TOKEN_GRID = (BATCH, TILES_PER_SEQ)
_FFN_SPECS = [_const_spec((N_NORMS, D_MODEL)), _const_spec((D_MODEL, 2 * D_FF)),
              _const_spec((D_FF, D_MODEL))]


def _skewed(stage_gens):
    waiting, running = list(stage_gens), []
    while waiting or running:
        if waiting:
            running.append(waiting.pop(0))
        for gen in list(running):
            if next(gen, _DONE) is _DONE:
                running.remove(gen)


_DONE = object()


def _sub_rows(r):
    return pl.ds(r * SUB_TM, SUB_TM)


def _ffn_stages(x, g_pre, g_post, wi_ref, wo_ref, out):
    h = _rms(x, g_pre).astype(BF16)
    yield
    y = None
    for lo, hi in zip(FF_SPLITS[:-1], FF_SPLITS[1:]):
        a = _dot(h, wi_ref[:, lo:hi])
        u = _dot(h, wi_ref[:, D_FF + lo:D_FF + hi])
        yield
        act = (a * _sigmoid(a) * u).astype(BF16)
        yield
        down = _dot(act, wo_ref[lo:hi, :])
        y = down if y is None else y + down
        yield
    out.append(x + FFN_RESIDUAL * _rms(y, g_post))


def _to_super_rows(h, r, o_ref, h_scr):
    chunks = slice(r * SUB_CH, (r + 1) * SUB_CH)
    for j in range(OCTETS):
        h_scr[j, _sub_rows(r), :] = h[:, j * LANES:(j + 1) * LANES]
    for j in range(OCTETS):
        for s in range(SSM_L):
            rows = h_scr[j, pl.ds(r * SUB_TM + s, SUB_CH, stride=SSM_L), :]
            o_ref[j, chunks, s * LANES:(s + 1) * LANES] = rows.astype(BF16)


def _from_super_rows(y_ref, r, y_scr):
    chunks = slice(r * SUB_CH, (r + 1) * SUB_CH)
    for j in range(OCTETS):
        for s in range(SSM_L):
            y_scr[j, pl.ds(r * SUB_TM + s, SUB_CH, stride=SSM_L), :] = (
                y_ref[j, chunks, s * LANES:(s + 1) * LANES].astype(F32))
    return jnp.concatenate([y_scr[j, _sub_rows(r), :] for j in range(OCTETS)], axis=1)


def _head_stages(r, x_ref, g_ref, wi_ref, wo_ref, xo_ref, out):
    res = []
    yield from _ffn_stages(x_ref[_sub_rows(r), :], g_ref[0:1, :], g_ref[1:2, :], wi_ref, wo_ref,
                           res)
    xo_ref[_sub_rows(r), :] = res[0]
    out.append(_rms(res[0], g_ref[2:3, :]))
    yield


def _head_attn_kernel(x_ref, g_ref, wi_ref, wo_ref, wqt_ref, wk_ref, wvt_ref,
                      xo_ref, qt_ref, k_ref, vt_ref):
    def stages(r):
        res = []
        yield from _head_stages(r, x_ref, g_ref, wi_ref, wo_ref, xo_ref, res)
        h = res[0].astype(BF16)
        cols = _sub_rows(r)
        qt = lax.dot_general(wqt_ref[...], h, _NT, preferred_element_type=F32)
        qt_ref[:, cols] = (qt * (HEAD_DIM ** -0.5 * LOG2E)).astype(BF16)
        k_ref[cols, :] = _dot(h, wk_ref[...]).astype(BF16)
        vt = lax.dot_general(wvt_ref[...], h, _NT, preferred_element_type=F32)
        vt_ref[:, cols] = vt.astype(BF16)

    _skewed(stages(r) for r in range(SUB_TILES))


def _head_ssm_kernel(x_ref, g_ref, wi_ref, wo_ref, xo_ref, u_ref, h_scr):
    def stages(r):
        res = []
        yield from _head_stages(r, x_ref, g_ref, wi_ref, wo_ref, xo_ref, res)
        _to_super_rows(res[0], r, u_ref, h_scr)

    _skewed(stages(r) for r in range(SUB_TILES))


def _layer_head(x, g, w_in, w_out, qkv=None):
    x_out = jax.ShapeDtypeStruct((TOKENS, D_MODEL), F32)
    if qkv is None:
        return pl.pallas_call(
            _head_ssm_kernel,
            grid=_TOKEN_GRID,
            in_specs=[_TOKEN_TILE] + _FFN_SPECS,
            out_specs=(_TOKEN_TILE, _SUPER_TILE),
            out_shape=(x_out, jax.ShapeDtypeStruct((OCTETS, SSM_ROWS, OCT_W), BF16)),
            scratch_shapes=[pltpu.VMEM((OCTETS, TM, LANES), F32)],
            compiler_params=_params(2),
            name="ffn_s5in",
        )(x, g, w_in, w_out)
    feat = jax.ShapeDtypeStruct((D_MODEL, TOKENS), BF16)
    return pl.pallas_call(
        _head_attn_kernel,
        grid=_TOKEN_GRID,
        in_specs=[_TOKEN_TILE] + _FFN_SPECS + [_const_spec((D_MODEL, D_MODEL))] * 3,
        out_specs=(_TOKEN_TILE, _FEATURE_TILE, _TOKEN_TILE, _FEATURE_TILE),
        out_shape=(x_out, feat, jax.ShapeDtypeStruct((TOKENS, D_MODEL), BF16), feat),
        compiler_params=_params(2),
        name="ffn_qkv",
    )(x, g, w_in, w_out, *qkv)


def _bias_kernel(rel_ref, o_ref):
    c = pl.program_id(0)
    r = pl.program_id(1)
    @pl.when(r == BIAS_TILES - 1)
    def _():
        o_ref[...] = jnp.zeros((TK, TQ), BF16)

    @pl.when(r < BIAS_TILES - 1)
    def _():
        ki = lax.broadcasted_iota(jnp.int32, (TK, TQ), 0)
        qi = lax.broadcasted_iota(jnp.int32, (TK, TQ), 1)
        d = qi - ki + r * TK
        val = jnp.full((TK, TQ), rel_ref[0, c], F32)
        for j, thr in enumerate(T5_THRESHOLDS, start=1):
            val = jnp.where(d >= thr, rel_ref[j, c], val)
        val = (val - rel_ref[REL_BUCKETS - 1, c]) * LOG2E
        o_ref[...] = jnp.where(d >= 0, val, NEG_INF).astype(BF16)


def _bias_tiles(rel_bias):
    tiles = pl.pallas_call(
        _bias_kernel,
        grid=(2 * N_HEADS, BIAS_TILES),
        in_specs=[pl.BlockSpec(memory_space=pltpu.SMEM)],
        out_specs=pl.BlockSpec((None, None, TK, TQ), lambda c, r: (c, r, 0, 0)),
        out_shape=jax.ShapeDtypeStruct((2 * N_HEADS, BIAS_TILES, TK, TQ), BF16),
        compiler_params=_params(2),
        name="t5_bias",
    )(rel_bias)
    return tiles.reshape(N_HEADS, 2, BIAS_TILES, TK, TQ)


def _attn_kernel(qt_ref, k_ref, vt_ref, bias_ref, lam_ref, sg_ref, o_ref, m_ref, acc_ref,
                 s_buf, p_buf, a_buf, *, lambda_init):
    i = pl.program_id(2)
    d = 2 * HEAD_DIM
    chains = [(hd, mi, slice(hf * TQ_HALF, (hf + 1) * TQ_HALF))
              for hd in range(HEADS_PER_STEP) for hf in range(2) for mi in range(2)]
    n = len(chains)

    qt = qt_ref[...]
    row = lax.broadcasted_iota(jnp.int32, (d, TQ), 0)
    zero = jnp.zeros((d, TQ), BF16)
    q_maps = [[jnp.where(row < HEAD_DIM, qt[hd * d:(hd + 1) * d], zero),
               jnp.where(row >= HEAD_DIM, qt[hd * d:(hd + 1) * d], zero)]
              for hd in range(HEADS_PER_STEP)]

    m_ref[...] = jnp.full(m_ref.shape, NEG_INF, F32)
    acc_ref[...] = jnp.zeros(acc_ref.shape, F32)
    sum_rows = jnp.where(lax.broadcasted_iota(jnp.int32, (SUM_ROWS, TK), 0) == 0, 1.0, 0.0).astype(BF16)

    def keys(j):
        return k_ref[pl.ds(pl.multiple_of(j * TK, TK), TK), :]

    def vals(j):
        vb = vt_ref[:, pl.ds(pl.multiple_of(j * TK, TK), TK)]
        return [jnp.concatenate([vb[hd * d:(hd + 1) * d], sum_rows], axis=0)
                for hd in range(HEADS_PER_STEP)]

    def scores(kb, j, chain):
        hd, mi, cols = chain
        tile = jnp.clip(i - j, 0, BIAS_TILES - 1)
        s = _dot(kb[:, hd * d:(hd + 1) * d], q_maps[hd][mi][:, cols])
        return s.astype(BF16) + bias_ref[hd, mi, tile, :, cols]

    def softmax(chain, s):
        hd, mi, cols = chain
        m_prev = m_ref[hd, mi, :, cols]
        m_new = jnp.maximum(m_prev, jnp.max(s, axis=0, keepdims=True).astype(F32))
        alpha = jnp.exp2(m_prev - m_new)
        p = jnp.exp2(s - m_new.astype(BF16))
        m_ref[hd, mi, :, cols] = m_new
        return p, alpha

    def values(vb, chain, p, alpha):
        hd, mi, cols = chain
        acc_ref[hd, mi, :, cols] = alpha * acc_ref[hd, mi, :, cols] + _dot(vb[hd], p)

    kb0 = keys(0)
    for e in range(EARLY):
        s_buf[e] = scores(kb0, 0, chains[e])
    for t in range(LATE):
        p_buf[t] = jnp.zeros((TK, TQ_HALF), BF16)
        a_buf[t] = jnp.ones((1, TQ_HALF), F32)

    def trip(j, carry):
        kb, vb = keys(j), vals(j)
        j_next = jnp.minimum(j + 1, i)
        kb_next, vb_prev = keys(j_next), vals(jnp.maximum(j - 1, 0))
        s_tiles, p_tiles = {}, {}
        for slot in range(n):
            ahead = slot + EARLY
            if ahead < n:
                s_tiles[ahead] = scores(kb, j, chains[ahead])
            else:
                s_next = scores(kb_next, j_next, chains[ahead - n])
            s = s_buf[slot] if slot < EARLY else s_tiles.pop(slot)
            p_tiles[slot] = softmax(chains[slot], s)
            if ahead >= n:
                s_buf[ahead - n] = s_next
            behind = slot - LATE
            if behind >= 0:
                values(vb, chains[behind], *p_tiles.pop(behind))
            else:
                values(vb_prev, chains[n + behind], p_buf[slot], a_buf[slot])
            if slot >= n - LATE:
                p_buf[slot - (n - LATE)], a_buf[slot - (n - LATE)] = p_tiles.pop(slot)
        return carry

    lax.fori_loop(0, i + 1, trip, 0)
    vb_last = vals(i)
    for t in range(LATE):
        values(vb_last, chains[n - LATE + t], p_buf[t], a_buf[t])

    lv = lam_ref[...]
    lam = (jnp.exp(jnp.sum(lv[0:1] * lv[1:2], keepdims=True))
           - jnp.exp(jnp.sum(lv[2:3] * lv[3:4], keepdims=True)) + lambda_init)
    for hd in range(HEADS_PER_STEP):
        ot = (acc_ref[hd, 0, 0:d, :] * (1.0 / acc_ref[hd, 0, d:d + 1, :])
              - lam * (acc_ref[hd, 1, 0:d, :] * (1.0 / acc_ref[hd, 1, d:d + 1, :])))
        inv = lax.rsqrt(jnp.mean(ot * ot, axis=0, keepdims=True) + RMS_EPS) * (1.0 - lambda_init)
        o_ref[hd * d:(hd + 1) * d, :] = (ot * inv * sg_ref[...]).astype(BF16)


def _attention(qt, k, vt, bias, lam_vecs, subln_g, lambda_init):
    nq = SEQ // TQ
    wide = HEADS_PER_STEP * 2 * HEAD_DIM
    return pl.pallas_call(
        functools.partial(_attn_kernel, lambda_init=lambda_init),
        grid=(BATCH, N_HEADS // HEADS_PER_STEP, nq),
        in_specs=[pl.BlockSpec((wide, TQ), lambda b, h, i: (h, b * nq + i)),
                  pl.BlockSpec((SEQ, wide), lambda b, h, i: (b, h), pipeline_mode=pl.Buffered(1)),
                  pl.BlockSpec((wide, SEQ), lambda b, h, i: (h, b), pipeline_mode=pl.Buffered(1)),
                  pl.BlockSpec((HEADS_PER_STEP, 2, BIAS_TILES, TK, TQ),
                               lambda b, h, i: (h, 0, 0, 0, 0), pipeline_mode=pl.Buffered(1)),
                  pl.BlockSpec((4, HEAD_DIM), lambda b, h, i: (0, 0)),
                  pl.BlockSpec((2 * HEAD_DIM, 1), lambda b, h, i: (0, 0))],
        out_specs=pl.BlockSpec((wide, TQ), lambda b, h, i: (h, b * nq + i)),
        out_shape=jax.ShapeDtypeStruct((D_MODEL, TOKENS), BF16),
        scratch_shapes=[pltpu.VMEM((HEADS_PER_STEP, 2, 1, TQ), F32),
                        pltpu.VMEM((HEADS_PER_STEP, 2, 2 * HEAD_DIM + SUM_ROWS, TQ), F32),
                        pltpu.VMEM((EARLY, TK, TQ_HALF), BF16), pltpu.VMEM((LATE, TK, TQ_HALF), BF16),
                        pltpu.VMEM((LATE, 1, TQ_HALF), F32)],
        compiler_params=_params(3),
        name="diff_attn",
    )(qt, k, vt, bias, lam_vecs, subln_g)


def _ssm_operators(lam_re, lam_im, log_dt, b_re, b_im, c_re, c_im, d_skip):
    lam_re, lam_im, log_dt, b_re, b_im, c_re, c_im, d_skip = lax.optimization_barrier(
        (lam_re, lam_im, log_dt, b_re, b_im, c_re, c_im, d_skip))
    dt = jnp.exp(log_dt)[:, None]
    zr, zi = lam_re * dt, lam_im * dt
    ks = jnp.arange(SSM_L + 1, dtype=F32)[:, None, None]
    mag = jnp.exp(ks * zr)
    pr, pi = mag * jnp.cos(ks * zi), mag * jnp.sin(ks * zi)
    nr = jnp.expm1(zr) * jnp.cos(zi) - 2.0 * jnp.sin(0.5 * zi) ** 2
    ni = pi[1]
    den = lam_re * lam_re + lam_im * lam_im
    fr, fi = (nr * lam_re + ni * lam_im) / den, (ni * lam_re - nr * lam_im) / den
    bb_re = fr[..., None] * b_re - fi[..., None] * b_im
    bb_im = fr[..., None] * b_im + fi[..., None] * b_re
    same = jnp.eye(OCT_GROUPS, dtype=F32)

    def block_rows(t):
        t = lax.optimization_barrier(t)
        return (t[:, :, :, None, :] * same[None, :, None, :, None]).reshape(OCTETS, LANES, OCT_STATE)

    b_rows = lambda t: block_rows(t.reshape(OCTETS, OCT_GROUPS, SSM_STATE, GROUP_CH)
                                  .transpose(0, 1, 3, 2))
    c_rows = lambda t: block_rows(t.reshape(OCTETS, OCT_GROUPS, GROUP_CH, SSM_STATE))
    bmat = jnp.stack([b_rows(bb_re), b_rows(bb_im)], axis=1)
    cmat = jnp.stack([c_rows(c_re), c_rows(c_im)], axis=1)

    kr = jnp.arange(SSM_L - 1, -1, -1, dtype=F32)[:, None, None]
    mag_r = jnp.exp(kr * zr)
    per_oct = lambda t: t.reshape(t.shape[:-2] + (OCTETS, OCT_STATE))
    pw_b = per_oct(jnp.stack([mag_r * jnp.cos(kr * zi), mag_r * jnp.sin(kr * zi)]))
    pw_c = per_oct(jnp.stack([pr[1:], pi[1:]]))
    a_chunk = per_oct(jnp.stack([pr[SSM_L], pi[SSM_L]]))
    return (bmat, cmat, pw_b.transpose(2, 0, 1, 3), pw_c.transpose(2, 0, 1, 3),
            a_chunk.transpose(1, 0, 2), d_skip.reshape(OCTETS, 1, LANES))


def _scaled_blocks(mat_ref, pw_ref, out_ref, im_sign):
    def block(s, carry):
        m_re, m_im = mat_ref[0], mat_ref[1]
        p_re, p_im = pw_ref[0, pl.ds(s, 1), :], pw_ref[1, pl.ds(s, 1), :]
        rows = pl.ds(pl.multiple_of(s * LANES, LANES), LANES)
        out_ref[rows, 0:OCT_STATE] = (m_re * p_re - m_im * p_im).astype(BF16)
        out_ref[rows, OCT_STATE:2 * OCT_STATE] = (im_sign * (m_re * p_im + m_im * p_re)).astype(BF16)
        return carry

    lax.fori_loop(0, SSM_L, block, 0)


def _s5_state_kernel(u_ref, bmat_ref, pw_ref, a_ref, xs_ref, bpow_scr, s_scr, x_scr):
    _scaled_blocks(bmat_ref, pw_ref, bpow_scr, 1.0)
    for r in range(0, SSM_ROWS, SSM_RT):
        s_loc = _dot(u_ref[r:r + SSM_RT, :], bpow_scr[...])
        for k in range(STATE_COLS):
            s_scr[k, r:r + SSM_RT, :] = s_loc[:, k * LANES:(k + 1) * LANES]
    a = a_ref[...]
    col = lambda r, k: jnp.broadcast_to(a[r:r + 1, k * LANES:(k + 1) * LANES], (BATCH, LANES))
    ar = [col(0, k) for k in range(RE_COLS)]
    ai = [col(1, k) for k in range(RE_COLS)]

    def step(c, carry):
        rows = pl.ds(c, BATCH, stride=SSM_NC)
        nxt_re, nxt_im = [], []
        for k in range(RE_COLS):
            xr, xi = carry[k], carry[RE_COLS + k]
            x_scr[k, rows, :] = xr
            x_scr[RE_COLS + k, rows, :] = xi
            nxt_re.append(ar[k] * xr - ai[k] * xi + s_scr[k, rows, :])
            nxt_im.append(ar[k] * xi + ai[k] * xr + s_scr[RE_COLS + k, rows, :])
        return tuple(nxt_re + nxt_im)

    zeros = jnp.zeros((BATCH, LANES), F32)
    lax.fori_loop(0, SSM_NC, step, (zeros,) * STATE_COLS, unroll=8)
    for k in range(STATE_COLS):
        xs_ref[:, k * LANES:(k + 1) * LANES] = x_scr[k].astype(BF16)


def _s5_states(u_oct, bmat, pw_b, a_chunk):
    per_oct = lambda *tail: pl.BlockSpec((None,) + tail, lambda j: (j,) + (0,) * len(tail))
    return pl.pallas_call(
        _s5_state_kernel,
        grid=(OCTETS,),
        in_specs=[per_oct(SSM_ROWS, OCT_W), per_oct(2, LANES, OCT_STATE),
                  per_oct(2, SSM_L, OCT_STATE), per_oct(2, OCT_STATE)],
        out_specs=per_oct(SSM_ROWS, 2 * OCT_STATE),
        out_shape=jax.ShapeDtypeStruct((OCTETS, SSM_ROWS, 2 * OCT_STATE), BF16),
        scratch_shapes=[pltpu.VMEM((OCT_W, 2 * OCT_STATE), BF16),
                        pltpu.VMEM((STATE_COLS, SSM_ROWS, LANES), F32),
                        pltpu.VMEM((STATE_COLS, SSM_ROWS, LANES), F32)],
        compiler_params=_params(),
        name="s5_states",
    )(u_oct, bmat, pw_b, a_chunk)


def _s5_out_kernel(u_ref, xs_ref, bmat_ref, cmat_ref, pw_ref, d_ref, y_ref, toep_scr, cpow_scr):
    @pl.when(pl.program_id(1) == 0)
    def _():
        _scaled_blocks(cmat_ref, pw_ref, cpow_scr, -1.0)
        b_cat = jnp.concatenate([bmat_ref[0], bmat_ref[1]], axis=1).astype(BF16)
        c_tau0 = jnp.concatenate([cmat_ref[0], -cmat_ref[1]], axis=1).astype(BF16)
        c_all = jnp.concatenate([c_tau0, cpow_scr[0:OCT_W - LANES, :]], axis=0)
        imp = lax.dot_general(b_cat, c_all, _NT, preferred_element_type=F32)
        on_diag = (lax.broadcasted_iota(jnp.int32, (LANES, LANES), 0)
                   == lax.broadcasted_iota(jnp.int32, (LANES, LANES), 1))
        skip = jnp.where(on_diag, d_ref[...], 0.0)
        imp = jnp.concatenate([imp[:, 0:LANES] + skip, imp[:, LANES:]], axis=1).astype(BF16)
        toep_scr[...] = jnp.zeros(toep_scr.shape, BF16)
        for s in range(SSM_L):
            toep_scr[s * LANES:(s + 1) * LANES, s * LANES:] = imp[:, :OCT_W - s * LANES]

    u, xs = u_ref[...], xs_ref[...]
    for lo in range(0, OCT_W, MXU_TILE):
        hi = lo + MXU_TILE
        y = _dot(u[:, :hi], toep_scr[0:hi, lo:hi]) + lax.dot_general(
            xs, cpow_scr[lo:hi, :], _NT, preferred_element_type=F32)
        y_ref[:, lo:hi] = y.astype(BF16)


def _s5_output(u_oct, xs, bmat, cmat, pw_c, d_oct):
    rows = lambda w: pl.BlockSpec((None, SSM_RT, w), lambda j, r: (j, r, 0))
    per_oct = lambda *tail: pl.BlockSpec((None,) + tail, lambda j, r: (j,) + (0,) * len(tail))
    return pl.pallas_call(
        _s5_out_kernel,
        grid=(OCTETS, SSM_ROWS // SSM_RT),
        in_specs=[rows(OCT_W), rows(2 * OCT_STATE), per_oct(2, LANES, OCT_STATE),
                  per_oct(2, LANES, OCT_STATE), per_oct(2, SSM_L, OCT_STATE), per_oct(1, LANES)],
        out_specs=rows(OCT_W),
        out_shape=jax.ShapeDtypeStruct((OCTETS, SSM_ROWS, OCT_W), BF16),
        scratch_shapes=[pltpu.VMEM((OCT_W, OCT_W), BF16), pltpu.VMEM((OCT_W, 2 * OCT_STATE), BF16)],
        compiler_params=_params(2),
        name="s5_output",
    )(u_oct, xs, bmat, cmat, pw_c, d_oct)


def _gelu_tanh(x):
    return 0.5 * x * (1.0 + jnp.tanh(math.sqrt(2.0 / math.pi) * (x + 0.044715 * (x * x * x))))


def _tail_stages(r, x, mixed, g_ref, wi_ref, wo_ref, p_ref, wg_ref, wp_ref, o_ref):
    x = x + _rms(mixed, g_ref[3:4, :])
    res = []
    yield from _ffn_stages(x, g_ref[4:5, :], g_ref[5:6, :], wi_ref, wo_ref, res)
    x = res[0]
    h = _rms(x, g_ref[6:7, :]).astype(BF16)
    yield
    gate = _dot(h, wg_ref[...])
    emb = _dot(p_ref[_sub_rows(r), :].astype(BF16), wp_ref[...])
    yield
    o_ref[_sub_rows(r), :] = x + _rms(_sigmoid(gate) * emb, g_ref[7:8, :])


def _tail_attn_kernel(a_ref, x_ref, wm_ref, g_ref, wi_ref, wo_ref, p_ref, wg_ref, wp_ref, o_ref):
    def stages(r):
        mixed = lax.dot_general(a_ref[:, _sub_rows(r)], wm_ref[...], (((0,), (0,)), ((), ())),
                                preferred_element_type=F32)
        yield
        yield from _tail_stages(r, x_ref[_sub_rows(r), :], mixed, g_ref, wi_ref, wo_ref, p_ref,
                                wg_ref, wp_ref, o_ref)

    _skewed(stages(r) for r in range(SUB_TILES))


def _tail_ssm_kernel(y_ref, x_ref, wm_ref, bm_ref, g_ref, wi_ref, wo_ref, p_ref, wg_ref, wp_ref,
                     o_ref, y_scr):
    def stages(r):
        act = _gelu_tanh(_from_super_rows(y_ref, r, y_scr)).astype(BF16)
        yield
        z = _dot(act, wm_ref[...]) + bm_ref[...]
        yield
        mixed = z[:, :D_MODEL] * _sigmoid(z[:, D_MODEL:])
        yield from _tail_stages(r, x_ref[_sub_rows(r), :], mixed, g_ref, wi_ref, wo_ref, p_ref,
                                wg_ref, wp_ref, o_ref)

    _skewed(stages(r) for r in range(SUB_TILES))


def _layer_tail(mix, x, p, layer, g, w_mix, b_mix, w_in, w_out, w_gate, w_proj):
    ple_specs = [pl.BlockSpec((None, TM, PLE_DIM), lambda b, c: (layer, b * TILES_PER_SEQ + c, 0)),
                 _const_spec((D_MODEL, D_MODEL)), _const_spec((PLE_DIM, D_MODEL))]
    common = dict(grid=_TOKEN_GRID, out_specs=_TOKEN_TILE,
                  out_shape=jax.ShapeDtypeStruct((TOKENS, D_MODEL), F32),
                  compiler_params=_params(2))
    if b_mix is None:
        return pl.pallas_call(
            _tail_attn_kernel,
            in_specs=[_FEATURE_TILE, _TOKEN_TILE, _const_spec((D_MODEL, D_MODEL))] + _FFN_SPECS
            + ple_specs,
            name="attn_out_ffn_ple", **common,
        )(mix, x, w_mix, g, w_in, w_out, p, w_gate, w_proj)
    return pl.pallas_call(
        _tail_ssm_kernel,
        in_specs=[_SUPER_TILE, _TOKEN_TILE, _const_spec((D_MODEL, 2 * D_MODEL)),
                  _const_spec((1, 2 * D_MODEL))] + _FFN_SPECS + ple_specs,
        scratch_shapes=[pltpu.VMEM((OCTETS, TM, LANES), F32)],
        name="s5_glu_ffn_ple", **common,
    )(mix, x, w_mix, b_mix, g, w_in, w_out, p, w_gate, w_proj)


def kernel(x, p, norm_g, ffn_w_in, ffn_w_out, attn_w_qkv, attn_w_o, attn_lam, attn_subln_g,
           rel_bias, ssm_lam_re, ssm_lam_im, ssm_log_dt, ssm_b_re, ssm_b_im, ssm_c_re, ssm_c_im,
           ssm_d, ssm_w_glu, ssm_b_glu, ple_w_proj, ple_w_gate):
    x = x.reshape(TOKENS, D_MODEL)
    p = p.reshape(DEPTH, TOKENS, PLE_DIM)
    bias = _bias_tiles(rel_bias)
    bf = lambda w: w.astype(BF16)
    for i in range(DEPTH):
        g = norm_g[i]
        j = i // N_MIXERS
        if i % N_MIXERS == 0:
            lambda_init = 0.8 - 0.6 * math.exp(-0.3 * i)
            w = bf(attn_w_qkv[j])
            x, qt, k, vt = _layer_head(x, g, bf(ffn_w_in[i, 0]), bf(ffn_w_out[i, 0]),
                                       (w[:, :D_MODEL].T, w[:, D_MODEL:2 * D_MODEL],
                                        w[:, 2 * D_MODEL:].T))
            mix = _attention(qt, k, vt, bias, attn_lam[j], attn_subln_g[j].reshape(-1, 1),
                             lambda_init)
            w_mix, b_mix = bf(attn_w_o[j]), None
        else:
            bmat, cmat, pw_b, pw_c, a_chunk, d_oct = _ssm_operators(
                ssm_lam_re[j], ssm_lam_im[j], ssm_log_dt[j], ssm_b_re[j], ssm_b_im[j],
                ssm_c_re[j], ssm_c_im[j], ssm_d[j])
            x, u_oct = _layer_head(x, g, bf(ffn_w_in[i, 0]), bf(ffn_w_out[i, 0]))
            xs = _s5_states(u_oct, bmat, pw_b, a_chunk)
            mix = _s5_output(u_oct, xs, bmat, cmat, pw_c, d_oct)
            w_mix, b_mix = bf(ssm_w_glu[j]), ssm_b_glu[j].reshape(1, -1)
        x = _layer_tail(mix, x, p, i, g, w_mix, b_mix, bf(ffn_w_in[i, 1]), bf(ffn_w_out[i, 1]),
                        bf(ple_w_gate[i]), bf(ple_w_proj[i]))
    return x.reshape(BATCH, SEQ, D_MODEL)
```

```python
import functools
import math

import numpy as np
import jax
import jax.numpy as jnp
from jax import lax
from jax.experimental import pallas as pl
from jax.experimental.pallas import tpu as pltpu

D_MODEL = 1024
BATCH = 8
SEQ = 4096
DEPTH = 4
N_MIXERS = 2
HEAD_DIM = 64
N_HEADS = D_MODEL // (2 * HEAD_DIM)
REL_BUCKETS = 32
REL_MAX_DIST = 128
GROUP_CH = 16
GROUPS = D_MODEL // GROUP_CH
SSM_STATE = 64
D_FF = 2816
FFN_RESIDUAL = 0.5
PLE_DIM = 256
N_NORMS = 8
RMS_EPS = 1e-6
NEG_INF = -1e30

TOKENS = BATCH * SEQ
F32 = jnp.float32
BF16 = jnp.bfloat16
LANES = 128

V7X_VMEM_BYTES = 64 * 1024 * 1024
VMEM_LIMIT = V7X_VMEM_BYTES - 8 * 1024 * 1024

TM = 512
TILES_PER_SEQ = SEQ // TM
SUB_TILES = 2
SUB_TM = TM // SUB_TILES
MXU_TILE = 256
FF_SPLITS = (0, 6 * MXU_TILE, D_FF)
assert D_FF % MXU_TILE == 0
TQ = 512
TK = 512
BIAS_TILES = 3
TQ_HALF = TQ // 2
SUM_ROWS = 16
HEADS_PER_STEP = 8
EARLY = 2
LATE = 2
LOG2E = math.log2(math.e)

SSM_L = 16
SSM_NC = SEQ // SSM_L
OCTETS = D_MODEL // LANES
OCT_GROUPS = LANES // GROUP_CH
OCT_W = SSM_L * LANES
OCT_STATE = OCT_GROUPS * SSM_STATE
SSM_ROWS = BATCH * SSM_NC
SSM_RT = 1024
CH_PER_TILE = TM // SSM_L
SUB_CH = SUB_TM // SSM_L
RE_COLS = OCT_STATE // LANES
STATE_COLS = 2 * RE_COLS


def _t5_thresholds():
    n = np.arange(0, 4 * REL_MAX_DIST)
    max_exact = REL_BUCKETS // 2
    nf = np.maximum(n, 1).astype(np.float64)
    large = max_exact + (np.log(nf / max_exact) / math.log(REL_MAX_DIST / max_exact)
                         * (REL_BUCKETS - max_exact)).astype(np.int32)
    bucket = np.where(n < max_exact, n, np.minimum(large, REL_BUCKETS - 1))
    return [int(np.argmax(bucket >= j)) for j in range(1, REL_BUCKETS)]


T5_THRESHOLDS = _t5_thresholds()
assert TQ == TK and T5_THRESHOLDS[-1] <= TK, "key blocks before i-1 sit in the last bucket"

_NT = (((1,), (1,)), ((), ()))


def _const_spec(shape):
    nd = len(shape)
    return pl.BlockSpec(shape, lambda *_: (0,) * nd, pipeline_mode=pl.Buffered(1))


def _params(n_axes=1):
    return pltpu.CompilerParams(dimension_semantics=("arbitrary",) * n_axes,
                                vmem_limit_bytes=VMEM_LIMIT)


def _rms(x, g):
    return x * lax.rsqrt(jnp.mean(x * x, axis=-1, keepdims=True) + RMS_EPS) * g


def _sigmoid(x):
    return 1.0 / (1.0 + jnp.exp(-x))


def _dot(a, b):
    return jnp.dot(a, b, preferred_element_type=F32)


_TOKEN_TILE = pl.BlockSpec((TM, D_MODEL), lambda b, c: (b * TILES_PER_SEQ + c, 0))
_FEATURE_TILE = pl.BlockSpec((D_MODEL, TM), lambda b, c: (0, b * TILES_PER_SEQ + c))
_SUPER_TILE = pl.BlockSpec((OCTETS, CH_PER_TILE, OCT_W),
                           lambda b, c: (0, b * TILES_PER_SEQ + c, 0))
_TOKEN_GRID = (BATCH, TILES_PER_SEQ)
_FFN_SPECS = [_const_spec((N_NORMS, D_MODEL)), _const_spec((D_MODEL, 2 * D_FF)),
              _const_spec((D_FF, D_MODEL))]


def _skewed(stage_gens):
    waiting, running = list(stage_gens), []
    while waiting or running:
        if waiting:
            running.append(waiting.pop(0))
        for gen in list(running):
            if next(gen, _DONE) is _DONE:
                running.remove(gen)


_DONE = object()


def _sub_rows(r):
    return pl.ds(r * SUB_TM, SUB_TM)


def _ffn_stages(x, g_pre, g_post, wi_ref, wo_ref, out):
    h = _rms(x, g_pre).astype(BF16)
    yield
    y = None
    for lo, hi in zip(FF_SPLITS[:-1], FF_SPLITS[1:]):
        a = _dot(h, wi_ref[:, lo:hi])
        u = _dot(h, wi_ref[:, D_FF + lo:D_FF + hi])
        yield
        act = (a * _sigmoid(a) * u).astype(BF16)
        yield
        down = _dot(act, wo_ref[lo:hi, :])
        y = down if y is None else y + down
        yield
    out.append(x + FFN_RESIDUAL * _rms(y, g_post))


def _to_super_rows(h, r, o_ref, h_scr):
    chunks = slice(r * SUB_CH, (r + 1) * SUB_CH)
    for j in range(OCTETS):
        h_scr[j, _sub_rows(r), :] = h[:, j * LANES:(j + 1) * LANES]
    for j in range(OCTETS):
        for s in range(SSM_L):
            rows = h_scr[j, pl.ds(r * SUB_TM + s, SUB_CH, stride=SSM_L), :]
            o_ref[j, chunks, s * LANES:(s + 1) * LANES] = rows.astype(BF16)


def _from_super_rows(y_ref, r, y_scr):
    chunks = slice(r * SUB_CH, (r + 1) * SUB_CH)
    for j in range(OCTETS):
        for s in range(SSM_L):
            y_scr[j, pl.ds(r * SUB_TM + s, SUB_CH, stride=SSM_L), :] = (
                y_ref[j, chunks, s * LANES:(s + 1) * LANES].astype(F32))
    return jnp.concatenate([y_scr[j, _sub_rows(r), :] for j in range(OCTETS)], axis=1)


def _head_stages(r, x_ref, g_ref, wi_ref, wo_ref, xo_ref, out):
    res = []
    yield from _ffn_stages(x_ref[_sub_rows(r), :], g_ref[0:1, :], g_ref[1:2, :], wi_ref, wo_ref,
                           res)
    xo_ref[_sub_rows(r), :] = res[0]
    out.append(_rms(res[0], g_ref[2:3, :]))
    yield


def _head_attn_kernel(x_ref, g_ref, wi_ref, wo_ref, wqt_ref, wk_ref, wvt_ref,
                      xo_ref, qt_ref, k_ref, vt_ref):
    def stages(r):
        res = []
        yield from _head_stages(r, x_ref, g_ref, wi_ref, wo_ref, xo_ref, res)
        h = res[0].astype(BF16)
        cols = _sub_rows(r)
        qt = lax.dot_general(wqt_ref[...], h, _NT, preferred_element_type=F32)
        qt_ref[:, cols] = (qt * (HEAD_DIM ** -0.5 * LOG2E)).astype(BF16)
        k_ref[cols, :] = _dot(h, wk_ref[...]).astype(BF16)
        vt = lax.dot_general(wvt_ref[...], h, _NT, preferred_element_type=F32)
        vt_ref[:, cols] = vt.astype(BF16)

    _skewed(stages(r) for r in range(SUB_TILES))


def _head_ssm_kernel(x_ref, g_ref, wi_ref, wo_ref, xo_ref, u_ref, h_scr):
    def stages(r):
        res = []
        yield from _head_stages(r, x_ref, g_ref, wi_ref, wo_ref, xo_ref, res)
        _to_super_rows(res[0], r, u_ref, h_scr)

    _skewed(stages(r) for r in range(SUB_TILES))


def _layer_head(x, g, w_in, w_out, qkv=None):
    x_out = jax.ShapeDtypeStruct((TOKENS, D_MODEL), F32)
    if qkv is None:
        return pl.pallas_call(
            _head_ssm_kernel,
            grid=_TOKEN_GRID,
            in_specs=[_TOKEN_TILE] + _FFN_SPECS,
            out_specs=(_TOKEN_TILE, _SUPER_TILE),
            out_shape=(x_out, jax.ShapeDtypeStruct((OCTETS, SSM_ROWS, OCT_W), BF16)),
            scratch_shapes=[pltpu.VMEM((OCTETS, TM, LANES), F32)],
            compiler_params=_params(2),
            name="ffn_s5in",
        )(x, g, w_in, w_out)
    feat = jax.ShapeDtypeStruct((D_MODEL, TOKENS), BF16)
    return pl.pallas_call(
        _head_attn_kernel,
        grid=_TOKEN_GRID,
        in_specs=[_TOKEN_TILE] + _FFN_SPECS + [_const_spec((D_MODEL, D_MODEL))] * 3,
        out_specs=(_TOKEN_TILE, _FEATURE_TILE, _TOKEN_TILE, _FEATURE_TILE),
        out_shape=(x_out, feat, jax.ShapeDtypeStruct((TOKENS, D_MODEL), BF16), feat),
        compiler_params=_params(2),
        name="ffn_qkv",
    )(x, g, w_in, w_out, *qkv)


def _bias_kernel(rel_ref, o_ref):
    c = pl.program_id(0)
    r = pl.program_id(1)
    @pl.when(r == BIAS_TILES - 1)
    def _():
        o_ref[...] = jnp.zeros((TK, TQ), BF16)

    @pl.when(r < BIAS_TILES - 1)
    def _():
        ki = lax.broadcasted_iota(jnp.int32, (TK, TQ), 0)
        qi = lax.broadcasted_iota(jnp.int32, (TK, TQ), 1)
        d = qi - ki + r * TK
        val = jnp.full((TK, TQ), rel_ref[0, c], F32)
        for j, thr in enumerate(T5_THRESHOLDS, start=1):
            val = jnp.where(d >= thr, rel_ref[j, c], val)
        val = (val - rel_ref[REL_BUCKETS - 1, c]) * LOG2E
        o_ref[...] = jnp.where(d >= 0, val, NEG_INF).astype(BF16)


def _bias_tiles(rel_bias):
    tiles = pl.pallas_call(
        _bias_kernel,
        grid=(2 * N_HEADS, BIAS_TILES),
        in_specs=[pl.BlockSpec(memory_space=pltpu.SMEM)],
        out_specs=pl.BlockSpec((None, None, TK, TQ), lambda c, r: (c, r, 0, 0)),
        out_shape=jax.ShapeDtypeStruct((2 * N_HEADS, BIAS_TILES, TK, TQ), BF16),
        compiler_params=_params(2),
        name="t5_bias",
    )(rel_bias)
    return tiles.reshape(N_HEADS, 2, BIAS_TILES, TK, TQ)


def _attn_kernel(qt_ref, k_ref, vt_ref, bias_ref, lam_ref, sg_ref, o_ref, m_ref, acc_ref,
                 s_buf, p_buf, a_buf, *, lambda_init):
    i = pl.program_id(2)
    d = 2 * HEAD_DIM
    chains = [(hd, mi, slice(hf * TQ_HALF, (hf + 1) * TQ_HALF))
              for hd in range(HEADS_PER_STEP) for hf in range(2) for mi in range(2)]
    n = len(chains)

    qt = qt_ref[...]
    row = lax.broadcasted_iota(jnp.int32, (d, TQ), 0)
    zero = jnp.zeros((d, TQ), BF16)
    q_maps = [[jnp.where(row < HEAD_DIM, qt[hd * d:(hd + 1) * d], zero),
               jnp.where(row >= HEAD_DIM, qt[hd * d:(hd + 1) * d], zero)]
              for hd in range(HEADS_PER_STEP)]

    m_ref[...] = jnp.full(m_ref.shape, NEG_INF, F32)
    acc_ref[...] = jnp.zeros(acc_ref.shape, F32)
    sum_rows = jnp.where(lax.broadcasted_iota(jnp.int32, (SUM_ROWS, TK), 0) == 0, 1.0, 0.0).astype(BF16)

    def keys(j):
        return k_ref[pl.ds(pl.multiple_of(j * TK, TK), TK), :]

    def vals(j):
        vb = vt_ref[:, pl.ds(pl.multiple_of(j * TK, TK), TK)]
        return [jnp.concatenate([vb[hd * d:(hd + 1) * d], sum_rows], axis=0)
                for hd in range(HEADS_PER_STEP)]

    def scores(kb, j, chain):
        hd, mi, cols = chain
        tile = jnp.clip(i - j, 0, BIAS_TILES - 1)
        s = _dot(kb[:, hd * d:(hd + 1) * d], q_maps[hd][mi][:, cols])
        return s.astype(BF16) + bias_ref[hd, mi, tile, :, cols]

    def softmax(chain, s):
        hd, mi, cols = chain
        m_prev = m_ref[hd, mi, :, cols]
        m_new = jnp.maximum(m_prev, jnp.max(s, axis=0, keepdims=True).astype(F32))
        alpha = jnp.exp2(m_prev - m_new)
        p = jnp.exp2(s - m_new.astype(BF16))
        m_ref[hd, mi, :, cols] = m_new
        return p, alpha

    def values(vb, chain, p, alpha):
        hd, mi, cols = chain
        acc_ref[hd, mi, :, cols] = alpha * acc_ref[hd, mi, :, cols] + _dot(vb[hd], p)

    kb0 = keys(0)
    for e in range(EARLY):
        s_buf[e] = scores(kb0, 0, chains[e])
    for t in range(LATE):
        p_buf[t] = jnp.zeros((TK, TQ_HALF), BF16)
        a_buf[t] = jnp.ones((1, TQ_HALF), F32)

    def trip(j, carry):
        kb, vb = keys(j), vals(j)
        j_next = jnp.minimum(j + 1, i)
        kb_next, vb_prev = keys(j_next), vals(jnp.maximum(j - 1, 0))
        s_tiles, p_tiles = {}, {}
        for slot in range(n):
            ahead = slot + EARLY
            if ahead < n:
                s_tiles[ahead] = scores(kb, j, chains[ahead])
            else:
                s_next = scores(kb_next, j_next, chains[ahead - n])
            s = s_buf[slot] if slot < EARLY else s_tiles.pop(slot)
            p_tiles[slot] = softmax(chains[slot], s)
            if ahead >= n:
                s_buf[ahead - n] = s_next
            behind = slot - LATE
            if behind >= 0:
                values(vb, chains[behind], *p_tiles.pop(behind))
            else:
                values(vb_prev, chains[n + behind], p_buf[slot], a_buf[slot])
            if slot >= n - LATE:
                p_buf[slot - (n - LATE)], a_buf[slot - (n - LATE)] = p_tiles.pop(slot)
        return carry

    lax.fori_loop(0, i + 1, trip, 0)
    vb_last = vals(i)
    for t in range(LATE):
        values(vb_last, chains[n - LATE + t], p_buf[t], a_buf[t])

    lv = lam_ref[...]
    lam = (jnp.exp(jnp.sum(lv[0:1] * lv[1:2], keepdims=True))
           - jnp.exp(jnp.sum(lv[2:3] * lv[3:4], keepdims=True)) + lambda_init)
    for hd in range(HEADS_PER_STEP):
        ot = (acc_ref[hd, 0, 0:d, :] * (1.0 / acc_ref[hd, 0, d:d + 1, :])
              - lam * (acc_ref[hd, 1, 0:d, :] * (1.0 / acc_ref[hd, 1, d:d + 1, :])))
        inv = lax.rsqrt(jnp.mean(ot * ot, axis=0, keepdims=True) + RMS_EPS) * (1.0 - lambda_init)
        o_ref[hd * d:(hd + 1) * d, :] = (ot * inv * sg_ref[...]).astype(BF16)


def _attention(qt, k, vt, bias, lam_vecs, subln_g, lambda_init):
    nq = SEQ // TQ
    wide = HEADS_PER_STEP * 2 * HEAD_DIM
    return pl.pallas_call(
        functools.partial(_attn_kernel, lambda_init=lambda_init),
        grid=(BATCH, N_HEADS // HEADS_PER_STEP, nq),
        in_specs=[pl.BlockSpec((wide, TQ), lambda b, h, i: (h, b * nq + i)),
                  pl.BlockSpec((SEQ, wide), lambda b, h, i: (b, h), pipeline_mode=pl.Buffered(1)),
                  pl.BlockSpec((wide, SEQ), lambda b, h, i: (h, b), pipeline_mode=pl.Buffered(1)),
                  pl.BlockSpec((HEADS_PER_STEP, 2, BIAS_TILES, TK, TQ),
                               lambda b, h, i: (h, 0, 0, 0, 0), pipeline_mode=pl.Buffered(1)),
                  pl.BlockSpec((4, HEAD_DIM), lambda b, h, i: (0, 0)),
                  pl.BlockSpec((2 * HEAD_DIM, 1), lambda b, h, i: (0, 0))],
        out_specs=pl.BlockSpec((wide, TQ), lambda b, h, i: (h, b * nq + i)),
        out_shape=jax.ShapeDtypeStruct((D_MODEL, TOKENS), BF16),
        scratch_shapes=[pltpu.VMEM((HEADS_PER_STEP, 2, 1, TQ), F32),
                        pltpu.VMEM((HEADS_PER_STEP, 2, 2 * HEAD_DIM + SUM_ROWS, TQ), F32),
                        pltpu.VMEM((EARLY, TK, TQ_HALF), BF16), pltpu.VMEM((LATE, TK, TQ_HALF), BF16),
                        pltpu.VMEM((LATE, 1, TQ_HALF), F32)],
        compiler_params=_params(3),
        name="diff_attn",
    )(qt, k, vt, bias, lam_vecs, subln_g)


def _ssm_operators(lam_re, lam_im, log_dt, b_re, b_im, c_re, c_im, d_skip):
    lam_re, lam_im, log_dt, b_re, b_im, c_re, c_im, d_skip = lax.optimization_barrier(
        (lam_re, lam_im, log_dt, b_re, b_im, c_re, c_im, d_skip))
    dt = jnp.exp(log_dt)[:, None]
    zr, zi = lam_re * dt, lam_im * dt
    ks = jnp.arange(SSM_L + 1, dtype=F32)[:, None, None]
    mag = jnp.exp(ks * zr)
    pr, pi = mag * jnp.cos(ks * zi), mag * jnp.sin(ks * zi)
    nr = jnp.expm1(zr) * jnp.cos(zi) - 2.0 * jnp.sin(0.5 * zi) ** 2
    ni = pi[1]
    den = lam_re * lam_re + lam_im * lam_im
    fr, fi = (nr * lam_re + ni * lam_im) / den, (ni * lam_re - nr * lam_im) / den
    bb_re = fr[..., None] * b_re - fi[..., None] * b_im
    bb_im = fr[..., None] * b_im + fi[..., None] * b_re
    same = jnp.eye(OCT_GROUPS, dtype=F32)

    def block_rows(t):
        t = lax.optimization_barrier(t)
        return (t[:, :, :, None, :] * same[None, :, None, :, None]).reshape(OCTETS, LANES, OCT_STATE)

    b_rows = lambda t: block_rows(t.reshape(OCTETS, OCT_GROUPS, SSM_STATE, GROUP_CH)
                                  .transpose(0, 1, 3, 2))
    c_rows = lambda t: block_rows(t.reshape(OCTETS, OCT_GROUPS, GROUP_CH, SSM_STATE))
    bmat = jnp.stack([b_rows(bb_re), b_rows(bb_im)], axis=1)
    cmat = jnp.stack([c_rows(c_re), c_rows(c_im)], axis=1)

    kr = jnp.arange(SSM_L - 1, -1, -1, dtype=F32)[:, None, None]
    mag_r = jnp.exp(kr * zr)
    per_oct = lambda t: t.reshape(t.shape[:-2] + (OCTETS, OCT_STATE))
    pw_b = per_oct(jnp.stack([mag_r * jnp.cos(kr * zi), mag_r * jnp.sin(kr * zi)]))
    pw_c = per_oct(jnp.stack([pr[1:], pi[1:]]))
    a_chunk = per_oct(jnp.stack([pr[SSM_L], pi[SSM_L]]))
    return (bmat, cmat, pw_b.transpose(2, 0, 1, 3), pw_c.transpose(2, 0, 1, 3),
            a_chunk.transpose(1, 0, 2), d_skip.reshape(OCTETS, 1, LANES))


def _scaled_blocks(mat_ref, pw_ref, out_ref, im_sign):
    def block(s, carry):
        m_re, m_im = mat_ref[0], mat_ref[1]
        p_re, p_im = pw_ref[0, pl.ds(s, 1), :], pw_ref[1, pl.ds(s, 1), :]
        rows = pl.ds(pl.multiple_of(s * LANES, LANES), LANES)
        out_ref[rows, 0:OCT_STATE] = (m_re * p_re - m_im * p_im).astype(BF16)
        out_ref[rows, OCT_STATE:2 * OCT_STATE] = (im_sign * (m_re * p_im + m_im * p_re)).astype(BF16)
        return carry

    lax.fori_loop(0, SSM_L, block, 0)


def _s5_state_kernel(u_ref, bmat_ref, pw_ref, a_ref, xs_ref, bpow_scr, s_scr, x_scr):
    _scaled_blocks(bmat_ref, pw_ref, bpow_scr, 1.0)
    for r in range(0, SSM_ROWS, SSM_RT):
        s_loc = _dot(u_ref[r:r + SSM_RT, :], bpow_scr[...])
        for k in range(STATE_COLS):
            s_scr[k, r:r + SSM_RT, :] = s_loc[:, k * LANES:(k + 1) * LANES]
    a = a_ref[...]
    col = lambda r, k: jnp.broadcast_to(a[r:r + 1, k * LANES:(k + 1) * LANES], (BATCH, LANES))
    ar = [col(0, k) for k in range(RE_COLS)]
    ai = [col(1, k) for k in range(RE_COLS)]

    def step(c, carry):
        rows = pl.ds(c, BATCH, stride=SSM_NC)
        nxt_re, nxt_im = [], []
        for k in range(RE_COLS):
            xr, xi = carry[k], carry[RE_COLS + k]
            x_scr[k, rows, :] = xr
            x_scr[RE_COLS + k, rows, :] = xi
            nxt_re.append(ar[k] * xr - ai[k] * xi + s_scr[k, rows, :])
            nxt_im.append(ar[k] * xi + ai[k] * xr + s_scr[RE_COLS + k, rows, :])
        return tuple(nxt_re + nxt_im)

    zeros = jnp.zeros((BATCH, LANES), F32)
    lax.fori_loop(0, SSM_NC, step, (zeros,) * STATE_COLS, unroll=8)
    for k in range(STATE_COLS):
        xs_ref[:, k * LANES:(k + 1) * LANES] = x_scr[k].astype(BF16)


def _s5_states(u_oct, bmat, pw_b, a_chunk):
    per_oct = lambda *tail: pl.BlockSpec((None,) + tail, lambda j: (j,) + (0,) * len(tail))
    return pl.pallas_call(
        _s5_state_kernel,
        grid=(OCTETS,),
        in_specs=[per_oct(SSM_ROWS, OCT_W), per_oct(2, LANES, OCT_STATE),
                  per_oct(2, SSM_L, OCT_STATE), per_oct(2, OCT_STATE)],
        out_specs=per_oct(SSM_ROWS, 2 * OCT_STATE),
        out_shape=jax.ShapeDtypeStruct((OCTETS, SSM_ROWS, 2 * OCT_STATE), BF16),
        scratch_shapes=[pltpu.VMEM((OCT_W, 2 * OCT_STATE), BF16),
                        pltpu.VMEM((STATE_COLS, SSM_ROWS, LANES), F32),
                        pltpu.VMEM((STATE_COLS, SSM_ROWS, LANES), F32)],
        compiler_params=_params(),
        name="s5_states",
    )(u_oct, bmat, pw_b, a_chunk)


def _s5_out_kernel(u_ref, xs_ref, bmat_ref, cmat_ref, pw_ref, d_ref, y_ref, toep_scr, cpow_scr):
    @pl.when(pl.program_id(1) == 0)
    def _():
        _scaled_blocks(cmat_ref, pw_ref, cpow_scr, -1.0)
        b_cat = jnp.concatenate([bmat_ref[0], bmat_ref[1]], axis=1).astype(BF16)
        c_tau0 = jnp.concatenate([cmat_ref[0], -cmat_ref[1]], axis=1).astype(BF16)
        c_all = jnp.concatenate([c_tau0, cpow_scr[0:OCT_W - LANES, :]], axis=0)
        imp = lax.dot_general(b_cat, c_all, _NT, preferred_element_type=F32)
        on_diag = (lax.broadcasted_iota(jnp.int32, (LANES, LANES), 0)
                   == lax.broadcasted_iota(jnp.int32, (LANES, LANES), 1))
        skip = jnp.where(on_diag, d_ref[...], 0.0)
        imp = jnp.concatenate([imp[:, 0:LANES] + skip, imp[:, LANES:]], axis=1).astype(BF16)
        toep_scr[...] = jnp.zeros(toep_scr.shape, BF16)
        for s in range(SSM_L):
            toep_scr[s * LANES:(s + 1) * LANES, s * LANES:] = imp[:, :OCT_W - s * LANES]

    u, xs = u_ref[...], xs_ref[...]
    for lo in range(0, OCT_W, MXU_TILE):
        hi = lo + MXU_TILE
        y = _dot(u[:, :hi], toep_scr[0:hi, lo:hi]) + lax.dot_general(
            xs, cpow_scr[lo:hi, :], _NT, preferred_element_type=F32)
        y_ref[:, lo:hi] = y.astype(BF16)


def _s5_output(u_oct, xs, bmat, cmat, pw_c, d_oct):
    rows = lambda w: pl.BlockSpec((None, SSM_RT, w), lambda j, r: (j, r, 0))
    per_oct = lambda *tail: pl.BlockSpec((None,) + tail, lambda j, r: (j,) + (0,) * len(tail))
    return pl.pallas_call(
        _s5_out_kernel,
        grid=(OCTETS, SSM_ROWS // SSM_RT),
        in_specs=[rows(OCT_W), rows(2 * OCT_STATE), per_oct(2, LANES, OCT_STATE),
                  per_oct(2, LANES, OCT_STATE), per_oct(2, SSM_L, OCT_STATE), per_oct(1, LANES)],
        out_specs=rows(OCT_W),
        out_shape=jax.ShapeDtypeStruct((OCTETS, SSM_ROWS, OCT_W), BF16),
        scratch_shapes=[pltpu.VMEM((OCT_W, OCT_W), BF16), pltpu.VMEM((OCT_W, 2 * OCT_STATE), BF16)],
        compiler_params=_params(2),
        name="s5_output",
    )(u_oct, xs, bmat, cmat, pw_c, d_oct)


def _gelu_tanh(x):
    return 0.5 * x * (1.0 + jnp.tanh(math.sqrt(2.0 / math.pi) * (x + 0.044715 * (x * x * x))))


def _tail_stages(r, x, mixed, g_ref, wi_ref, wo_ref, p_ref, wg_ref, wp_ref, o_ref):
    x = x + _rms(mixed, g_ref[3:4, :])
    res = []
    yield from _ffn_stages(x, g_ref[4:5, :], g_ref[5:6, :], wi_ref, wo_ref, res)
    x = res[0]
    h = _rms(x, g_ref[6:7, :]).astype(BF16)
    yield
    gate = _dot(h, wg_ref[...])
    emb = _dot(p_ref[_sub_rows(r), :].astype(BF16), wp_ref[...])
    yield
    o_ref[_sub_rows(r), :] = x + _rms(_sigmoid(gate) * emb, g_ref[7:8, :])


def _tail_attn_kernel(a_ref, x_ref, wm_ref, g_ref, wi_ref, wo_ref, p_ref, wg_ref, wp_ref, o_ref):
    def stages(r):
        mixed = lax.dot_general(a_ref[:, _sub_rows(r)], wm_ref[...], (((0,), (0,)), ((), ())),
                                preferred_element_type=F32)
        yield
        yield from _tail_stages(r, x_ref[_sub_rows(r), :], mixed, g_ref, wi_ref, wo_ref, p_ref,
                                wg_ref, wp_ref, o_ref)

    _skewed(stages(r) for r in range(SUB_TILES))


def _tail_ssm_kernel(y_ref, x_ref, wm_ref, bm_ref, g_ref, wi_ref, wo_ref, p_ref, wg_ref, wp_ref,
                     o_ref, y_scr):
    def stages(r):
        act = _gelu_tanh(_from_super_rows(y_ref, r, y_scr)).astype(BF16)
        yield
        z = _dot(act, wm_ref[...]) + bm_ref[...]
        yield
        mixed = z[:, :D_MODEL] * _sigmoid(z[:, D_MODEL:])
        yield from _tail_stages(r, x_ref[_sub_rows(r), :], mixed, g_ref, wi_ref, wo_ref, p_ref,
                                wg_ref, wp_ref, o_ref)

    _skewed(stages(r) for r in range(SUB_TILES))


def _layer_tail(mix, x, p, layer, g, w_mix, b_mix, w_in, w_out, w_gate, w_proj):
    ple_specs = [pl.BlockSpec((None, TM, PLE_DIM), lambda b, c: (layer, b * TILES_PER_SEQ + c, 0)),
                 _const_spec((D_MODEL, D_MODEL)), _const_spec((PLE_DIM, D_MODEL))]
    common = dict(grid=_TOKEN_GRID, out_specs=_TOKEN_TILE,
                  out_shape=jax.ShapeDtypeStruct((TOKENS, D_MODEL), F32),
                  compiler_params=_params(2))
    if b_mix is None:
        return pl.pallas_call(
            _tail_attn_kernel,
            in_specs=[_FEATURE_TILE, _TOKEN_TILE, _const_spec((D_MODEL, D_MODEL))] + _FFN_SPECS
            + ple_specs,
            name="attn_out_ffn_ple", **common,
        )(mix, x, w_mix, g, w_in, w_out, p, w_gate, w_proj)
    return pl.pallas_call(
        _tail_ssm_kernel,
        in_specs=[_SUPER_TILE, _TOKEN_TILE, _const_spec((D_MODEL, 2 * D_MODEL)),
                  _const_spec((1, 2 * D_MODEL))] + _FFN_SPECS + ple_specs,
        scratch_shapes=[pltpu.VMEM((OCTETS, TM, LANES), F32)],
        name="s5_glu_ffn_ple", **common,
    )(mix, x, w_mix, b_mix, g, w_in, w_out, p, w_gate, w_proj)


def kernel(x, p, norm_g, ffn_w_in, ffn_w_out, attn_w_qkv, attn_w_o, attn_lam, attn_subln_g,
           rel_bias, ssm_lam_re, ssm_lam_im, ssm_log_dt, ssm_b_re, ssm_b_im, ssm_c_re, ssm_c_im,
           ssm_d, ssm_w_glu, ssm_b_glu, ple_w_proj, ple_w_gate):
    x = x.reshape(TOKENS, D_MODEL)
    p = p.reshape(DEPTH, TOKENS, PLE_DIM)
    bias = _bias_tiles(rel_bias)
    bf = lambda w: w.astype(BF16)
    for i in range(DEPTH):
        g = norm_g[i]
        j = i // N_MIXERS
        if i % N_MIXERS == 0:
            lambda_init = 0.8 - 0.6 * math.exp(-0.3 * i)
            w = bf(attn_w_qkv[j])
            x, qt, k, vt = _layer_head(x, g, bf(ffn_w_in[i, 0]), bf(ffn_w_out[i, 0]),
                                       (w[:, :D_MODEL].T, w[:, D_MODEL:2 * D_MODEL],
                                        w[:, 2 * D_MODEL:].T))
            mix = _attention(qt, k, vt, bias, attn_lam[j], attn_subln_g[j].reshape(-1, 1),
                             lambda_init)
            w_mix, b_mix = bf(attn_w_o[j]), None
        else:
            bmat, cmat, pw_b, pw_c, a_chunk, d_oct = _ssm_operators(
                ssm_lam_re[j], ssm_lam_im[j], ssm_log_dt[j], ssm_b_re[j], ssm_b_im[j],
                ssm_c_re[j], ssm_c_im[j], ssm_d[j])
            x, u_oct = _layer_head(x, g, bf(ffn_w_in[i, 0]), bf(ffn_w_out[i, 0]))
            xs = _s5_states(u_oct, bmat, pw_b, a_chunk)
            mix = _s5_output(u_oct, xs, bmat, cmat, pw_c, d_oct)
            w_mix, b_mix = bf(ssm_w_glu[j]), ssm_b_glu[j].reshape(1, -1)
        x = _layer_tail(mix, x, p, i, g, w_mix, b_mix, bf(ffn_w_in[i, 1]), bf(ffn_w_out[i, 1]),
                        bf(ple_w_gate[i]), bf(ple_w_proj[i]))
    return x.reshape(BATCH, SEQ, D_MODEL)
```

```python
import functools
import math

import numpy as np
import jax
import jax.numpy as jnp
from jax import lax
from jax.experimental import pallas as pl
from jax.experimental.pallas import tpu as pltpu

D_MODEL = 1024
BATCH = 8
SEQ = 4096
DEPTH = 4
N_MIXERS = 2
HEAD_DIM = 64
N_HEADS = D_MODEL // (2 * HEAD_DIM)
REL_BUCKETS = 32
REL_MAX_DIST = 128
GROUP_CH = 16
GROUPS = D_MODEL // GROUP_CH
SSM_STATE = 64
D_FF = 2816
FFN_RESIDUAL = 0.5
PLE_DIM = 256
N_NORMS = 8
RMS_EPS = 1e-6
NEG_INF = -1e30

TOKENS = BATCH * SEQ
F32 = jnp.float32
BF16 = jnp.bfloat16
LANES = 128

V7X_VMEM_BYTES = 64 * 1024 * 1024
VMEM_LIMIT = V7X_VMEM_BYTES - 8 * 1024 * 1024

TM = 512
TILES_PER_SEQ = SEQ // TM
SUB_TILES = 2
SUB_TM = TM // SUB_TILES
MXU_TILE = 256
FF_SPLITS = (0, 6 * MXU_TILE, D_FF)
assert D_FF % MXU_TILE == 0
TQ = 512
TK = 512
BIAS_TILES = 3
TQ_HALF = TQ // 2
SUM_ROWS = 16
HEADS_PER_STEP = 8
EARLY = 2
LATE = 2
LOG2E = math.log2(math.e)

SSM_L = 16
SSM_NC = SEQ // SSM_L
OCTETS = D_MODEL // LANES
OCT_GROUPS = LANES // GROUP_CH
OCT_W = SSM_L * LANES
OCT_STATE = OCT_GROUPS * SSM_STATE
SSM_ROWS = BATCH * SSM_NC
SSM_RT = 1024
CH_PER_TILE = TM // SSM_L
SUB_CH = SUB_TM // SSM_L
RE_COLS = OCT_STATE // LANES
STATE_COLS = 2 * RE_COLS


def _t5_thresholds():
    n = np.arange(0, 4 * REL_MAX_DIST)
    max_exact = REL_BUCKETS // 2
    nf = np.maximum(n, 1).astype(np.float64)
    large = max_exact + (np.log(nf / max_exact) / math.log(REL_MAX_DIST / max_exact)
                         * (REL_BUCKETS - max_exact)).astype(np.int32)
    bucket = np.where(n < max_exact, n, np.minimum(large, REL_BUCKETS - 1))
    return [int(np.argmax(bucket >= j)) for j in range(1, REL_BUCKETS)]


T5_THRESHOLDS = _t5_thresholds()
assert TQ == TK and T5_THRESHOLDS[-1] <= TK, "key blocks before i-1 sit in the last bucket"

_NT = (((1,), (1,)), ((), ()))


def _const_spec(shape):
    nd = len(shape)
    return pl.BlockSpec(shape, lambda *_: (0,) * nd, pipeline_mode=pl.Buffered(1))


def _params(n_axes=1):
    return pltpu.CompilerParams(dimension_semantics=("arbitrary",) * n_axes,
                                vmem_limit_bytes=VMEM_LIMIT)


def _rms(x, g):
    return x * lax.rsqrt(jnp.mean(x * x, axis=-1, keepdims=True) + RMS_EPS) * g


def _sigmoid(x):
    return 1.0 / (1.0 + jnp.exp(-x))


def _dot(a, b):
    return jnp.dot(a, b, preferred_element_type=F32)


_TOKEN_TILE = pl.BlockSpec((TM, D_MODEL), lambda b, c: (b * TILES_PER_SEQ + c, 0))
_FEATURE_TILE = pl.BlockSpec((D_MODEL, TM), lambda b, c: (0, b * TILES_PER_SEQ + c))
_SUPER_TILE = pl.BlockSpec((OCTETS, CH_PER_TILE, OCT_W),
                           lambda b, c: (0, b * TILES_PER_SEQ + c, 0))
_TOKEN_GRID = (BATCH, TILES_PER_SEQ)
_FFN_SPECS = [_const_spec((N_NORMS, D_MODEL)), _const_spec((D_MODEL, 2 * D_FF)),
              _const_spec((D_FF, D_MODEL))]


def _skewed(stage_gens):
    waiting, running = list(stage_gens), []
    while waiting or running:
        if waiting:
            running.append(waiting.pop(0))
        for gen in list(running):
            if next(gen, _DONE) is _DONE:
                running.remove(gen)


_DONE = object()


def _sub_rows(r):
    return pl.ds(r * SUB_TM, SUB_TM)


def _ffn_stages(x, g_pre, g_post, wi_ref, wo_ref, out):
    h = _rms(x, g_pre).astype(BF16)
    yield
    y = None
    for lo, hi in zip(FF_SPLITS[:-1], FF_SPLITS[1:]):
        a = _dot(h, wi_ref[:, lo:hi])
        u = _dot(h, wi_ref[:, D_FF + lo:D_FF + hi])
        yield
        act = (a * _sigmoid(a) * u).astype(BF16)
        yield
        down = _dot(act, wo_ref[lo:hi, :])
        y = down if y is None else y + down
        yield
    out.append(x + FFN_RESIDUAL * _rms(y, g_post))


def _to_super_rows(h, r, o_ref, h_scr):
    chunks = slice(r * SUB_CH, (r + 1) * SUB_CH)
    for j in range(OCTETS):
        h_scr[j, _sub_rows(r), :] = h[:, j * LANES:(j + 1) * LANES]
    for j in range(OCTETS):
        for s in range(SSM_L):
            rows = h_scr[j, pl.ds(r * SUB_TM + s, SUB_CH, stride=SSM_L), :]
            o_ref[j, chunks, s * LANES:(s + 1) * LANES] = rows.astype(BF16)


def _from_super_rows(y_ref, r, y_scr):
    chunks = slice(r * SUB_CH, (r + 1) * SUB_CH)
    for j in range(OCTETS):
        for s in range(SSM_L):
            y_scr[j, pl.ds(r * SUB_TM + s, SUB_CH, stride=SSM_L), :] = (
                y_ref[j, chunks, s * LANES:(s + 1) * LANES].astype(F32))
    return jnp.concatenate([y_scr[j, _sub_rows(r), :] for j in range(OCTETS)], axis=1)


def _head_stages(r, x_ref, g_ref, wi_ref, wo_ref, xo_ref, out):
    res = []
    yield from _ffn_stages(x_ref[_sub_rows(r), :], g_ref[0:1, :], g_ref[1:2, :], wi_ref, wo_ref,
                           res)
    xo_ref[_sub_rows(r), :] = res[0]
    out.append(_rms(res[0], g_ref[2:3, :]))
    yield


def _head_attn_kernel(x_ref, g_ref, wi_ref, wo_ref, wqt_ref, wk_ref, wvt_ref,
                      xo_ref, qt_ref, k_ref, vt_ref):
    def stages(r):
        res = []
        yield from _head_stages(r, x_ref, g_ref, wi_ref, wo_ref, xo_ref, res)
        h = res[0].astype(BF16)
        cols = _sub_rows(r)
        qt = lax.dot_general(wqt_ref[...], h, _NT, preferred_element_type=F32)
        qt_ref[:, cols] = (qt * (HEAD_DIM ** -0.5 * LOG2E)).astype(BF16)
        k_ref[cols, :] = _dot(h, wk_ref[...]).astype(BF16)
        vt = lax.dot_general(wvt_ref[...], h, _NT, preferred_element_type=F32)
        vt_ref[:, cols] = vt.astype(BF16)

    _skewed(stages(r) for r in range(SUB_TILES))


def _head_ssm_kernel(x_ref, g_ref, wi_ref, wo_ref, xo_ref, u_ref, h_scr):
    def stages(r):
        res = []
        yield from _head_stages(r, x_ref, g_ref, wi_ref, wo_ref, xo_ref, res)
        _to_super_rows(res[0], r, u_ref, h_scr)

    _skewed(stages(r) for r in range(SUB_TILES))


def _layer_head(x, g, w_in, w_out, qkv=None):
    x_out = jax.ShapeDtypeStruct((TOKENS, D_MODEL), F32)
    if qkv is None:
        return pl.pallas_call(
            _head_ssm_kernel,
            grid=_TOKEN_GRID,
            in_specs=[_TOKEN_TILE] + _FFN_SPECS,
            out_specs=(_TOKEN_TILE, _SUPER_TILE),
            out_shape=(x_out, jax.ShapeDtypeStruct((OCTETS, SSM_ROWS, OCT_W), BF16)),
            scratch_shapes=[pltpu.VMEM((OCTETS, TM, LANES), F32)],
            compiler_params=_params(2),
            name="ffn_s5in",
        )(x, g, w_in, w_out)
    feat = jax.ShapeDtypeStruct((D_MODEL, TOKENS), BF16)
    return pl.pallas_call(
        _head_attn_kernel,
        grid=_TOKEN_GRID,
        in_specs=[_TOKEN_TILE] + _FFN_SPECS + [_const_spec((D_MODEL, D_MODEL))] * 3,
        out_specs=(_TOKEN_TILE, _FEATURE_TILE, _TOKEN_TILE, _FEATURE_TILE),
        out_shape=(x_out, feat, jax.ShapeDtypeStruct((TOKENS, D_MODEL), BF16), feat),
        compiler_params=_params(2),
        name="ffn_qkv",
    )(x, g, w_in, w_out, *qkv)


def _bias_kernel(rel_ref, o_ref):
    c = pl.program_id(0)
    ki = lax.broadcasted_iota(jnp.int32, (LANES, LANES), 0)
    qi = lax.broadcasted_iota(jnp.int32, (LANES, LANES), 1)
    bands = {}

    def band(lag):
        if lag not in bands:
            d = qi - ki + lag
            val = jnp.full((LANES, LANES), rel_ref[0, c], F32)
            for j, thr in enumerate(T5_THRESHOLDS, start=1):
                val = jnp.where(d >= thr, rel_ref[j, c], val)
            val = (val - rel_ref[REL_BUCKETS - 1, c]) * LOG2E
            bands[lag] = jnp.where(d >= 0, val, NEG_INF).astype(BF16)
        return bands[lag]

    zeros = jnp.zeros((LANES, LANES), BF16)
    masked = jnp.full((LANES, LANES), NEG_INF, BF16)
    for r in range(BIAS_TILES):
        for kb in range(TK // LANES):
            for qb in range(TQ // LANES):
                lag = (qb - kb) * LANES + r * TK
                if lag + LANES - 1 < 0:
                    blk = masked
                elif lag - (LANES - 1) >= T5_THRESHOLDS[-1]:
                    blk = zeros
                else:
                    blk = band(lag)
                o_ref[r, kb * LANES:(kb + 1) * LANES, qb * LANES:(qb + 1) * LANES] = blk


def _bias_tiles(rel_bias):
    tiles = pl.pallas_call(
        _bias_kernel,
        grid=(2 * N_HEADS,),
        in_specs=[pl.BlockSpec(memory_space=pltpu.SMEM)],
        out_specs=pl.BlockSpec((None, BIAS_TILES, TK, TQ), lambda c: (c, 0, 0, 0)),
        out_shape=jax.ShapeDtypeStruct((2 * N_HEADS, BIAS_TILES, TK, TQ), BF16),
        compiler_params=_params(1),
        name="t5_bias",
    )(rel_bias)
    return tiles.reshape(N_HEADS, 2, BIAS_TILES, TK, TQ)


def _attn_kernel(qt_ref, k_ref, vt_ref, bias_ref, lam_ref, sg_ref, o_ref, m_ref, acc_ref,
                 s_buf, p_buf, a_buf, *, lambda_init):
    i = pl.program_id(2)
    d = 2 * HEAD_DIM
    chains = [(hd, mi, slice(hf * TQ_HALF, (hf + 1) * TQ_HALF))
              for hd in range(HEADS_PER_STEP) for hf in range(2) for mi in range(2)]
    n = len(chains)

    qt = qt_ref[...]
    row = lax.broadcasted_iota(jnp.int32, (d, TQ), 0)
    zero = jnp.zeros((d, TQ), BF16)
    q_maps = [[jnp.where(row < HEAD_DIM, qt[hd * d:(hd + 1) * d], zero),
               jnp.where(row >= HEAD_DIM, qt[hd * d:(hd + 1) * d], zero)]
              for hd in range(HEADS_PER_STEP)]

    m_ref[...] = jnp.full(m_ref.shape, NEG_INF, F32)
    acc_ref[...] = jnp.zeros(acc_ref.shape, F32)
    sum_rows = jnp.where(lax.broadcasted_iota(jnp.int32, (SUM_ROWS, TK), 0) == 0, 1.0, 0.0).astype(BF16)

    def keys(j):
        return k_ref[pl.ds(pl.multiple_of(j * TK, TK), TK), :]

    def vals(j):
        vb = vt_ref[:, pl.ds(pl.multiple_of(j * TK, TK), TK)]
        return [jnp.concatenate([vb[hd * d:(hd + 1) * d], sum_rows], axis=0)
                for hd in range(HEADS_PER_STEP)]

    def scores(kb, j, chain):
        hd, mi, cols = chain
        tile = jnp.clip(i - j, 0, BIAS_TILES - 1)
        s = _dot(kb[:, hd * d:(hd + 1) * d], q_maps[hd][mi][:, cols])
        return s.astype(BF16) + bias_ref[hd, mi, tile, :, cols]

    def softmax(chain, s):
        hd, mi, cols = chain
        m_prev = m_ref[hd, mi, :, cols]
        m_new = jnp.maximum(m_prev, jnp.max(s, axis=0, keepdims=True).astype(F32))
        alpha = jnp.exp2(m_prev - m_new)
        p = jnp.exp2(s - m_new.astype(BF16))
        m_ref[hd, mi, :, cols] = m_new
        return p, alpha

    def values(vb, chain, p, alpha):
        hd, mi, cols = chain
        acc_ref[hd, mi, :, cols] = alpha * acc_ref[hd, mi, :, cols] + _dot(vb[hd], p)

    kb0 = keys(0)
    for e in range(EARLY):
        s_buf[e] = scores(kb0, 0, chains[e])
    for t in range(LATE):
        p_buf[t] = jnp.zeros((TK, TQ_HALF), BF16)
        a_buf[t] = jnp.ones((1, TQ_HALF), F32)

    def trip(j, carry):
        kb, vb = keys(j), vals(j)
        j_next = jnp.minimum(j + 1, i)
        kb_next, vb_prev = keys(j_next), vals(jnp.maximum(j - 1, 0))
        s_tiles, p_tiles = {}, {}
        for slot in range(n):
            ahead = slot + EARLY
            if ahead < n:
                s_tiles[ahead] = scores(kb, j, chains[ahead])
            else:
                s_next = scores(kb_next, j_next, chains[ahead - n])
            s = s_buf[slot] if slot < EARLY else s_tiles.pop(slot)
            p_tiles[slot] = softmax(chains[slot], s)
            if ahead >= n:
                s_buf[ahead - n] = s_next
            behind = slot - LATE
            if behind >= 0:
                values(vb, chains[behind], *p_tiles.pop(behind))
            else:
                values(vb_prev, chains[n + behind], p_buf[slot], a_buf[slot])
            if slot >= n - LATE:
                p_buf[slot - (n - LATE)], a_buf[slot - (n - LATE)] = p_tiles.pop(slot)
        return carry

    lax.fori_loop(0, i + 1, trip, 0)
    vb_last = vals(i)
    for t in range(LATE):
        values(vb_last, chains[n - LATE + t], p_buf[t], a_buf[t])

    lv = lam_ref[...]
    lam = (jnp.exp(jnp.sum(lv[0:1] * lv[1:2], keepdims=True))
           - jnp.exp(jnp.sum(lv[2:3] * lv[3:4], keepdims=True)) + lambda_init)
    for hd in range(HEADS_PER_STEP):
        ot = (acc_ref[hd, 0, 0:d, :] * (1.0 / acc_ref[hd, 0, d:d + 1, :])
              - lam * (acc_ref[hd, 1, 0:d, :] * (1.0 / acc_ref[hd, 1, d:d + 1, :])))
        inv = lax.rsqrt(jnp.mean(ot * ot, axis=0, keepdims=True) + RMS_EPS) * (1.0 - lambda_init)
        o_ref[hd * d:(hd + 1) * d, :] = (ot * inv * sg_ref[...]).astype(BF16)


def _attention(qt, k, vt, bias, lam_vecs, subln_g, lambda_init):
    nq = SEQ // TQ
    wide = HEADS_PER_STEP * 2 * HEAD_DIM
    return pl.pallas_call(
        functools.partial(_attn_kernel, lambda_init=lambda_init),
        grid=(BATCH, N_HEADS // HEADS_PER_STEP, nq),
        in_specs=[pl.BlockSpec((wide, TQ), lambda b, h, i: (h, b * nq + i)),
                  pl.BlockSpec((SEQ, wide), lambda b, h, i: (b, h), pipeline_mode=pl.Buffered(1)),
                  pl.BlockSpec((wide, SEQ), lambda b, h, i: (h, b), pipeline_mode=pl.Buffered(1)),
                  pl.BlockSpec((HEADS_PER_STEP, 2, BIAS_TILES, TK, TQ),
                               lambda b, h, i: (h, 0, 0, 0, 0), pipeline_mode=pl.Buffered(1)),
                  pl.BlockSpec((4, HEAD_DIM), lambda b, h, i: (0, 0)),
                  pl.BlockSpec((2 * HEAD_DIM, 1), lambda b, h, i: (0, 0))],
        out_specs=pl.BlockSpec((wide, TQ), lambda b, h, i: (h, b * nq + i)),
        out_shape=jax.ShapeDtypeStruct((D_MODEL, TOKENS), BF16),
        scratch_shapes=[pltpu.VMEM((HEADS_PER_STEP, 2, 1, TQ), F32),
                        pltpu.VMEM((HEADS_PER_STEP, 2, 2 * HEAD_DIM + SUM_ROWS, TQ), F32),
                        pltpu.VMEM((EARLY, TK, TQ_HALF), BF16), pltpu.VMEM((LATE, TK, TQ_HALF), BF16),
                        pltpu.VMEM((LATE, 1, TQ_HALF), F32)],
        compiler_params=_params(3),
        name="diff_attn",
    )(qt, k, vt, bias, lam_vecs, subln_g)


def _ssm_operators(lam_re, lam_im, log_dt, b_re, b_im, c_re, c_im, d_skip):
    lam_re, lam_im, log_dt, b_re, b_im, c_re, c_im, d_skip = lax.optimization_barrier(
        (lam_re, lam_im, log_dt, b_re, b_im, c_re, c_im, d_skip))
    dt = jnp.exp(log_dt)[:, None]
    zr, zi = lam_re * dt, lam_im * dt
    ks = jnp.arange(SSM_L + 1, dtype=F32)[:, None, None]
    mag = jnp.exp(ks * zr)
    pr, pi = mag * jnp.cos(ks * zi), mag * jnp.sin(ks * zi)
    nr = jnp.expm1(zr) * jnp.cos(zi) - 2.0 * jnp.sin(0.5 * zi) ** 2
    ni = pi[1]
    den = lam_re * lam_re + lam_im * lam_im
    fr, fi = (nr * lam_re + ni * lam_im) / den, (ni * lam_re - nr * lam_im) / den
    bb_re = fr[..., None] * b_re - fi[..., None] * b_im
    bb_im = fr[..., None] * b_im + fi[..., None] * b_re
    same = jnp.eye(OCT_GROUPS, dtype=F32)

    def block_rows(t):
        t = lax.optimization_barrier(t)
        return (t[:, :, :, None, :] * same[None, :, None, :, None]).reshape(OCTETS, LANES, OCT_STATE)

    b_rows = lambda t: block_rows(t.reshape(OCTETS, OCT_GROUPS, SSM_STATE, GROUP_CH)
                                  .transpose(0, 1, 3, 2))
    c_rows = lambda t: block_rows(t.reshape(OCTETS, OCT_GROUPS, GROUP_CH, SSM_STATE))
    bmat = jnp.stack([b_rows(bb_re), b_rows(bb_im)], axis=1)
    cmat = jnp.stack([c_rows(c_re), c_rows(c_im)], axis=1)

    kr = jnp.arange(SSM_L - 1, -1, -1, dtype=F32)[:, None, None]
    mag_r = jnp.exp(kr * zr)
    per_oct = lambda t: t.reshape(t.shape[:-2] + (OCTETS, OCT_STATE))
    pw_b = per_oct(jnp.stack([mag_r * jnp.cos(kr * zi), mag_r * jnp.sin(kr * zi)]))
    pw_c = per_oct(jnp.stack([pr[1:], pi[1:]]))
    a_chunk = per_oct(jnp.stack([pr[SSM_L], pi[SSM_L]]))
    return (bmat, cmat, pw_b.transpose(2, 0, 1, 3), pw_c.transpose(2, 0, 1, 3),
            a_chunk.transpose(1, 0, 2), d_skip.reshape(OCTETS, 1, LANES))


def _scaled_blocks(mat_ref, pw_ref, out_ref, im_sign):
    def block(s, carry):
        m_re, m_im = mat_ref[0], mat_ref[1]
        p_re, p_im = pw_ref[0, pl.ds(s, 1), :], pw_ref[1, pl.ds(s, 1), :]
        rows = pl.ds(pl.multiple_of(s * LANES, LANES), LANES)
        out_ref[rows, 0:OCT_STATE] = (m_re * p_re - m_im * p_im).astype(BF16)
        out_ref[rows, OCT_STATE:2 * OCT_STATE] = (im_sign * (m_re * p_im + m_im * p_re)).astype(BF16)
        return carry

    lax.fori_loop(0, SSM_L, block, 0)


def _s5_state_kernel(u_ref, bmat_ref, pw_ref, a_ref, xs_ref, bpow_scr, s_scr, x_scr):
    _scaled_blocks(bmat_ref, pw_ref, bpow_scr, 1.0)
    for r in range(0, SSM_ROWS, SSM_RT):
        s_loc = _dot(u_ref[r:r + SSM_RT, :], bpow_scr[...])
        for k in range(STATE_COLS):
            s_scr[k, r:r + SSM_RT, :] = s_loc[:, k * LANES:(k + 1) * LANES]
    a = a_ref[...]
    col = lambda r, k: jnp.broadcast_to(a[r:r + 1, k * LANES:(k + 1) * LANES], (BATCH, LANES))
    ar = [col(0, k) for k in range(RE_COLS)]
    ai = [col(1, k) for k in range(RE_COLS)]

    def step(c, carry):
        rows = pl.ds(c, BATCH, stride=SSM_NC)
        nxt_re, nxt_im = [], []
        for k in range(RE_COLS):
            xr, xi = carry[k], carry[RE_COLS + k]
            x_scr[k, rows, :] = xr
            x_scr[RE_COLS + k, rows, :] = xi
            nxt_re.append(ar[k] * xr - ai[k] * xi + s_scr[k, rows, :])
            nxt_im.append(ar[k] * xi + ai[k] * xr + s_scr[RE_COLS + k, rows, :])
        return tuple(nxt_re + nxt_im)

    zeros = jnp.zeros((BATCH, LANES), F32)
    lax.fori_loop(0, SSM_NC, step, (zeros,) * STATE_COLS, unroll=8)
    for k in range(STATE_COLS):
        xs_ref[:, k * LANES:(k + 1) * LANES] = x_scr[k].astype(BF16)


def _s5_states(u_oct, bmat, pw_b, a_chunk):
    per_oct = lambda *tail: pl.BlockSpec((None,) + tail, lambda j: (j,) + (0,) * len(tail))
    return pl.pallas_call(
        _s5_state_kernel,
        grid=(OCTETS,),
        in_specs=[per_oct(SSM_ROWS, OCT_W), per_oct(2, LANES, OCT_STATE),
                  per_oct(2, SSM_L, OCT_STATE), per_oct(2, OCT_STATE)],
        out_specs=per_oct(SSM_ROWS, 2 * OCT_STATE),
        out_shape=jax.ShapeDtypeStruct((OCTETS, SSM_ROWS, 2 * OCT_STATE), BF16),
        scratch_shapes=[pltpu.VMEM((OCT_W, 2 * OCT_STATE), BF16),
                        pltpu.VMEM((STATE_COLS, SSM_ROWS, LANES), F32),
                        pltpu.VMEM((STATE_COLS, SSM_ROWS, LANES), F32)],
        compiler_params=_params(),
        name="s5_states",
    )(u_oct, bmat, pw_b, a_chunk)


def _s5_out_kernel(u_ref, xs_ref, bmat_ref, cmat_ref, pw_ref, d_ref, y_ref, toep_scr, cpow_scr):
    @pl.when(pl.program_id(1) == 0)
    def _():
        _scaled_blocks(cmat_ref, pw_ref, cpow_scr, -1.0)
        b_cat = jnp.concatenate([bmat_ref[0], bmat_ref[1]], axis=1).astype(BF16)
        c_tau0 = jnp.concatenate([cmat_ref[0], -cmat_ref[1]], axis=1).astype(BF16)
        c_all = jnp.concatenate([c_tau0, cpow_scr[0:OCT_W - LANES, :]], axis=0)
        imp = lax.dot_general(b_cat, c_all, _NT, preferred_element_type=F32)
        on_diag = (lax.broadcasted_iota(jnp.int32, (LANES, LANES), 0)
                   == lax.broadcasted_iota(jnp.int32, (LANES, LANES), 1))
        skip = jnp.where(on_diag, d_ref[...], 0.0)
        imp = jnp.concatenate([imp[:, 0:LANES] + skip, imp[:, LANES:]], axis=1).astype(BF16)
        toep_scr[...] = jnp.zeros(toep_scr.shape, BF16)
        for s in range(SSM_L):
            toep_scr[s * LANES:(s + 1) * LANES, s * LANES:] = imp[:, :OCT_W - s * LANES]

    u, xs = u_ref[...], xs_ref[...]
    for lo in range(0, OCT_W, MXU_TILE):
        hi = lo + MXU_TILE
        y = _dot(u[:, :hi], toep_scr[0:hi, lo:hi]) + lax.dot_general(
            xs, cpow_scr[lo:hi, :], _NT, preferred_element_type=F32)
        y_ref[:, lo:hi] = y.astype(BF16)


def _s5_output(u_oct, xs, bmat, cmat, pw_c, d_oct):
    rows = lambda w: pl.BlockSpec((None, SSM_RT, w), lambda j, r: (j, r, 0))
    per_oct = lambda *tail: pl.BlockSpec((None,) + tail, lambda j, r: (j,) + (0,) * len(tail))
    return pl.pallas_call(
        _s5_out_kernel,
        grid=(OCTETS, SSM_ROWS // SSM_RT),
        in_specs=[rows(OCT_W), rows(2 * OCT_STATE), per_oct(2, LANES, OCT_STATE),
                  per_oct(2, LANES, OCT_STATE), per_oct(2, SSM_L, OCT_STATE), per_oct(1, LANES)],
        out_specs=rows(OCT_W),
        out_shape=jax.ShapeDtypeStruct((OCTETS, SSM_ROWS, OCT_W), BF16),
        scratch_shapes=[pltpu.VMEM((OCT_W, OCT_W), BF16), pltpu.VMEM((OCT_W, 2 * OCT_STATE), BF16)],
        compiler_params=_params(2),
        name="s5_output",
    )(u_oct, xs, bmat, cmat, pw_c, d_oct)


def _gelu_tanh(x):
    return 0.5 * x * (1.0 + jnp.tanh(math.sqrt(2.0 / math.pi) * (x + 0.044715 * (x * x * x))))


def _tail_stages(r, x, mixed, g_ref, wi_ref, wo_ref, p_ref, wg_ref, wp_ref, o_ref):
    x = x + _rms(mixed, g_ref[3:4, :])
    res = []
    yield from _ffn_stages(x, g_ref[4:5, :], g_ref[5:6, :], wi_ref, wo_ref, res)
    x = res[0]
    h = _rms(x, g_ref[6:7, :]).astype(BF16)
    yield
    gate = _dot(h, wg_ref[...])
    emb = _dot(p_ref[_sub_rows(r), :].astype(BF16), wp_ref[...])
    yield
    o_ref[_sub_rows(r), :] = x + _rms(_sigmoid(gate) * emb, g_ref[7:8, :])


def _tail_attn_kernel(a_ref, x_ref, wm_ref, g_ref, wi_ref, wo_ref, p_ref, wg_ref, wp_ref, o_ref):
    def stages(r):
        mixed = lax.dot_general(a_ref[:, _sub_rows(r)], wm_ref[...], (((0,), (0,)), ((), ())),
                                preferred_element_type=F32)
        yield
        yield from _tail_stages(r, x_ref[_sub_rows(r), :], mixed, g_ref, wi_ref, wo_ref, p_ref,
                                wg_ref, wp_ref, o_ref)

    _skewed(stages(r) for r in range(SUB_TILES))


def _tail_ssm_kernel(y_ref, x_ref, wm_ref, bm_ref, g_ref, wi_ref, wo_ref, p_ref, wg_ref, wp_ref,
                     o_ref, y_scr):
    def stages(r):
        act = _gelu_tanh(_from_super_rows(y_ref, r, y_scr)).astype(BF16)
        yield
        z = _dot(act, wm_ref[...]) + bm_ref[...]
        yield
        mixed = z[:, :D_MODEL] * _sigmoid(z[:, D_MODEL:])
        yield from _tail_stages(r, x_ref[_sub_rows(r), :], mixed, g_ref, wi_ref, wo_ref, p_ref,
                                wg_ref, wp_ref, o_ref)

    _skewed(stages(r) for r in range(SUB_TILES))


def _layer_tail(mix, x, p, layer, g, w_mix, b_mix, w_in, w_out, w_gate, w_proj):
    ple_specs = [pl.BlockSpec((None, TM, PLE_DIM), lambda b, c: (layer, b * TILES_PER_SEQ + c, 0)),
                 _const_spec((D_MODEL, D_MODEL)), _const_spec((PLE_DIM, D_MODEL))]
    common = dict(grid=_TOKEN_GRID, out_specs=_TOKEN_TILE,
                  out_shape=jax.ShapeDtypeStruct((TOKENS, D_MODEL), F32),
                  compiler_params=_params(2))
    if b_mix is None:
        return pl.pallas_call(
            _tail_attn_kernel,
            in_specs=[_FEATURE_TILE, _TOKEN_TILE, _const_spec((D_MODEL, D_MODEL))] + _FFN_SPECS
            + ple_specs,
            name="attn_out_ffn_ple", **common,
        )(mix, x, w_mix, g, w_in, w_out, p, w_gate, w_proj)
    return pl.pallas_call(
        _tail_ssm_kernel,
        in_specs=[_SUPER_TILE, _TOKEN_TILE, _const_spec((D_MODEL, 2 * D_MODEL)),
                  _const_spec((1, 2 * D_MODEL))] + _FFN_SPECS + ple_specs,
        scratch_shapes=[pltpu.VMEM((OCTETS, TM, LANES), F32)],
        name="s5_glu_ffn_ple", **common,
    )(mix, x, w_mix, b_mix, g, w_in, w_out, p, w_gate, w_proj)


def kernel(x, p, norm_g, ffn_w_in, ffn_w_out, attn_w_qkv, attn_w_o, attn_lam, attn_subln_g,
           rel_bias, ssm_lam_re, ssm_lam_im, ssm_log_dt, ssm_b_re, ssm_b_im, ssm_c_re, ssm_c_im,
           ssm_d, ssm_w_glu, ssm_b_glu, ple_w_proj, ple_w_gate):
    x = x.reshape(TOKENS, D_MODEL)
    p = p.reshape(DEPTH, TOKENS, PLE_DIM)
    bias = _bias_tiles(rel_bias)
    bf = lambda w: w.astype(BF16)
    for i in range(DEPTH):
        g = norm_g[i]
        j = i // N_MIXERS
        if i % N_MIXERS == 0:
            lambda_init = 0.8 - 0.6 * math.exp(-0.3 * i)
            w = bf(attn_w_qkv[j])
            x, qt, k, vt = _layer_head(x, g, bf(ffn_w_in[i, 0]), bf(ffn_w_out[i, 0]),
                                       (w[:, :D_MODEL].T, w[:, D_MODEL:2 * D_MODEL],
                                        w[:, 2 * D_MODEL:].T))
            mix = _attention(qt, k, vt, bias, attn_lam[j], attn_subln_g[j].reshape(-1, 1),
                             lambda_init)
            w_mix, b_mix = bf(attn_w_o[j]), None
        else:
            bmat, cmat, pw_b, pw_c, a_chunk, d_oct = _ssm_operators(
                ssm_lam_re[j], ssm_lam_im[j], ssm_log_dt[j], ssm_b_re[j], ssm_b_im[j],
                ssm_c_re[j], ssm_c_im[j], ssm_d[j])
            x, u_oct = _layer_head(x, g, bf(ffn_w_in[i, 0]), bf(ffn_w_out[i, 0]))
            xs = _s5_states(u_oct, bmat, pw_b, a_chunk)
            mix = _s5_output(u_oct, xs, bmat, cmat, pw_c, d_oct)
            w_mix, b_mix = bf(ssm_w_glu[j]), ssm_b_glu[j].reshape(1, -1)
        x = _layer_tail(mix, x, p, i, g, w_mix, b_mix, bf(ffn_w_in[i, 1]), bf(ffn_w_out[i, 1]),
                        bf(ple_w_gate[i]), bf(ple_w_proj[i]))
    return x.reshape(BATCH, SEQ, D_MODEL)
```

```python
import functools
import math

import numpy as np
import jax
import jax.numpy as jnp
from jax import lax
from jax.experimental import pallas as pl
from jax.experimental.pallas import tpu as pltpu

D_MODEL = 1024
BATCH = 8
SEQ = 4096
DEPTH = 4
N_MIXERS = 2
HEAD_DIM = 64
N_HEADS = D_MODEL // (2 * HEAD_DIM)
REL_BUCKETS = 32
REL_MAX_DIST = 128
GROUP_CH = 16
GROUPS = D_MODEL // GROUP_CH
SSM_STATE = 64
D_FF = 2816
FFN_RESIDUAL = 0.5
PLE_DIM = 256
N_NORMS = 8
RMS_EPS = 1e-6
NEG_INF = -1e30

TOKENS = BATCH * SEQ
F32 = jnp.float32
BF16 = jnp.bfloat16
LANES = 128

V7X_VMEM_BYTES = 64 * 1024 * 1024
VMEM_LIMIT = V7X_VMEM_BYTES - 8 * 1024 * 1024

TM = 512
TILES_PER_SEQ = SEQ // TM
SUB_TILES = 2
SUB_TM = TM // SUB_TILES
MXU_TILE = 256
FF_SPLITS = (0, 4 * MXU_TILE, 8 * MXU_TILE, D_FF)
assert D_FF % MXU_TILE == 0
TQ = 512
TK = 512
BIAS_TILES = 3
TQ_HALF = TQ // 2
SUM_ROWS = 16
HEADS_PER_STEP = 8
EARLY = 2
LATE = 2
LOG2E = math.log2(math.e)

SSM_L = 16
SSM_NC = SEQ // SSM_L
OCTETS = D_MODEL // LANES
OCT_GROUPS = LANES // GROUP_CH
OCT_W = SSM_L * LANES
OCT_STATE = OCT_GROUPS * SSM_STATE
SSM_ROWS = BATCH * SSM_NC
SSM_RT = 1024
CH_PER_TILE = TM // SSM_L
SUB_CH = SUB_TM // SSM_L
RE_COLS = OCT_STATE // LANES
STATE_COLS = 2 * RE_COLS


def _t5_thresholds():
    n = np.arange(0, 4 * REL_MAX_DIST)
    max_exact = REL_BUCKETS // 2
    nf = np.maximum(n, 1).astype(np.float64)
    large = max_exact + (np.log(nf / max_exact) / math.log(REL_MAX_DIST / max_exact)
                         * (REL_BUCKETS - max_exact)).astype(np.int32)
    bucket = np.where(n < max_exact, n, np.minimum(large, REL_BUCKETS - 1))
    return [int(np.argmax(bucket >= j)) for j in range(1, REL_BUCKETS)]


T5_THRESHOLDS = _t5_thresholds()
assert TQ == TK and T5_THRESHOLDS[-1] <= TK, "key blocks before i-1 sit in the last bucket"

_NT = (((1,), (1,)), ((), ()))


def _const_spec(shape):
    nd = len(shape)
    return pl.BlockSpec(shape, lambda *_: (0,) * nd, pipeline_mode=pl.Buffered(1))


def _params(n_axes=1):
    return pltpu.CompilerParams(dimension_semantics=("arbitrary",) * n_axes,
                                vmem_limit_bytes=VMEM_LIMIT)


def _rms(x, g):
    return x * lax.rsqrt(jnp.mean(x * x, axis=-1, keepdims=True) + RMS_EPS) * g


def _sigmoid(x):
    return 1.0 / (1.0 + jnp.exp(-x))


def _dot(a, b):
    return jnp.dot(a, b, preferred_element_type=F32)


_TOKEN_TILE = pl.BlockSpec((TM, D_MODEL), lambda b, c: (b * TILES_PER_SEQ + c, 0))
_FEATURE_TILE = pl.BlockSpec((D_MODEL, TM), lambda b, c: (0, b * TILES_PER_SEQ + c))
_SUPER_TILE = pl.BlockSpec((OCTETS, CH_PER_TILE, OCT_W),
                           lambda b, c: (0, b * TILES_PER_SEQ + c, 0))
_TOKEN_GRID = (BATCH, TILES_PER_SEQ)
_FFN_SPECS = [_const_spec((N_NORMS, D_MODEL)), _const_spec((D_MODEL, 2 * D_FF)),
              _const_spec((D_FF, D_MODEL))]


def _skewed(stage_gens):
    waiting, running = list(stage_gens), []
    while waiting or running:
        if waiting:
            running.append(waiting.pop(0))
        for gen in list(running):
            if next(gen, _DONE) is _DONE:
                running.remove(gen)


_DONE = object()


def _sub_rows(r):
    return pl.ds(r * SUB_TM, SUB_TM)


def _ffn_stages(x, g_pre, g_post, wi_ref, wo_ref, out):
    h = _rms(x, g_pre).astype(BF16)
    yield
    y = None
    for lo, hi in zip(FF_SPLITS[:-1], FF_SPLITS[1:]):
        a = _dot(h, wi_ref[:, lo:hi])
        u = _dot(h, wi_ref[:, D_FF + lo:D_FF + hi])
        yield
        act = (a * _sigmoid(a) * u).astype(BF16)
        yield
        down = _dot(act, wo_ref[lo:hi, :])
        y = down if y is None else y + down
        yield
    out.append(x + FFN_RESIDUAL * _rms(y, g_post))


def _to_super_rows(h, r, o_ref, h_scr):
    chunks = slice(r * SUB_CH, (r + 1) * SUB_CH)
    for j in range(OCTETS):
        h_scr[j, _sub_rows(r), :] = h[:, j * LANES:(j + 1) * LANES]
    for j in range(OCTETS):
        for s in range(SSM_L):
            rows = h_scr[j, pl.ds(r * SUB_TM + s, SUB_CH, stride=SSM_L), :]
            o_ref[j, chunks, s * LANES:(s + 1) * LANES] = rows.astype(BF16)


def _from_super_rows(y_ref, r, y_scr):
    chunks = slice(r * SUB_CH, (r + 1) * SUB_CH)
    for j in range(OCTETS):
        for s in range(SSM_L):
            y_scr[j, pl.ds(r * SUB_TM + s, SUB_CH, stride=SSM_L), :] = (
                y_ref[j, chunks, s * LANES:(s + 1) * LANES].astype(F32))
    return jnp.concatenate([y_scr[j, _sub_rows(r), :] for j in range(OCTETS)], axis=1)


def _head_stages(r, x_ref, g_ref, wi_ref, wo_ref, xo_ref, out):
    res = []
    yield from _ffn_stages(x_ref[_sub_rows(r), :], g_ref[0:1, :], g_ref[1:2, :], wi_ref, wo_ref,
                           res)
    xo_ref[_sub_rows(r), :] = res[0]
    out.append(_rms(res[0], g_ref[2:3, :]))
    yield


def _head_attn_kernel(x_ref, g_ref, wi_ref, wo_ref, wqt_ref, wk_ref, wvt_ref,
                      xo_ref, qt_ref, k_ref, vt_ref):
    def stages(r):
        res = []
        yield from _head_stages(r, x_ref, g_ref, wi_ref, wo_ref, xo_ref, res)
        h = res[0].astype(BF16)
        cols = _sub_rows(r)
        qt = lax.dot_general(wqt_ref[...], h, _NT, preferred_element_type=F32)
        qt_ref[:, cols] = (qt * (HEAD_DIM ** -0.5 * LOG2E)).astype(BF16)
        k_ref[cols, :] = _dot(h, wk_ref[...]).astype(BF16)
        vt = lax.dot_general(wvt_ref[...], h, _NT, preferred_element_type=F32)
        vt_ref[:, cols] = vt.astype(BF16)

    _skewed(stages(r) for r in range(SUB_TILES))


def _head_ssm_kernel(x_ref, g_ref, wi_ref, wo_ref, xo_ref, u_ref, h_scr):
    def stages(r):
        res = []
        yield from _head_stages(r, x_ref, g_ref, wi_ref, wo_ref, xo_ref, res)
        _to_super_rows(res[0], r, u_ref, h_scr)

    _skewed(stages(r) for r in range(SUB_TILES))


def _layer_head(x, g, w_in, w_out, qkv=None):
    x_out = jax.ShapeDtypeStruct((TOKENS, D_MODEL), F32)
    if qkv is None:
        return pl.pallas_call(
            _head_ssm_kernel,
            grid=_TOKEN_GRID,
            in_specs=[_TOKEN_TILE] + _FFN_SPECS,
            out_specs=(_TOKEN_TILE, _SUPER_TILE),
            out_shape=(x_out, jax.ShapeDtypeStruct((OCTETS, SSM_ROWS, OCT_W), BF16)),
            scratch_shapes=[pltpu.VMEM((OCTETS, TM, LANES), F32)],
            compiler_params=_params(2),
            name="ffn_s5in",
        )(x, g, w_in, w_out)
    feat = jax.ShapeDtypeStruct((D_MODEL, TOKENS), BF16)
    return pl.pallas_call(
        _head_attn_kernel,
        grid=_TOKEN_GRID,
        in_specs=[_TOKEN_TILE] + _FFN_SPECS + [_const_spec((D_MODEL, D_MODEL))] * 3,
        out_specs=(_TOKEN_TILE, _FEATURE_TILE, _TOKEN_TILE, _FEATURE_TILE),
        out_shape=(x_out, feat, jax.ShapeDtypeStruct((TOKENS, D_MODEL), BF16), feat),
        compiler_params=_params(2),
        name="ffn_qkv",
    )(x, g, w_in, w_out, *qkv)


def _bias_kernel(rel_ref, o_ref):
    c = pl.program_id(0)
    ki = lax.broadcasted_iota(jnp.int32, (LANES, LANES), 0)
    qi = lax.broadcasted_iota(jnp.int32, (LANES, LANES), 1)
    bands = {}

    def band(lag):
        if lag not in bands:
            d = qi - ki + lag
            val = jnp.full((LANES, LANES), rel_ref[0, c], F32)
            for j, thr in enumerate(T5_THRESHOLDS, start=1):
                val = jnp.where(d >= thr, rel_ref[j, c], val)
            val = (val - rel_ref[REL_BUCKETS - 1, c]) * LOG2E
            bands[lag] = jnp.where(d >= 0, val, NEG_INF).astype(BF16)
        return bands[lag]

    zeros = jnp.zeros((LANES, LANES), BF16)
    masked = jnp.full((LANES, LANES), NEG_INF, BF16)
    for r in range(BIAS_TILES):
        for kb in range(TK // LANES):
            for qb in range(TQ // LANES):
                lag = (qb - kb) * LANES + r * TK
                if lag + LANES - 1 < 0:
                    blk = masked
                elif lag - (LANES - 1) >= T5_THRESHOLDS[-1]:
                    blk = zeros
                else:
                    blk = band(lag)
                o_ref[r, kb * LANES:(kb + 1) * LANES, qb * LANES:(qb + 1) * LANES] = blk


def _bias_tiles(rel_bias):
    tiles = pl.pallas_call(
        _bias_kernel,
        grid=(2 * N_HEADS,),
        in_specs=[pl.BlockSpec(memory_space=pltpu.SMEM)],
        out_specs=pl.BlockSpec((None, BIAS_TILES, TK, TQ), lambda c: (c, 0, 0, 0)),
        out_shape=jax.ShapeDtypeStruct((2 * N_HEADS, BIAS_TILES, TK, TQ), BF16),
        compiler_params=_params(1),
        name="t5_bias",
    )(rel_bias)
    return tiles.reshape(N_HEADS, 2, BIAS_TILES, TK, TQ)


def _attn_kernel(qt_ref, k_ref, vt_ref, bias_ref, lam_ref, sg_ref, o_ref, m_ref, acc_ref,
                 s_buf, p_buf, a_buf, *, lambda_init):
    i = pl.program_id(2)
    d = 2 * HEAD_DIM
    chains = [(hd, mi, slice(hf * TQ_HALF, (hf + 1) * TQ_HALF))
              for hd in range(HEADS_PER_STEP) for hf in range(2) for mi in range(2)]
    n = len(chains)

    qt = qt_ref[...]
    row = lax.broadcasted_iota(jnp.int32, (d, TQ), 0)
    zero = jnp.zeros((d, TQ), BF16)
    q_maps = [[jnp.where(row < HEAD_DIM, qt[hd * d:(hd + 1) * d], zero),
               jnp.where(row >= HEAD_DIM, qt[hd * d:(hd + 1) * d], zero)]
              for hd in range(HEADS_PER_STEP)]

    m_ref[...] = jnp.full(m_ref.shape, NEG_INF, F32)
    acc_ref[...] = jnp.zeros(acc_ref.shape, F32)
    sum_rows = jnp.where(lax.broadcasted_iota(jnp.int32, (SUM_ROWS, TK), 0) == 0, 1.0, 0.0).astype(BF16)

    def keys(j):
        return k_ref[pl.ds(pl.multiple_of(j * TK, TK), TK), :]

    def vals(j):
        vb = vt_ref[:, pl.ds(pl.multiple_of(j * TK, TK), TK)]
        return [jnp.concatenate([vb[hd * d:(hd + 1) * d], sum_rows], axis=0)
                for hd in range(HEADS_PER_STEP)]

    def scores(kb, j, chain):
        hd, mi, cols = chain
        tile = jnp.clip(i - j, 0, BIAS_TILES - 1)
        s = _dot(kb[:, hd * d:(hd + 1) * d], q_maps[hd][mi][:, cols])
        return s.astype(BF16) + bias_ref[hd, mi, tile, :, cols]

    def softmax(chain, s):
        hd, mi, cols = chain
        m_prev = m_ref[hd, mi, :, cols]
        m_new = jnp.maximum(m_prev, jnp.max(s, axis=0, keepdims=True).astype(F32))
        alpha = jnp.exp2(m_prev - m_new)
        p = jnp.exp2(s - m_new.astype(BF16))
        m_ref[hd, mi, :, cols] = m_new
        return p, alpha

    def values(vb, chain, p, alpha):
        hd, mi, cols = chain
        acc_ref[hd, mi, :, cols] = alpha * acc_ref[hd, mi, :, cols] + _dot(vb[hd], p)

    kb0 = keys(0)
    for e in range(EARLY):
        s_buf[e] = scores(kb0, 0, chains[e])
    for t in range(LATE):
        p_buf[t] = jnp.zeros((TK, TQ_HALF), BF16)
        a_buf[t] = jnp.ones((1, TQ_HALF), F32)

    def trip(j, carry):
        kb, vb = keys(j), vals(j)
        j_next = jnp.minimum(j + 1, i)
        kb_next, vb_prev = keys(j_next), vals(jnp.maximum(j - 1, 0))
        s_tiles, p_tiles = {}, {}
        for slot in range(n):
            ahead = slot + EARLY
            if ahead < n:
                s_tiles[ahead] = scores(kb, j, chains[ahead])
            else:
                s_next = scores(kb_next, j_next, chains[ahead - n])
            s = s_buf[slot] if slot < EARLY else s_tiles.pop(slot)
            p_tiles[slot] = softmax(chains[slot], s)
            if ahead >= n:
                s_buf[ahead - n] = s_next
            behind = slot - LATE
            if behind >= 0:
                values(vb, chains[behind], *p_tiles.pop(behind))
            else:
                values(vb_prev, chains[n + behind], p_buf[slot], a_buf[slot])
            if slot >= n - LATE:
                p_buf[slot - (n - LATE)], a_buf[slot - (n - LATE)] = p_tiles.pop(slot)
        return carry

    lax.fori_loop(0, i + 1, trip, 0)
    vb_last = vals(i)
    for t in range(LATE):
        values(vb_last, chains[n - LATE + t], p_buf[t], a_buf[t])

    lv = lam_ref[...]
    lam = (jnp.exp(jnp.sum(lv[0:1] * lv[1:2], keepdims=True))
           - jnp.exp(jnp.sum(lv[2:3] * lv[3:4], keepdims=True)) + lambda_init)
    for hd in range(HEADS_PER_STEP):
        ot = (acc_ref[hd, 0, 0:d, :] * (1.0 / acc_ref[hd, 0, d:d + 1, :])
              - lam * (acc_ref[hd, 1, 0:d, :] * (1.0 / acc_ref[hd, 1, d:d + 1, :])))
        inv = lax.rsqrt(jnp.mean(ot * ot, axis=0, keepdims=True) + RMS_EPS) * (1.0 - lambda_init)
        o_ref[hd * d:(hd + 1) * d, :] = (ot * inv * sg_ref[...]).astype(BF16)


def _attention(qt, k, vt, bias, lam_vecs, subln_g, lambda_init):
    nq = SEQ // TQ
    wide = HEADS_PER_STEP * 2 * HEAD_DIM
    return pl.pallas_call(
        functools.partial(_attn_kernel, lambda_init=lambda_init),
        grid=(BATCH, N_HEADS // HEADS_PER_STEP, nq),
        in_specs=[pl.BlockSpec((wide, TQ), lambda b, h, i: (h, b * nq + i)),
                  pl.BlockSpec((SEQ, wide), lambda b, h, i: (b, h), pipeline_mode=pl.Buffered(1)),
                  pl.BlockSpec((wide, SEQ), lambda b, h, i: (h, b), pipeline_mode=pl.Buffered(1)),
                  pl.BlockSpec((HEADS_PER_STEP, 2, BIAS_TILES, TK, TQ),
                               lambda b, h, i: (h, 0, 0, 0, 0), pipeline_mode=pl.Buffered(1)),
                  pl.BlockSpec((4, HEAD_DIM), lambda b, h, i: (0, 0)),
                  pl.BlockSpec((2 * HEAD_DIM, 1), lambda b, h, i: (0, 0))],
        out_specs=pl.BlockSpec((wide, TQ), lambda b, h, i: (h, b * nq + i)),
        out_shape=jax.ShapeDtypeStruct((D_MODEL, TOKENS), BF16),
        scratch_shapes=[pltpu.VMEM((HEADS_PER_STEP, 2, 1, TQ), F32),
                        pltpu.VMEM((HEADS_PER_STEP, 2, 2 * HEAD_DIM + SUM_ROWS, TQ), F32),
                        pltpu.VMEM((EARLY, TK, TQ_HALF), BF16), pltpu.VMEM((LATE, TK, TQ_HALF), BF16),
                        pltpu.VMEM((LATE, 1, TQ_HALF), F32)],
        compiler_params=_params(3),
        name="diff_attn",
    )(qt, k, vt, bias, lam_vecs, subln_g)


def _ssm_operators(lam_re, lam_im, log_dt, b_re, b_im, c_re, c_im, d_skip):
    lam_re, lam_im, log_dt, b_re, b_im, c_re, c_im, d_skip = lax.optimization_barrier(
        (lam_re, lam_im, log_dt, b_re, b_im, c_re, c_im, d_skip))
    dt = jnp.exp(log_dt)[:, None]
    zr, zi = lam_re * dt, lam_im * dt
    ks = jnp.arange(SSM_L + 1, dtype=F32)[:, None, None]
    mag = jnp.exp(ks * zr)
    pr, pi = mag * jnp.cos(ks * zi), mag * jnp.sin(ks * zi)
    nr = jnp.expm1(zr) * jnp.cos(zi) - 2.0 * jnp.sin(0.5 * zi) ** 2
    ni = pi[1]
    den = lam_re * lam_re + lam_im * lam_im
    fr, fi = (nr * lam_re + ni * lam_im) / den, (ni * lam_re - nr * lam_im) / den
    bb_re = fr[..., None] * b_re - fi[..., None] * b_im
    bb_im = fr[..., None] * b_im + fi[..., None] * b_re
    same = jnp.eye(OCT_GROUPS, dtype=F32)

    def block_rows(t):
        t = lax.optimization_barrier(t)
        return (t[:, :, :, None, :] * same[None, :, None, :, None]).reshape(OCTETS, LANES, OCT_STATE)

    b_rows = lambda t: block_rows(t.reshape(OCTETS, OCT_GROUPS, SSM_STATE, GROUP_CH)
                                  .transpose(0, 1, 3, 2))
    c_rows = lambda t: block_rows(t.reshape(OCTETS, OCT_GROUPS, GROUP_CH, SSM_STATE))
    bmat = jnp.stack([b_rows(bb_re), b_rows(bb_im)], axis=1)
    cmat = jnp.stack([c_rows(c_re), c_rows(c_im)], axis=1)

    kr = jnp.arange(SSM_L - 1, -1, -1, dtype=F32)[:, None, None]
    mag_r = jnp.exp(kr * zr)
    per_oct = lambda t: t.reshape(t.shape[:-2] + (OCTETS, OCT_STATE))
    pw_b = per_oct(jnp.stack([mag_r * jnp.cos(kr * zi), mag_r * jnp.sin(kr * zi)]))
    pw_c = per_oct(jnp.stack([pr[1:], pi[1:]]))
    a_chunk = per_oct(jnp.stack([pr[SSM_L], pi[SSM_L]]))
    return (bmat, cmat, pw_b.transpose(2, 0, 1, 3), pw_c.transpose(2, 0, 1, 3),
            a_chunk.transpose(1, 0, 2), d_skip.reshape(OCTETS, 1, LANES))


def _scaled_blocks(mat_ref, pw_ref, out_ref, im_sign):
    def block(s, carry):
        m_re, m_im = mat_ref[0], mat_ref[1]
        p_re, p_im = pw_ref[0, pl.ds(s, 1), :], pw_ref[1, pl.ds(s, 1), :]
        rows = pl.ds(pl.multiple_of(s * LANES, LANES), LANES)
        out_ref[rows, 0:OCT_STATE] = (m_re * p_re - m_im * p_im).astype(BF16)
        out_ref[rows, OCT_STATE:2 * OCT_STATE] = (im_sign * (m_re * p_im + m_im * p_re)).astype(BF16)
        return carry

    lax.fori_loop(0, SSM_L, block, 0)


def _s5_state_kernel(u_ref, bmat_ref, pw_ref, a_ref, xs_ref, bpow_scr, s_scr, x_scr):
    _scaled_blocks(bmat_ref, pw_ref, bpow_scr, 1.0)
    for r in range(0, SSM_ROWS, SSM_RT):
        s_loc = _dot(u_ref[r:r + SSM_RT, :], bpow_scr[...])
        for k in range(STATE_COLS):
            s_scr[k, r:r + SSM_RT, :] = s_loc[:, k * LANES:(k + 1) * LANES]
    a = a_ref[...]
    col = lambda r, k: jnp.broadcast_to(a[r:r + 1, k * LANES:(k + 1) * LANES], (BATCH, LANES))
    ar = [col(0, k) for k in range(RE_COLS)]
    ai = [col(1, k) for k in range(RE_COLS)]

    def step(c, carry):
        rows = pl.ds(c, BATCH, stride=SSM_NC)
        nxt_re, nxt_im = [], []
        for k in range(RE_COLS):
            xr, xi = carry[k], carry[RE_COLS + k]
            x_scr[k, rows, :] = xr
            x_scr[RE_COLS + k, rows, :] = xi
            nxt_re.append(ar[k] * xr - ai[k] * xi + s_scr[k, rows, :])
            nxt_im.append(ar[k] * xi + ai[k] * xr + s_scr[RE_COLS + k, rows, :])
        return tuple(nxt_re + nxt_im)

    zeros = jnp.zeros((BATCH, LANES), F32)
    lax.fori_loop(0, SSM_NC, step, (zeros,) * STATE_COLS, unroll=8)
    for k in range(STATE_COLS):
        xs_ref[:, k * LANES:(k + 1) * LANES] = x_scr[k].astype(BF16)


def _s5_states(u_oct, bmat, pw_b, a_chunk):
    per_oct = lambda *tail: pl.BlockSpec((None,) + tail, lambda j: (j,) + (0,) * len(tail))
    return pl.pallas_call(
        _s5_state_kernel,
        grid=(OCTETS,),
        in_specs=[per_oct(SSM_ROWS, OCT_W), per_oct(2, LANES, OCT_STATE),
                  per_oct(2, SSM_L, OCT_STATE), per_oct(2, OCT_STATE)],
        out_specs=per_oct(SSM_ROWS, 2 * OCT_STATE),
        out_shape=jax.ShapeDtypeStruct((OCTETS, SSM_ROWS, 2 * OCT_STATE), BF16),
        scratch_shapes=[pltpu.VMEM((OCT_W, 2 * OCT_STATE), BF16),
                        pltpu.VMEM((STATE_COLS, SSM_ROWS, LANES), F32),
                        pltpu.VMEM((STATE_COLS, SSM_ROWS, LANES), F32)],
        compiler_params=_params(),
        name="s5_states",
    )(u_oct, bmat, pw_b, a_chunk)


def _s5_out_kernel(u_ref, xs_ref, bmat_ref, cmat_ref, pw_ref, d_ref, y_ref, toep_scr, cpow_scr):
    @pl.when(pl.program_id(1) == 0)
    def _():
        _scaled_blocks(cmat_ref, pw_ref, cpow_scr, -1.0)
        b_cat = jnp.concatenate([bmat_ref[0], bmat_ref[1]], axis=1).astype(BF16)
        c_tau0 = jnp.concatenate([cmat_ref[0], -cmat_ref[1]], axis=1).astype(BF16)
        c_all = jnp.concatenate([c_tau0, cpow_scr[0:OCT_W - LANES, :]], axis=0)
        imp = lax.dot_general(b_cat, c_all, _NT, preferred_element_type=F32)
        on_diag = (lax.broadcasted_iota(jnp.int32, (LANES, LANES), 0)
                   == lax.broadcasted_iota(jnp.int32, (LANES, LANES), 1))
        skip = jnp.where(on_diag, d_ref[...], 0.0)
        imp = jnp.concatenate([imp[:, 0:LANES] + skip, imp[:, LANES:]], axis=1).astype(BF16)
        toep_scr[...] = jnp.zeros(toep_scr.shape, BF16)
        for s in range(SSM_L):
            toep_scr[s * LANES:(s + 1) * LANES, s * LANES:] = imp[:, :OCT_W - s * LANES]

    u, xs = u_ref[...], xs_ref[...]
    for lo in range(0, OCT_W, MXU_TILE):
        hi = lo + MXU_TILE
        y = _dot(u[:, :hi], toep_scr[0:hi, lo:hi]) + lax.dot_general(
            xs, cpow_scr[lo:hi, :], _NT, preferred_element_type=F32)
        y_ref[:, lo:hi] = y.astype(BF16)


def _s5_output(u_oct, xs, bmat, cmat, pw_c, d_oct):
    rows = lambda w: pl.BlockSpec((None, SSM_RT, w), lambda j, r: (j, r, 0))
    per_oct = lambda *tail: pl.BlockSpec((None,) + tail, lambda j, r: (j,) + (0,) * len(tail))
    return pl.pallas_call(
        _s5_out_kernel,
        grid=(OCTETS, SSM_ROWS // SSM_RT),
        in_specs=[rows(OCT_W), rows(2 * OCT_STATE), per_oct(2, LANES, OCT_STATE),
                  per_oct(2, LANES, OCT_STATE), per_oct(2, SSM_L, OCT_STATE), per_oct(1, LANES)],
        out_specs=rows(OCT_W),
        out_shape=jax.ShapeDtypeStruct((OCTETS, SSM_ROWS, OCT_W), BF16),
        scratch_shapes=[pltpu.VMEM((OCT_W, OCT_W), BF16), pltpu.VMEM((OCT_W, 2 * OCT_STATE), BF16)],
        compiler_params=_params(2),
        name="s5_output",
    )(u_oct, xs, bmat, cmat, pw_c, d_oct)


def _gelu_tanh(x):
    return 0.5 * x * (1.0 + jnp.tanh(math.sqrt(2.0 / math.pi) * (x + 0.044715 * (x * x * x))))


def _tail_stages(r, x, mixed, g_ref, wi_ref, wo_ref, p_ref, wg_ref, wp_ref, o_ref):
    x = x + _rms(mixed, g_ref[3:4, :])
    res = []
    yield from _ffn_stages(x, g_ref[4:5, :], g_ref[5:6, :], wi_ref, wo_ref, res)
    x = res[0]
    h = _rms(x, g_ref[6:7, :]).astype(BF16)
    yield
    gate = _dot(h, wg_ref[...])
    emb = _dot(p_ref[_sub_rows(r), :].astype(BF16), wp_ref[...])
    yield
    o_ref[_sub_rows(r), :] = x + _rms(_sigmoid(gate) * emb, g_ref[7:8, :])


def _tail_attn_kernel(a_ref, x_ref, wm_ref, g_ref, wi_ref, wo_ref, p_ref, wg_ref, wp_ref, o_ref):
    def stages(r):
        mixed = lax.dot_general(a_ref[:, _sub_rows(r)], wm_ref[...], (((0,), (0,)), ((), ())),
                                preferred_element_type=F32)
        yield
        yield from _tail_stages(r, x_ref[_sub_rows(r), :], mixed, g_ref, wi_ref, wo_ref, p_ref,
                                wg_ref, wp_ref, o_ref)

    _skewed(stages(r) for r in range(SUB_TILES))


def _tail_ssm_kernel(y_ref, x_ref, wm_ref, bm_ref, g_ref, wi_ref, wo_ref, p_ref, wg_ref, wp_ref,
                     o_ref, y_scr):
    def stages(r):
        act = _gelu_tanh(_from_super_rows(y_ref, r, y_scr)).astype(BF16)
        yield
        z = _dot(act, wm_ref[...]) + bm_ref[...]
        yield
        mixed = z[:, :D_MODEL] * _sigmoid(z[:, D_MODEL:])
        yield from _tail_stages(r, x_ref[_sub_rows(r), :], mixed, g_ref, wi_ref, wo_ref, p_ref,
                                wg_ref, wp_ref, o_ref)

    _skewed(stages(r) for r in range(SUB_TILES))


def _layer_tail(mix, x, p, layer, g, w_mix, b_mix, w_in, w_out, w_gate, w_proj):
    ple_specs = [pl.BlockSpec((None, TM, PLE_DIM), lambda b, c: (layer, b * TILES_PER_SEQ + c, 0)),
                 _const_spec((D_MODEL, D_MODEL)), _const_spec((PLE_DIM, D_MODEL))]
    common = dict(grid=_TOKEN_GRID, out_specs=_TOKEN_TILE,
                  out_shape=jax.ShapeDtypeStruct((TOKENS, D_MODEL), F32),
                  compiler_params=_params(2))
    if b_mix is None:
        return pl.pallas_call(
            _tail_attn_kernel,
            in_specs=[_FEATURE_TILE, _TOKEN_TILE, _const_spec((D_MODEL, D_MODEL))] + _FFN_SPECS
            + ple_specs,
            name="attn_out_ffn_ple", **common,
        )(mix, x, w_mix, g, w_in, w_out, p, w_gate, w_proj)
    return pl.pallas_call(
        _tail_ssm_kernel,
        in_specs=[_SUPER_TILE, _TOKEN_TILE, _const_spec((D_MODEL, 2 * D_MODEL)),
                  _const_spec((1, 2 * D_MODEL))] + _FFN_SPECS + ple_specs,
        scratch_shapes=[pltpu.VMEM((OCTETS, TM, LANES), F32)],
        name="s5_glu_ffn_ple", **common,
    )(mix, x, w_mix, b_mix, g, w_in, w_out, p, w_gate, w_proj)


def kernel(x, p, norm_g, ffn_w_in, ffn_w_out, attn_w_qkv, attn_w_o, attn_lam, attn_subln_g,
           rel_bias, ssm_lam_re, ssm_lam_im, ssm_log_dt, ssm_b_re, ssm_b_im, ssm_c_re, ssm_c_im,
           ssm_d, ssm_w_glu, ssm_b_glu, ple_w_proj, ple_w_gate):
    x = x.reshape(TOKENS, D_MODEL)
    p = p.reshape(DEPTH, TOKENS, PLE_DIM)
    bias = _bias_tiles(rel_bias)
    bf = lambda w: w.astype(BF16)
    for i in range(DEPTH):
        g = norm_g[i]
        j = i // N_MIXERS
        if i % N_MIXERS == 0:
            lambda_init = 0.8 - 0.6 * math.exp(-0.3 * i)
            w = bf(attn_w_qkv[j])
            x, qt, k, vt = _layer_head(x, g, bf(ffn_w_in[i, 0]), bf(ffn_w_out[i, 0]),
                                       (w[:, :D_MODEL].T, w[:, D_MODEL:2 * D_MODEL],
                                        w[:, 2 * D_MODEL:].T))
            mix = _attention(qt, k, vt, bias, attn_lam[j], attn_subln_g[j].reshape(-1, 1),
                             lambda_init)
            w_mix, b_mix = bf(attn_w_o[j]), None
        else:
            bmat, cmat, pw_b, pw_c, a_chunk, d_oct = _ssm_operators(
                ssm_lam_re[j], ssm_lam_im[j], ssm_log_dt[j], ssm_b_re[j], ssm_b_im[j],
                ssm_c_re[j], ssm_c_im[j], ssm_d[j])
            x, u_oct = _layer_head(x, g, bf(ffn_w_in[i, 0]), bf(ffn_w_out[i, 0]))
            xs = _s5_states(u_oct, bmat, pw_b, a_chunk)
            mix = _s5_output(u_oct, xs, bmat, cmat, pw_c, d_oct)
            w_mix, b_mix = bf(ssm_w_glu[j]), ssm_b_glu[j].reshape(1, -1)
        x = _layer_tail(mix, x, p, i, g, w_mix, b_mix, bf(ffn_w_in[i, 1]), bf(ffn_w_out[i, 1]),
                        bf(ple_w_gate[i]), bf(ple_w_proj[i]))
    return x.reshape(BATCH, SEQ, D_MODEL)
```
